```python
import jax, jax.numpy as jnp
from jax import lax
import numpy as np

D_MODEL = 1024
BATCH = 4
SEQ = 4096
DEPTH = 2

GRID_W = 64
CTX_LEN = 256
N_EVEN = (DEPTH + 1) // 2
N_ODD = DEPTH // 2
N_MOD = 9
D_FF = 2816
EPS = 1e-6
CONV_DIM = D_MODEL // 2
CONV_WIDTH = 31
CONV_PAD = (CONV_WIDTH - 1) // 2
HEAD_DIM = 64
ATT_HEADS = 8
ATT_KV_HEADS = 2
ATT_GROUP = ATT_HEADS // ATT_KV_HEADS
ATT_DIM = ATT_HEADS * HEAD_DIM
KV_DIM = ATT_KV_HEADS * HEAD_DIM
WINDOW = 128
BLOCK = 128
ROPE_BASE = 10000.0
ROPE_FREQS = HEAD_DIM // 4
Q_OFF = 2 * CONV_DIM
K_OFF = Q_OFF + ATT_DIM
V_OFF = K_OFF + KV_DIM
EVEN_IN = V_OFF + KV_DIM
EVEN_MIX = CONV_DIM + ATT_DIM
ML_HEADS = 4
ML_HEAD_DIM = D_MODEL // ML_HEADS
ML_DIM = ML_HEADS * ML_HEAD_DIM
ML_CHUNK = 128
ODD_IN = 4 * ML_DIM + 4 * ML_HEADS
FORGET_BIAS = 3.0

kernel_name = "hybrid_conv_swa_mlstm_dit_prefix"


def rmsnorm(x, g):
    xf = x.astype(jnp.float32)
    y = xf * lax.rsqrt(jnp.mean(xf * xf, axis=-1, keepdims=True) + EPS)
    return (y * g.astype(jnp.float32)).astype(x.dtype)


def layernorm(x, g, b):
    xf = x.astype(jnp.float32)
    mu = jnp.mean(xf, axis=-1, keepdims=True)
    var = jnp.mean(jnp.square(xf - mu), axis=-1, keepdims=True)
    return ((xf - mu) * lax.rsqrt(var + EPS) * g.astype(jnp.float32) + b.astype(jnp.float32)).astype(x.dtype)


def ada_params(cond, w, b):
    m = (jax.nn.silu(cond) @ w + b)[..., None, :]
    return jnp.split(m, N_MOD, axis=-1)


def modulate(x, shift, scale):
    return x * (1 + scale) + shift


def ffn_half(x, mods, g, w_gu, w_d):
    shift, scale, gate = mods
    h = modulate(rmsnorm(x, g), shift, scale) @ w_gu
    hg, hu = jnp.split(h, 2, axis=-1)
    return x + 0.5 * gate * ((jax.nn.silu(hg) * hu) @ w_d)


def axial_rope_tables(n):
    rows = n // GRID_W
    row = jnp.repeat(jnp.arange(rows, dtype=jnp.float32), GRID_W)
    col = jnp.tile(jnp.arange(GRID_W, dtype=jnp.float32), rows)
    inv = ROPE_BASE ** (-jnp.arange(ROPE_FREQS, dtype=jnp.float32) / ROPE_FREQS)
    ang = jnp.concatenate([row[:, None] * inv, col[:, None] * inv], axis=-1)
    return jnp.cos(ang), jnp.sin(ang)


def apply_rope(x, cos, sin):
    n = x.shape[1]
    xr = x.astype(jnp.float32).reshape(x.shape[:-1] + (2, 2, ROPE_FREQS))
    x1, x2 = xr[..., 0, :], xr[..., 1, :]
    c = cos.reshape(n, 1, 2, ROPE_FREQS)
    s = sin.reshape(n, 1, 2, ROPE_FREQS)
    out = jnp.stack([x1 * c - x2 * s, x2 * c + x1 * s], axis=-2)
    return out.reshape(x.shape).astype(x.dtype)


def conv_module(val, gate, w_dw, b_dw, ln_g, ln_b):
    u = val * jax.nn.sigmoid(gate)
    u = lax.conv_general_dilated(u, w_dw[:, None, :], window_strides=(1,), padding=[(CONV_PAD, CONV_PAD)],
                                 dimension_numbers=("NWC", "WIO", "NWC"), feature_group_count=CONV_DIM) + b_dw
    return jax.nn.silu(layernorm(u, ln_g, ln_b))


def banded(t, nb):
    tb = t.reshape(t.shape[0], nb, BLOCK, ATT_KV_HEADS, HEAD_DIM)
    tp = jnp.pad(tb, ((0, 0), (1, 1), (0, 0), (0, 0), (0, 0)))
    return jnp.concatenate([tp[:, :-2], tp[:, 1:-1], tp[:, 2:]], axis=2)


def band_mask(nb):
    r = jnp.arange(BLOCK)[:, None]
    j = jnp.arange(3 * BLOCK)[None, :]
    rel = j - BLOCK - r
    kpos = (jnp.arange(nb)[:, None, None] - 1) * BLOCK + j[None]
    return (jnp.abs(rel) <= WINDOW)[None] & (kpos >= 0) & (kpos < nb * BLOCK)


def sink_logits(sink, shape):
    s = sink.astype(jnp.float32).reshape((1, ATT_KV_HEADS, ATT_GROUP) + (1,) * (len(shape) - 3))
    return jnp.broadcast_to(s, shape[:-1] + (1,))


def window_attn_latent(q, k, v, kc, vc, sink):
    B, S = q.shape[:2]
    nb = S // BLOCK
    nk = 3 * BLOCK
    ncx = kc.shape[1]
    scale = HEAD_DIM ** -0.5
    qb = q.reshape(B, nb, BLOCK, ATT_KV_HEADS, ATT_GROUP, HEAD_DIM)
    kb, vb = banded(k, nb), banded(v, nb)
    s_band = jnp.einsum("bnqkgd,bnjkd->bkgnqj", qb, kb).astype(jnp.float32) * scale
    s_band = jnp.where(band_mask(nb), s_band, -jnp.inf)
    s_ctx = jnp.einsum("bnqkgd,bckd->bkgnqc", qb, kc).astype(jnp.float32) * scale
    p = jax.nn.softmax(jnp.concatenate([s_band, s_ctx, sink_logits(sink, s_band.shape)], axis=-1), axis=-1)
    o = (jnp.einsum("bkgnqj,bnjkd->bnqkgd", p[..., :nk].astype(v.dtype), vb)
         + jnp.einsum("bkgnqc,bckd->bnqkgd", p[..., nk:nk + ncx].astype(v.dtype), vc))
    return o.reshape(B, S, ATT_DIM)


def attn_context(qc, kc, vc, sink):
    B, C = qc.shape[:2]
    qg = qc.reshape(B, C, ATT_KV_HEADS, ATT_GROUP, HEAD_DIM)
    s = jnp.einsum("bqkgd,bckd->bkgqc", qg, kc).astype(jnp.float32) * HEAD_DIM ** -0.5
    p = jax.nn.softmax(jnp.concatenate([s, sink_logits(sink, s.shape)], axis=-1), axis=-1)
    o = jnp.einsum("bkgqc,bckd->bqkgd", p[..., :C].astype(vc.dtype), vc)
    return o.reshape(B, C, ATT_DIM)


def even_mixer(hl, hc, w_in, conv_w, conv_b, ln_g, ln_b, sink, w_out, cos, sin, with_ctx_out):
    B, S, _ = hl.shape
    C = hc.shape[1]
    pl = hl @ w_in
    a_l = conv_module(pl[..., :CONV_DIM], pl[..., CONV_DIM:Q_OFF], conv_w, conv_b, ln_g, ln_b)
    q = apply_rope(pl[..., Q_OFF:K_OFF].reshape(B, S, ATT_HEADS, HEAD_DIM), cos, sin)
    k = apply_rope(pl[..., K_OFF:V_OFF].reshape(B, S, ATT_KV_HEADS, HEAD_DIM), cos, sin)
    v = pl[..., V_OFF:].reshape(B, S, ATT_KV_HEADS, HEAD_DIM)
    pkv = hc @ w_in[:, K_OFF:]
    kc = pkv[..., :KV_DIM].reshape(B, C, ATT_KV_HEADS, HEAD_DIM)
    vc = pkv[..., KV_DIM:].reshape(B, C, ATT_KV_HEADS, HEAD_DIM)
    att_l = window_attn_latent(q, k, v, kc, vc, sink)
    yl = jnp.concatenate([a_l, att_l], axis=-1) @ w_out
    if not with_ctx_out:
        return yl, None
    pc = hc @ w_in[:, :K_OFF]
    a_c = conv_module(pc[..., :CONV_DIM], pc[..., CONV_DIM:Q_OFF], conv_w, conv_b, ln_g, ln_b)
    att_c = attn_context(pc[..., Q_OFF:].reshape(B, C, ATT_HEADS, HEAD_DIM), kc, vc, sink)
    yc = jnp.concatenate([a_c, att_c], axis=-1) @ w_out
    return yl, yc


def mlstm_zero_state(B):
    f32 = jnp.float32
    return (jnp.zeros((B, ML_HEADS, ML_HEAD_DIM, ML_HEAD_DIM), f32),
            jnp.zeros((B, ML_HEADS, ML_HEAD_DIM), f32),
            jnp.zeros((B, ML_HEADS), f32))


def mlstm_scan(q, k, v, i_pre, f_pre, state, with_out):
    B, n, H, dh = q.shape
    nc = n // ML_CHUNK
    f32 = jnp.float32

    def chunks(t):
        return jnp.moveaxis(t.astype(f32).reshape((B, nc, ML_CHUNK) + t.shape[2:]), 1, 0)

    xs = (chunks(q), chunks(k * dh ** -0.5), chunks(v), chunks(i_pre),
          chunks(jax.nn.log_sigmoid(f_pre.astype(f32))))
    causal = jnp.tril(jnp.ones((ML_CHUNK, ML_CHUNK), bool))[None, :, :, None]

    def step(carry, inp):
        Cm, nv, m = carry
        qc, kc, vc, li, lf = inp
        b = jnp.cumsum(lf, axis=1)
        b_end = b[:, -1]
        w_end = b_end[:, None] - b + li
        m_new = jnp.maximum(b_end + m, jnp.max(w_end, axis=1))
        decay = jnp.exp(b_end + m - m_new)
        wk = jnp.exp(w_end - m_new[:, None])[..., None] * kc
        C_new = decay[..., None, None] * Cm + jnp.einsum("blhv,blhk->bhvk", vc, wk)
        n_new = decay[..., None] * nv + jnp.sum(wk, axis=1)
        if not with_out:
            return (C_new, n_new, m_new), None
        dmat = jnp.where(causal, b[:, :, None] - b[:, None] + li[:, None], -jnp.inf)
        inter = b + m[:, None]
        m_q = jnp.maximum(jnp.max(dmat, axis=2), inter)
        s = jnp.einsum("bqhd,bkhd->bqkh", qc, kc) * jnp.exp(dmat - m_q[:, :, None])
        w_inter = jnp.exp(inter - m_q)
        num = (jnp.einsum("bqkh,bkhd->bqhd", s, vc)
               + w_inter[..., None] * jnp.einsum("bhvk,bqhk->bqhv", Cm, qc))
        den = jnp.sum(s, axis=2) + w_inter * jnp.einsum("bhk,bqhk->bqh", nv, qc)
        h = num / jnp.maximum(jnp.abs(den), jnp.exp(-m_q))[..., None]
        return (C_new, n_new, m_new), h

    state, hs = lax.scan(step, state, xs)
    if with_out:
        hs = jnp.moveaxis(hs, 0, 1).reshape(B, n, H, dh)
    return state, hs


def odd_mixer(hl, hc, w_in, b_gate, norm_g, w_out, with_ctx_out):
    def qkv_gates(h):
        B, n, _ = h.shape
        qkv = (h @ w_in[:, :3 * ML_DIM]).reshape(B, n, 3, ML_HEADS, ML_HEAD_DIM)
        g = (h @ w_in[:, 4 * ML_DIM:] + b_gate).astype(jnp.float32).reshape(B, n, 4, ML_HEADS)
        return qkv[:, :, 0], qkv[:, :, 1], qkv[:, :, 2], g

    def out(h, h_sum):
        B, n, _ = h.shape
        o = jax.nn.sigmoid(h @ w_in[:, 3 * ML_DIM:4 * ML_DIM])
        hn = h_sum * lax.rsqrt(jnp.mean(h_sum * h_sum, axis=-1, keepdims=True) + EPS)
        hn = hn.reshape(B, n, ML_DIM) * norm_g.astype(jnp.float32)
        return (o * hn.astype(h.dtype)) @ w_out

    flip = lambda t: jnp.flip(t, axis=1)
    ql, kl, vl, gl = qkv_gates(hl)
    qc, kc, vc, gc = qkv_gates(hc)
    z = mlstm_zero_state(hl.shape[0])
    st_f, hcf = mlstm_scan(qc, kc, vc, gc[:, :, 0], gc[:, :, 1], z, with_ctx_out)
    st_b, hcb = mlstm_scan(flip(qc), flip(kc), flip(vc), flip(gc[:, :, 2]), flip(gc[:, :, 3]), z, with_ctx_out)
    _, hlf = mlstm_scan(ql, kl, vl, gl[:, :, 0], gl[:, :, 1], st_f, True)
    _, hlb = mlstm_scan(flip(ql), flip(kl), flip(vl), flip(gl[:, :, 2]), flip(gl[:, :, 3]), st_b, True)
    yl = out(hl, hlf + flip(hlb))
    if not with_ctx_out:
        return yl, None
    return yl, out(hc, hcf + flip(hcb))


def setup_inputs(seed: int = 0) -> dict:
    key = jax.random.key(seed)
    ks = iter(jax.random.split(key, 32))
    f32 = jnp.float32
    D = D_MODEL

    def nrm(shape, s):
        return jax.random.normal(next(ks), shape, f32) * s

    def gain(shape):
        return 1.0 + nrm(shape, 0.02)

    gate_offset = jnp.tile(jnp.repeat(jnp.array([0.0, FORGET_BIAS], f32), ML_HEADS), 2)
    return {
        "x": nrm((BATCH, SEQ, D), 1.0),
        "c": nrm((BATCH, D), 1.0),
        "ctx": nrm((BATCH, CTX_LEN, D), 1.0),
        "c_ctx": nrm((D,), 1.0),
        "mod_w": nrm((DEPTH, D, N_MOD * D), 0.5 * D ** -0.5),
        "mod_b": nrm((DEPTH, N_MOD * D), 0.02),
        "ffn1_norm": gain((DEPTH, D)),
        "ffn1_w_gu": nrm((DEPTH, D, 2 * D_FF), D ** -0.5),
        "ffn1_w_d": nrm((DEPTH, D_FF, D), D_FF ** -0.5),
        "mix_norm": gain((DEPTH, D)),
        "ffn2_norm": gain((DEPTH, D)),
        "ffn2_w_gu": nrm((DEPTH, D, 2 * D_FF), D ** -0.5),
        "ffn2_w_d": nrm((DEPTH, D_FF, D), D_FF ** -0.5),
        "ev_w_in": nrm((N_EVEN, D, EVEN_IN), D ** -0.5),
        "ev_conv_w": nrm((N_EVEN, CONV_WIDTH, CONV_DIM), CONV_WIDTH ** -0.5),
        "ev_conv_b": nrm((N_EVEN, CONV_DIM), 0.02),
        "ev_conv_ln_g": gain((N_EVEN, CONV_DIM)),
        "ev_conv_ln_b": nrm((N_EVEN, CONV_DIM), 0.02),
        "ev_sink": nrm((N_EVEN, ATT_HEADS), 0.5),
        "ev_w_out": nrm((N_EVEN, EVEN_MIX, D), EVEN_MIX ** -0.5),
        "od_w_in": nrm((N_ODD, D, ODD_IN), D ** -0.5),
        "od_b_gate": gate_offset[None] + nrm((N_ODD, 4 * ML_HEADS), 0.3),
        "od_norm_g": gain((N_ODD, ML_DIM)),
        "od_w_out": nrm((N_ODD, ML_DIM, D), ML_DIM ** -0.5),
        "final_norm": gain((D,)),
    }


def reference(x, c, ctx, c_ctx, mod_w, mod_b, ffn1_norm, ffn1_w_gu, ffn1_w_d, mix_norm,
              ffn2_norm, ffn2_w_gu, ffn2_w_d, ev_w_in, ev_conv_w, ev_conv_b, ev_conv_ln_g,
              ev_conv_ln_b, ev_sink, ev_w_out, od_w_in, od_b_gate, od_norm_g, od_w_out, final_norm):
    cos, sin = axial_rope_tables(x.shape[1])
    xl, xc = x, ctx
    for l in range(DEPTH):
        last = l == DEPTH - 1
        ml = ada_params(c, mod_w[l], mod_b[l])
        mc = ada_params(c_ctx, mod_w[l], mod_b[l])
        xl = ffn_half(xl, ml[0:3], ffn1_norm[l], ffn1_w_gu[l], ffn1_w_d[l])
        xc = ffn_half(xc, mc[0:3], ffn1_norm[l], ffn1_w_gu[l], ffn1_w_d[l])
        hl = modulate(rmsnorm(xl, mix_norm[l]), ml[3], ml[4])
        hc = modulate(rmsnorm(xc, mix_norm[l]), mc[3], mc[4])
        j = l // 2
        if l % 2 == 0:
            yl, yc = even_mixer(hl, hc, ev_w_in[j], ev_conv_w[j], ev_conv_b[j], ev_conv_ln_g[j],
                                ev_conv_ln_b[j], ev_sink[j], ev_w_out[j], cos, sin, not last)
        else:
            yl, yc = odd_mixer(hl, hc, od_w_in[j], od_b_gate[j], od_norm_g[j], od_w_out[j], not last)
        xl = xl + ml[5] * yl
        xl = ffn_half(xl, ml[6:9], ffn2_norm[l], ffn2_w_gu[l], ffn2_w_d[l])
        if not last:
            xc = xc + mc[5] * yc
            xc = ffn_half(xc, mc[6:9], ffn2_norm[l], ffn2_w_gu[l], ffn2_w_d[l])
    return rmsnorm(xl, final_norm)
```

```python
import functools

import jax
import jax.numpy as jnp
import numpy as np
from jax import lax
from jax.experimental import pallas as pl
from jax.experimental.pallas import tpu as pltpu

D = 1024
BATCH = 4
SEQ = 4096
DEPTH = 2
GRID_W = 64
CTX = 256
N_MOD = 9
D_FF = 2816
EPS = 1e-6
CONV_DIM = 512
CONV_W = 31
CONV_PAD = 15
HEAD_DIM = 64
ATT_HEADS = 8
KV_HEADS = 2
ATT_DIM = 512
KV_DIM = 128
WINDOW = 128
BLK = 128
ROPE_BASE = 10000.0
ROPE_FREQS = 16
Q_OFF = 2 * CONV_DIM
K_OFF = Q_OFF + ATT_DIM
V_OFF = K_OFF + KV_DIM
ML_HEADS = 4
ML_DH = 256
ML_DIM = 1024
ML_CHUNK = 128

T_LAT = BATCH * SEQ
T_CTX = BATCH * CTX
T_ALL = T_LAT + T_CTX
TM = 1024
N_LAT_TILES = T_LAT // TM
N_ALL_TILES = T_ALL // TM
MOD_ROWS = 8
HALO = 16
NEG = -1e30
VMEM_LIMIT = 56 * 1024 * 1024

F32 = jnp.float32
BF16 = jnp.bfloat16


def _sigmoid(x):
    return 1.0 / (1.0 + jnp.exp(-x))


def _silu(x):
    return x * _sigmoid(x)


def _dot(a, b, precision=None):
    return jnp.dot(a, b, preferred_element_type=F32, precision=precision)


def _dot_nt(a, b, precision=None):
    return lax.dot_general(a, b, (((1,), (1,)), ((), ())),
                           preferred_element_type=F32, precision=precision)


def _dot_tn(a, b):
    return lax.dot_general(a, b, (((0,), (0,)), ((), ())), preferred_element_type=F32)


def _rms_mod(x, g, shift, scale):
    y = x * lax.rsqrt(jnp.mean(x * x, axis=-1, keepdims=True) + EPS)
    return (y * g) * (1.0 + scale) + shift


def _resident(shape):
    nd = len(shape)
    return pl.BlockSpec(shape, lambda *_: (0,) * nd, pipeline_mode=pl.Buffered(1))


def _params(n_axes=1):
    return pltpu.CompilerParams(dimension_semantics=("arbitrary",) * n_axes,
                                vmem_limit_bytes=VMEM_LIMIT)


def _mod_kernel(c_ref, w_ref, b_ref, o_ref):
    s = _silu(c_ref[...])
    o_ref[...] = _dot(s, w_ref[...], precision=lax.Precision.HIGHEST) + b_ref[...]


def _ada_mods(cond, mod_w, mod_b):
    tn = 1024
    n = N_MOD * D
    return pl.pallas_call(
        _mod_kernel,
        grid=(DEPTH, n // tn),
        in_specs=[
            pl.BlockSpec((MOD_ROWS, D), lambda l, j: (0, 0)),
            pl.BlockSpec((None, D, tn), lambda l, j: (l, 0, j)),
            pl.BlockSpec((None, 1, tn), lambda l, j: (l, 0, j)),
        ],
        out_specs=pl.BlockSpec((None, MOD_ROWS, tn), lambda l, j: (l, 0, j)),
        out_shape=jax.ShapeDtypeStruct((DEPTH, MOD_ROWS, n), F32),
        compiler_params=_params(2),
        name="ada_mods",
    )(cond, mod_w, mod_b.reshape(DEPTH, 1, n))


def _mod_spec(k):
    return pl.BlockSpec((None, 1, 3 * D), lambda i: (i // (SEQ // TM), 0, k))


FF_CHUNKS = tuple((c, min(c + 512, D_FF)) for c in range(0, D_FF, 512))


def _ffn_kernel(x_ref, m_ref, g_ref, wgu_ref, wd_ref, *rest, final):
    if final:
        fn_ref, o_ref = rest
    else:
        (o_ref,) = rest
    x = x_ref[...]
    shift = m_ref[:, 0:D]
    scale = m_ref[:, D:2 * D]
    gate = m_ref[:, 2 * D:3 * D]
    h = _rms_mod(x, g_ref[...], shift, scale).astype(BF16)
    acc = None
    for c0, c1 in FF_CHUNKS:
        hg = _dot(h, wgu_ref[:, c0:c1])
        hu = _dot(h, wgu_ref[:, D_FF + c0:D_FF + c1])
        a = (_silu(hg) * hu).astype(BF16)
        p = _dot(a, wd_ref[c0:c1, :])
        acc = p if acc is None else acc + p
    y = x + (0.5 * gate) * acc
    if final:
        y = (y * lax.rsqrt(jnp.mean(y * y, axis=-1, keepdims=True) + EPS)) * fn_ref[...]
    o_ref[...] = y


def _ffn(x, mods, k, norm_g, w_gu, w_d, n_tiles, final_g=None):
    final = final_g is not None
    in_specs = [
        pl.BlockSpec((TM, D), lambda i: (i, 0)),
        _mod_spec(k),
        _resident((1, D)),
        _resident((D, 2 * D_FF)),
        _resident((D_FF, D)),
    ]
    args = [x, mods, norm_g.reshape(1, D), w_gu, w_d]
    if final:
        in_specs.append(_resident((1, D)))
        args.append(final_g.reshape(1, D))
    return pl.pallas_call(
        functools.partial(_ffn_kernel, final=final),
        grid=(n_tiles,),
        in_specs=in_specs,
        out_specs=pl.BlockSpec((TM, D), lambda i: (i, 0)),
        out_shape=jax.ShapeDtypeStruct((n_tiles * TM, D), F32),
        compiler_params=_params(1),
        name="ffn_final" if final else "ffn",
    )(*args)


EV_COLS = 2 * CONV_DIM + ATT_DIM + 4 * KV_DIM


def _rope_slab(x, c, s1, s2):
    return x * c + pltpu.roll(x, 112, 1) * s1 + pltpu.roll(x, 16, 1) * s2


def _even_in_kernel(x_ref, m_ref, g_ref, w_ref, c_ref, s1_ref, s2_ref,
                    u_ref, q_ref, k_ref, v_ref):
    x = x_ref[...]
    h = _rms_mod(x, g_ref[...], m_ref[:, 0:D], m_ref[:, D:2 * D]).astype(BF16)
    vg = _dot(h, w_ref[:, 0:Q_OFF])
    u_ref[...] = vg[:, 0:CONV_DIM] * _sigmoid(vg[:, CONV_DIM:Q_OFF])
    c = c_ref[...]
    s1 = s1_ref[...]
    s2 = s2_ref[...]
    qf = _dot(h, w_ref[:, Q_OFF:K_OFF])
    for s in range(ATT_DIM // 128):
        r = _rope_slab(qf[:, s * 128:(s + 1) * 128], c, s1, s2)
        q_ref[:, s * 128:(s + 1) * 128] = (r * (HEAD_DIM ** -0.5)).astype(BF16)
    kf = _dot(h, w_ref[:, K_OFF:K_OFF + 2 * KV_DIM])
    for s in range(2 * KV_DIM // 128):
        r = _rope_slab(kf[:, s * 128:(s + 1) * 128], c, s1, s2)
        k_ref[:, s * 128:(s + 1) * 128] = r.astype(BF16)
    v_ref[...] = _dot(h, w_ref[:, K_OFF + 2 * KV_DIM:EV_COLS]).astype(BF16)


def _even_in(x, mods, norm_g, w_aug, rope_c, rope_s1, rope_s2):
    def rope_spec():
        return pl.BlockSpec((TM, 128), lambda i: (jnp.where(i < N_LAT_TILES, i % (SEQ // TM), SEQ // TM), 0))

    return pl.pallas_call(
        _even_in_kernel,
        grid=(N_ALL_TILES,),
        in_specs=[
            pl.BlockSpec((TM, D), lambda i: (i, 0)),
            _mod_spec(1),
            _resident((1, D)),
            _resident((D, EV_COLS)),
            rope_spec(), rope_spec(), rope_spec(),
        ],
        out_specs=[
            pl.BlockSpec((TM, CONV_DIM), lambda i: (i, 0)),
            pl.BlockSpec((TM, ATT_DIM), lambda i: (i, 0)),
            pl.BlockSpec((TM, 2 * KV_DIM), lambda i: (i, 0)),
            pl.BlockSpec((TM, 2 * KV_DIM), lambda i: (i, 0)),
        ],
        out_shape=[
            jax.ShapeDtypeStruct((T_ALL, CONV_DIM), F32),
            jax.ShapeDtypeStruct((T_ALL, ATT_DIM), BF16),
            jax.ShapeDtypeStruct((T_ALL, 2 * KV_DIM), BF16),
            jax.ShapeDtypeStruct((T_ALL, 2 * KV_DIM), BF16),
        ],
        compiler_params=_params(1),
        name="even_in",
    )(x, mods, norm_g.reshape(1, D), w_aug, rope_c, rope_s1, rope_s2)


def _rope_tables():
    rows = SEQ // GRID_W
    row = jnp.repeat(jnp.arange(rows, dtype=F32), GRID_W)
    col = jnp.tile(jnp.arange(GRID_W, dtype=F32), rows)
    inv = ROPE_BASE ** (-jnp.arange(ROPE_FREQS, dtype=F32) / ROPE_FREQS)
    ang = jnp.concatenate([row[:, None] * inv, col[:, None] * inv], axis=-1)
    cos, sin = jnp.cos(ang), jnp.sin(ang)
    lane = np.arange(128)
    d = lane % HEAD_DIM
    src = (d // 32) * ROPE_FREQS + d % ROPE_FREQS
    first = jnp.asarray(((d % 32) // ROPE_FREQS) == 0)
    c = cos[:, src]
    s = sin[:, src]
    s1 = jnp.where(first, -s, 0.0)
    s2 = jnp.where(first, 0.0, s)
    pad = lambda t, v: jnp.concatenate([t, jnp.full((TM, 128), v, F32)], axis=0)
    return pad(c, 1.0), pad(s1, 0.0), pad(s2, 0.0)


NB = SEQ // BLK
NCB = CTX // BLK
GROUP = ATT_HEADS // KV_HEADS


def _attn_kernel(q_ref, kp_ref, kc_ref, kn_ref, vp_ref, vc_ref, vn_ref,
                 kx_ref, vx_ref, sink_ref, o_ref):
    i = pl.program_id(1)
    r = lax.broadcasted_iota(jnp.int32, (BLK, 3 * BLK), 0)
    j = lax.broadcasted_iota(jnp.int32, (BLK, 3 * BLK), 1)
    rel = j - BLK - r
    kpos = (i - 1) * BLK + j
    ok = (jnp.abs(rel) <= WINDOW) & (kpos >= 0) & (kpos < SEQ) & (i < NB)
    bias1 = jnp.where(ok, 0.0, NEG).astype(F32)
    bias = jnp.concatenate([bias1] * GROUP, axis=0)
    lane = lax.broadcasted_iota(jnp.int32, (BLK, 128), 1)
    lo = lane < HEAD_DIM
    q = q_ref[...]
    zero = jnp.zeros((BLK, 128), BF16)
    for g in range(KV_HEADS):
        gs = slice(g * 128, (g + 1) * 128)
        kb = jnp.concatenate([kp_ref[:, gs], kc_ref[:, gs], kn_ref[:, gs]], axis=0)
        vb = jnp.concatenate([vp_ref[:, gs], vc_ref[:, gs], vn_ref[:, gs]], axis=0)
        kx = kx_ref[:, gs]
        vx = vx_ref[:, gs]
        qs, sk = [], []
        for hh in range(GROUP):
            hd = g * GROUP + hh
            slab = q[:, (hd // 2) * 128:(hd // 2 + 1) * 128]
            qs.append(jnp.where(lo if hd % 2 == 0 else ~lo, slab, zero))
            sk.append(jnp.broadcast_to(sink_ref[hd:hd + 1, 0:1], (BLK, 1)))
        qg = jnp.concatenate(qs, axis=0)
        sink = jnp.concatenate(sk, axis=0)
        s_b = _dot_nt(qg, kb) + bias
        s_x = _dot_nt(qg, kx)
        m = jnp.maximum(jnp.maximum(jnp.max(s_b, axis=1, keepdims=True),
                                    jnp.max(s_x, axis=1, keepdims=True)), sink)
        p_b = jnp.exp(s_b - m)
        p_x = jnp.exp(s_x - m)
        den = (jnp.sum(p_b, axis=1, keepdims=True) + jnp.sum(p_x, axis=1, keepdims=True)
               + jnp.exp(sink - m))
        inv = 1.0 / den
        o = _dot((p_b * inv).astype(BF16), vb) + _dot((p_x * inv).astype(BF16), vx)
        for pair in range(GROUP // 2):
            o0 = o[(2 * pair) * BLK:(2 * pair + 1) * BLK]
            o1 = o[(2 * pair + 1) * BLK:(2 * pair + 2) * BLK]
            col = (g * GROUP // 2 + pair) * 128
            o_ref[:, col:col + 128] = jnp.where(lo, o0, o1).astype(BF16)


def _attention(q, kd, vd, sink):
    lat_blocks = T_LAT // BLK

    def q_map(b, i):
        return (jnp.where(i < NB, b * NB + i, lat_blocks + b * NCB + (i - NB)), 0)

    def band_map(off):
        return lambda b, i: (b * NB + jnp.clip(i + off, 0, NB - 1), 0)

    ctx_map = lambda b, i: (T_LAT // CTX + b, 0)
    kv_blk = lambda off: pl.BlockSpec((BLK, 2 * KV_DIM), band_map(off))
    return pl.pallas_call(
        _attn_kernel,
        grid=(BATCH, NB + NCB),
        in_specs=[
            pl.BlockSpec((BLK, ATT_DIM), q_map),
            kv_blk(-1), kv_blk(0), kv_blk(1),
            kv_blk(-1), kv_blk(0), kv_blk(1),
            pl.BlockSpec((CTX, 2 * KV_DIM), ctx_map),
            pl.BlockSpec((CTX, 2 * KV_DIM), ctx_map),
            pl.BlockSpec((ATT_HEADS, 128), lambda b, i: (0, 0)),
        ],
        out_specs=pl.BlockSpec((BLK, ATT_DIM), q_map),
        out_shape=jax.ShapeDtypeStruct((T_ALL, ATT_DIM), BF16),
        compiler_params=_params(2),
        name="window_attn",
    )(q, kd, kd, kd, vd, vd, vd, kd, vd, sink)


CONV_ROWS = 64


def _even_out_kernel(x_ref, m_ref, u_ref, up_ref, un_ref, att_ref, cw_ref, cb_ref,
                     lg_ref, lb_ref, wa_ref, wb_ref, o_ref, ext_ref, win_ref, a_ref, *, tm, tiles_per_seq):
    t = pl.program_id(0)
    first = (t % tiles_per_seq) == 0
    last = (t % tiles_per_seq) == tiles_per_seq - 1
    ext_ref[0:HALO, :] = jnp.where(first, 0.0, up_ref[...])
    ext_ref[HALO:HALO + tm, :] = u_ref[...]
    ext_ref[HALO + tm:, :] = jnp.where(last, 0.0, un_ref[...])
    cb = cb_ref[...]
    lg = lg_ref[...]
    lb = lb_ref[...]

    def body(rb, carry):
        base = pl.multiple_of(rb * CONV_ROWS, CONV_ROWS)
        win_ref[...] = ext_ref[pl.ds(base, CONV_ROWS + 2 * HALO), :]
        acc = jnp.broadcast_to(cb, (CONV_ROWS, CONV_DIM))
        for k in range(CONV_W):
            acc = acc + cw_ref[k:k + 1, :] * win_ref[HALO - CONV_PAD + k:HALO - CONV_PAD + k + CONV_ROWS, :]
        mu = jnp.mean(acc, axis=-1, keepdims=True)
        cen = acc - mu
        var = jnp.mean(cen * cen, axis=-1, keepdims=True)
        y = cen * lax.rsqrt(var + EPS) * lg + lb
        a_ref[pl.ds(base, CONV_ROWS), :] = _silu(y).astype(BF16)
        return carry

    lax.fori_loop(0, tm // CONV_ROWS, body, 0)
    y = _dot(a_ref[...], wa_ref[...]) + _dot(att_ref[...], wb_ref[...])
    o_ref[...] = x_ref[...] + m_ref[:, 2 * D:3 * D] * y


def _even_out(x, mods, u, att, conv_w, conv_b, ln_g, ln_b, wa, wb, *, tm, row0, n_tiles,
              tiles_per_seq, mod_row, prev=None):
    blk0 = row0 // tm
    hb = tm // HALO
    n_halo = T_ALL // HALO
    row_map = lambda i: (blk0 + i, 0)
    in_specs = [
        pl.BlockSpec((tm, D), row_map),
        pl.BlockSpec((None, 1, 3 * D), lambda i: (mod_row(i), 0, 1)),
        pl.BlockSpec((tm, CONV_DIM), row_map),
        pl.BlockSpec((HALO, CONV_DIM), lambda i: (jnp.maximum((blk0 + i) * hb - 1, 0), 0)),
        pl.BlockSpec((HALO, CONV_DIM), lambda i: (jnp.minimum((blk0 + i + 1) * hb, n_halo - 1), 0)),
        pl.BlockSpec((tm, ATT_DIM), row_map),
        _resident((32, CONV_DIM)),
        _resident((1, CONV_DIM)),
        _resident((1, CONV_DIM)),
        _resident((1, CONV_DIM)),
        _resident((CONV_DIM, D)),
        _resident((ATT_DIM, D)),
    ]
    args = [x, mods, u, u, u, att, conv_w, conv_b, ln_g, ln_b, wa, wb]
    aliases = {}
    if prev is not None:
        in_specs.append(pl.BlockSpec(memory_space=pl.ANY))
        args.append(prev)
        aliases = {len(args) - 1: 0}

    def kern(*refs):
        if prev is not None:
            refs = refs[:12] + refs[13:]
        _even_out_kernel(*refs, tm=tm, tiles_per_seq=tiles_per_seq)

    return pl.pallas_call(
        kern,
        grid=(n_tiles,),
        in_specs=in_specs,
        out_specs=pl.BlockSpec((tm, D), row_map),
        out_shape=jax.ShapeDtypeStruct((T_ALL, D), F32),
        scratch_shapes=[pltpu.VMEM((tm + 2 * HALO, CONV_DIM), F32),
                        pltpu.VMEM((CONV_ROWS + 2 * HALO, CONV_DIM), F32),
                        pltpu.VMEM((tm, CONV_DIM), BF16)],
        input_output_aliases=aliases,
        compiler_params=_params(1),
        name="even_out_ctx" if prev is not None else "even_out",
    )(*args)


N_GATE = 4 * ML_HEADS


def _odd_in_kernel(x_ref, m_ref, g_ref, w_ref, wg_ref, wgt_ref, bg_ref, bgt_ref,
                   q_ref, k_ref, v_ref, gc_ref, gr_ref):
    x = x_ref[...]
    hf = _rms_mod(x, g_ref[...], m_ref[:, 0:D], m_ref[:, D:2 * D])
    h = hf.astype(BF16)
    q_ref[...] = _dot(h, w_ref[:, 0:ML_DIM]).astype(BF16)
    k_ref[...] = (_dot(h, w_ref[:, ML_DIM:2 * ML_DIM]) * (ML_DH ** -0.5)).astype(BF16)
    v_ref[...] = _dot(h, w_ref[:, 2 * ML_DIM:3 * ML_DIM]).astype(BF16)
    hi = lax.Precision.HIGHEST
    gc_ref[...] = _dot(hf, wg_ref[...], precision=hi) + bg_ref[...]
    gr_ref[...] = _dot_nt(wgt_ref[...], hf, precision=hi) + bgt_ref[...]


def _odd_in(x, mods, norm_g, w_qkv, w_gate, w_gate_t, b_gate, b_gate_t):
    row = lambda i: (i, 0)
    return pl.pallas_call(
        _odd_in_kernel,
        grid=(N_ALL_TILES,),
        in_specs=[
            pl.BlockSpec((TM, D), row),
            _mod_spec(1),
            _resident((1, D)),
            _resident((D, 3 * ML_DIM)),
            _resident((D, 128)),
            _resident((N_GATE, D)),
            _resident((1, 128)),
            _resident((N_GATE, 1)),
        ],
        out_specs=[
            pl.BlockSpec((TM, ML_DIM), row),
            pl.BlockSpec((TM, ML_DIM), row),
            pl.BlockSpec((TM, ML_DIM), row),
            pl.BlockSpec((TM, 128), row),
            pl.BlockSpec((N_GATE, TM), lambda i: (0, i)),
        ],
        out_shape=[
            jax.ShapeDtypeStruct((T_ALL, ML_DIM), BF16),
            jax.ShapeDtypeStruct((T_ALL, ML_DIM), BF16),
            jax.ShapeDtypeStruct((T_ALL, ML_DIM), BF16),
            jax.ShapeDtypeStruct((T_ALL, 128), F32),
            jax.ShapeDtypeStruct((N_GATE, T_ALL), F32),
        ],
        compiler_params=_params(1),
        name="odd_in",
    )(x, mods, norm_g.reshape(1, D), w_qkv, w_gate, w_gate_t, b_gate, b_gate_t)


N_CTX_CHUNKS = CTX // ML_CHUNK
N_LAT_CHUNKS = SEQ // ML_CHUNK
N_STEPS = N_CTX_CHUNKS + N_LAT_CHUNKS
L = ML_CHUNK


def _log_sigmoid(x):
    return jnp.minimum(x, 0.0) - jnp.log(1.0 + jnp.exp(-jnp.abs(x)))


def _mlstm_chain(q, k, v, li_c, lf_c, li_r, lf_r, c_ref, n_ref, m_ref, reverse):
    row = lax.broadcasted_iota(jnp.int32, (L, L), 0)
    col = lax.broadcasted_iota(jnp.int32, (L, L), 1)
    causal = (col >= row) if reverse else (col <= row)
    before = (row >= col) if reverse else (row <= col)
    b_c = jnp.sum(jnp.where(causal, lf_r, 0.0), axis=1, keepdims=True)
    b_r = jnp.sum(jnp.where(before, lf_c, 0.0), axis=0, keepdims=True)
    b_end = jnp.sum(lf_r, axis=1, keepdims=True)
    m_old = m_ref[:, 0:1]
    w_end_r = b_end - b_r + li_r
    w_end_c = b_end - b_c + li_c
    m_new = jnp.maximum(b_end + m_old, jnp.max(w_end_r, axis=1, keepdims=True))
    decay = jnp.exp(b_end + m_old - m_new)
    wk = jnp.exp(w_end_c - m_new) * k.astype(F32)
    c_old = c_ref[...]
    n_old = n_ref[...]
    dmat = jnp.where(causal, b_c - b_r + li_r, NEG)
    inter = b_c + m_old
    m_q = jnp.maximum(jnp.max(dmat, axis=1, keepdims=True), inter)
    s = _dot_nt(q, k) * jnp.exp(dmat - m_q)
    w_inter = jnp.exp(inter - m_q)
    num = _dot(s.astype(BF16), v) + w_inter * _dot(q, c_old.astype(BF16))
    den = (jnp.sum(s, axis=1, keepdims=True)
           + w_inter * jnp.sum(q.astype(F32) * n_old, axis=1, keepdims=True))
    h = num / jnp.maximum(jnp.abs(den), jnp.exp(-m_q))
    c_ref[...] = decay * c_old + _dot_tn(wk.astype(BF16), v)
    n_ref[...] = decay * n_old + jnp.sum(wk, axis=0, keepdims=True)
    m_ref[...] = jnp.broadcast_to(m_new, (1, 128))
    return h


def _mlstm_kernel(qf_ref, kf_ref, vf_ref, qb_ref, kb_ref, vb_ref,
                  gcf_ref, gcb_ref, grf_ref, grb_ref, hf_ref, hb_ref, c_ref, n_ref, m_ref):
    t = pl.program_id(1)

    @pl.when(t == 0)
    def _():
        c_ref[...] = jnp.zeros_like(c_ref)
        n_ref[...] = jnp.zeros_like(n_ref)
        m_ref[...] = jnp.zeros_like(m_ref)

    for d, (q_ref, k_ref, v_ref, gc_ref, gr_ref, h_ref) in enumerate(
            ((qf_ref, kf_ref, vf_ref, gcf_ref, grf_ref, hf_ref),
             (qb_ref, kb_ref, vb_ref, gcb_ref, grb_ref, hb_ref))):
        for hd in range(ML_HEADS):
            ci = d * 2 * ML_HEADS + hd
            cf = ci + ML_HEADS
            cs = slice(hd * ML_DH, (hd + 1) * ML_DH)
            chain = d * ML_HEADS + hd
            h = _mlstm_chain(
                q_ref[:, cs], k_ref[:, cs], v_ref[:, cs],
                gc_ref[:, ci:ci + 1], _log_sigmoid(gc_ref[:, cf:cf + 1]),
                gr_ref[ci:ci + 1, :], _log_sigmoid(gr_ref[cf:cf + 1, :]),
                c_ref.at[chain], n_ref.at[chain], m_ref.at[chain], reverse=(d == 1))
            h_ref[:, cs] = h


def _mlstm(q, k, v, g_col, g_row):
    lat_chunks = T_LAT // L

    def fwd_in(b, t):
        return jnp.where(t < N_CTX_CHUNKS, lat_chunks + b * N_CTX_CHUNKS + t,
                         b * N_LAT_CHUNKS + (t - N_CTX_CHUNKS))

    def bwd_in(b, t):
        return jnp.where(t < N_CTX_CHUNKS, lat_chunks + b * N_CTX_CHUNKS + (N_CTX_CHUNKS - 1 - t),
                         b * N_LAT_CHUNKS + (N_STEPS - 1 - t))

    def fwd_out(b, t):
        return b * N_LAT_CHUNKS + jnp.maximum(t - N_CTX_CHUNKS, 0)

    def bwd_out(b, t):
        return b * N_LAT_CHUNKS + (N_STEPS - 1 - jnp.maximum(t, N_CTX_CHUNKS))

    rows = lambda f: pl.BlockSpec((L, ML_DIM), lambda b, t: (f(b, t), 0))
    gcol = lambda f: pl.BlockSpec((L, 128), lambda b, t: (f(b, t), 0))
    grow = lambda f: pl.BlockSpec((N_GATE, L), lambda b, t: (0, f(b, t)))
    return pl.pallas_call(
        _mlstm_kernel,
        grid=(BATCH, N_STEPS),
        in_specs=[rows(fwd_in), rows(fwd_in), rows(fwd_in),
                  rows(bwd_in), rows(bwd_in), rows(bwd_in),
                  gcol(fwd_in), gcol(bwd_in), grow(fwd_in), grow(bwd_in)],
        out_specs=[rows(fwd_out), rows(bwd_out)],
        out_shape=[jax.ShapeDtypeStruct((T_LAT, ML_DIM), F32),
                   jax.ShapeDtypeStruct((T_LAT, ML_DIM), F32)],
        scratch_shapes=[pltpu.VMEM((2 * ML_HEADS, ML_DH, ML_DH), F32),
                        pltpu.VMEM((2 * ML_HEADS, 1, ML_DH), F32),
                        pltpu.VMEM((2 * ML_HEADS, 1, 128), F32)],
        compiler_params=_params(2),
        name="mlstm",
    )(q, k, v, q, k, v, g_col, g_col, g_row, g_row)


def _odd_out_kernel(x_ref, m_ref, g_ref, hf_ref, hb_ref, wo_ref, ng_ref, wout_ref, o_ref):
    x = x_ref[...]
    h = _rms_mod(x, g_ref[...], m_ref[:, 0:D], m_ref[:, D:2 * D]).astype(BF16)
    o = _sigmoid(_dot(h, wo_ref[...]))
    hs = hf_ref[...] + hb_ref[...]
    parts = []
    for hd in range(ML_HEADS):
        p = hs[:, hd * ML_DH:(hd + 1) * ML_DH]
        parts.append(p * lax.rsqrt(jnp.mean(p * p, axis=-1, keepdims=True) + EPS))
    hn = jnp.concatenate(parts, axis=-1) * ng_ref[...]
    y = _dot((o * hn).astype(BF16), wout_ref[...])
    o_ref[...] = x + m_ref[:, 2 * D:3 * D] * y


def _odd_out(x, mods, norm_g, hf, hb, w_o, head_g, w_out):
    row = lambda i: (i, 0)
    return pl.pallas_call(
        _odd_out_kernel,
        grid=(N_LAT_TILES,),
        in_specs=[
            pl.BlockSpec((TM, D), row),
            _mod_spec(1),
            _resident((1, D)),
            pl.BlockSpec((TM, ML_DIM), row),
            pl.BlockSpec((TM, ML_DIM), row),
            _resident((D, ML_DIM)),
            _resident((1, ML_DIM)),
            _resident((ML_DIM, D)),
        ],
        out_specs=pl.BlockSpec((TM, D), row),
        out_shape=jax.ShapeDtypeStruct((T_LAT, D), F32),
        compiler_params=_params(1),
        name="odd_out",
    )(x, mods, norm_g.reshape(1, D), hf, hb, w_o, head_g.reshape(1, ML_DIM), w_out)


def kernel(x, c, ctx, c_ctx, mod_w, mod_b, ffn1_norm, ffn1_w_gu, ffn1_w_d, mix_norm, ffn2_norm,
           ffn2_w_gu, ffn2_w_d, ev_w_in, ev_conv_w, ev_conv_b, ev_conv_ln_g, ev_conv_ln_b, ev_sink,
           ev_w_out, od_w_in, od_b_gate, od_norm_g, od_w_out, final_norm):
    assert DEPTH == 2 and x.shape == (BATCH, SEQ, D) and ctx.shape == (BATCH, CTX, D)
    xs = jnp.concatenate([x.reshape(T_LAT, D), ctx.reshape(T_CTX, D)], axis=0)
    cond = jnp.zeros((MOD_ROWS, D), F32).at[:BATCH].set(c).at[BATCH].set(c_ctx)
    mods = _ada_mods(cond, mod_w, mod_b).reshape(DEPTH, MOD_ROWS, 1, N_MOD * D)
    bf = lambda w: w.astype(BF16)

    m0 = mods[0]
    xs = _ffn(xs, m0, 0, ffn1_norm[0], bf(ffn1_w_gu[0]), bf(ffn1_w_d[0]), N_ALL_TILES)
    w_in = ev_w_in[0]
    dup = lambda w: jnp.concatenate([w[:, 0:64], w[:, 0:64], w[:, 64:128], w[:, 64:128]], axis=1)
    w_aug = bf(jnp.concatenate([w_in[:, :K_OFF], dup(w_in[:, K_OFF:V_OFF]), dup(w_in[:, V_OFF:])], axis=1))
    rope_c, rope_s1, rope_s2 = _rope_tables()
    u, q, kd, vd = _even_in(xs, m0, mix_norm[0], w_aug, rope_c, rope_s1, rope_s2)
    sink = jnp.broadcast_to(ev_sink[0].astype(F32)[:, None], (ATT_HEADS, 128))
    att = _attention(q, kd, vd, sink)
    conv_w = jnp.concatenate([ev_conv_w[0], jnp.zeros((1, CONV_DIM), F32)], axis=0)
    ev_args = (conv_w, ev_conv_b[0].reshape(1, -1), ev_conv_ln_g[0].reshape(1, -1),
               ev_conv_ln_b[0].reshape(1, -1), bf(ev_w_out[0][:CONV_DIM]), bf(ev_w_out[0][CONV_DIM:]))
    xn = _even_out(xs, m0, u, att, *ev_args, tm=TM, row0=0, n_tiles=N_LAT_TILES,
                   tiles_per_seq=SEQ // TM, mod_row=lambda i: i // (SEQ // TM))
    xs = _even_out(xs, m0, u, att, *ev_args, tm=CTX, row0=T_LAT, n_tiles=BATCH,
                   tiles_per_seq=1, mod_row=lambda i: BATCH, prev=xn)
    xs = _ffn(xs, m0, 2, ffn2_norm[0], bf(ffn2_w_gu[0]), bf(ffn2_w_d[0]), N_ALL_TILES)

    m1 = mods[1]
    xs = _ffn(xs, m1, 0, ffn1_norm[1], bf(ffn1_w_gu[1]), bf(ffn1_w_d[1]), N_ALL_TILES)
    w_in = od_w_in[0]
    w_gate = w_in[:, 4 * ML_DIM:]
    w_gate_pad = jnp.zeros((D, 128), F32).at[:, :N_GATE].set(w_gate)
    b_gate_pad = jnp.zeros((1, 128), F32).at[0, :N_GATE].set(od_b_gate[0])
    qm, km, vm, g_col, g_row = _odd_in(xs, m1, mix_norm[1], bf(w_in[:, :3 * ML_DIM]), w_gate_pad,
                                       w_gate.T, b_gate_pad, od_b_gate[0].reshape(N_GATE, 1))
    hf, hb = _mlstm(qm, km, vm, g_col, g_row)
    xl = _odd_out(xs, m1, mix_norm[1], hf, hb, bf(w_in[:, 3 * ML_DIM:4 * ML_DIM]), od_norm_g[0],
                  bf(od_w_out[0]))
    out = _ffn(xl, m1, 2, ffn2_norm[1], bf(ffn2_w_gu[1]), bf(ffn2_w_d[1]), N_LAT_TILES,
               final_g=final_norm)
    return out.reshape(BATCH, SEQ, D)
```

```python
import functools

import jax
import jax.numpy as jnp
import numpy as np
from jax import lax
from jax.experimental import pallas as pl
from jax.experimental.pallas import tpu as pltpu

D = 1024
BATCH = 4
SEQ = 4096
DEPTH = 2
GRID_W = 64
CTX = 256
N_MOD = 9
D_FF = 2816
EPS = 1e-6
CONV_DIM = 512
CONV_W = 31
CONV_PAD = 15
HEAD_DIM = 64
ATT_HEADS = 8
KV_HEADS = 2
ATT_DIM = 512
KV_DIM = 128
WINDOW = 128
BLK = 128
ROPE_BASE = 10000.0
ROPE_FREQS = 16
Q_OFF = 2 * CONV_DIM
K_OFF = Q_OFF + ATT_DIM
V_OFF = K_OFF + KV_DIM
ML_HEADS = 4
ML_DH = 256
ML_DIM = 1024
ML_CHUNK = 128

T_LAT = BATCH * SEQ
T_CTX = BATCH * CTX
T_ALL = T_LAT + T_CTX
TM = 1024
N_LAT_TILES = T_LAT // TM
N_ALL_TILES = T_ALL // TM
MOD_ROWS = 8
HALO = 16
NEG = -1e30
VMEM_LIMIT = 56 * 1024 * 1024

F32 = jnp.float32
BF16 = jnp.bfloat16


def _sigmoid(x):
    return 1.0 / (1.0 + jnp.exp(-x))


def _silu(x):
    return x * _sigmoid(x)


def _dot(a, b, precision=None):
    return jnp.dot(a, b, preferred_element_type=F32, precision=precision)


def _dot_nt(a, b, precision=None):
    return lax.dot_general(a, b, (((1,), (1,)), ((), ())),
                           preferred_element_type=F32, precision=precision)


def _dot_tn(a, b):
    return lax.dot_general(a, b, (((0,), (0,)), ((), ())), preferred_element_type=F32)


def _rms_mod(x, g, shift, scale):
    y = x * lax.rsqrt(jnp.mean(x * x, axis=-1, keepdims=True) + EPS)
    return (y * g) * (1.0 + scale) + shift


def _resident(shape):
    nd = len(shape)
    return pl.BlockSpec(shape, lambda *_: (0,) * nd, pipeline_mode=pl.Buffered(1))


def _params(n_axes=1):
    return pltpu.CompilerParams(dimension_semantics=("arbitrary",) * n_axes,
                                vmem_limit_bytes=VMEM_LIMIT)


def _mod_kernel(c_ref, w_ref, b_ref, o_ref):
    s = _silu(c_ref[...])
    o_ref[...] = _dot(s, w_ref[...], precision=lax.Precision.HIGHEST) + b_ref[...]


def _ada_mods(cond, mod_w, mod_b):
    tn = 1024
    n = N_MOD * D
    return pl.pallas_call(
        _mod_kernel,
        grid=(DEPTH, n // tn),
        in_specs=[
            pl.BlockSpec((MOD_ROWS, D), lambda l, j: (0, 0)),
            pl.BlockSpec((None, D, tn), lambda l, j: (l, 0, j)),
            pl.BlockSpec((None, 1, tn), lambda l, j: (l, 0, j)),
        ],
        out_specs=pl.BlockSpec((None, MOD_ROWS, tn), lambda l, j: (l, 0, j)),
        out_shape=jax.ShapeDtypeStruct((DEPTH, MOD_ROWS, n), F32),
        compiler_params=_params(2),
        name="ada_mods",
    )(cond, mod_w, mod_b.reshape(DEPTH, 1, n))


def _mod_spec(k):
    return pl.BlockSpec((None, 1, 3 * D), lambda i: (i // (SEQ // TM), 0, k))


FF_CHUNKS = tuple((c, min(c + 512, D_FF)) for c in range(0, D_FF, 512))


def _ffn_kernel(x_ref, m_ref, g_ref, wgu_ref, wd_ref, *rest, final):
    if final:
        fn_ref, o_ref = rest
    else:
        (o_ref,) = rest
    x = x_ref[...]
    shift = m_ref[:, 0:D]
    scale = m_ref[:, D:2 * D]
    gate = m_ref[:, 2 * D:3 * D]
    h = _rms_mod(x, g_ref[...], shift, scale).astype(BF16)
    acc = None
    for c0, c1 in FF_CHUNKS:
        hg = _dot(h, wgu_ref[:, c0:c1])
        hu = _dot(h, wgu_ref[:, D_FF + c0:D_FF + c1])
        a = (_silu(hg) * hu).astype(BF16)
        p = _dot(a, wd_ref[c0:c1, :])
        acc = p if acc is None else acc + p
    y = x + (0.5 * gate) * acc
    if final:
        y = (y * lax.rsqrt(jnp.mean(y * y, axis=-1, keepdims=True) + EPS)) * fn_ref[...]
    o_ref[...] = y


def _ffn(x, mods, k, norm_g, w_gu, w_d, n_tiles, final_g=None):
    final = final_g is not None
    in_specs = [
        pl.BlockSpec((TM, D), lambda i: (i, 0)),
        _mod_spec(k),
        _resident((1, D)),
        _resident((D, 2 * D_FF)),
        _resident((D_FF, D)),
    ]
    args = [x, mods, norm_g.reshape(1, D), w_gu, w_d]
    if final:
        in_specs.append(_resident((1, D)))
        args.append(final_g.reshape(1, D))
    return pl.pallas_call(
        functools.partial(_ffn_kernel, final=final),
        grid=(n_tiles,),
        in_specs=in_specs,
        out_specs=pl.BlockSpec((TM, D), lambda i: (i, 0)),
        out_shape=jax.ShapeDtypeStruct((n_tiles * TM, D), F32),
        compiler_params=_params(1),
        name="ffn_final" if final else "ffn",
    )(*args)


def _even_in_kernel(x_ref, m_ref, g_ref, wvg_ref, wqt_ref, wk_ref, wvt_ref,
                    c_ref, s1_ref, s2_ref, ct_ref, s1t_ref, s2t_ref,
                    u_ref, qt_ref, k_ref, vt_ref):
    x = x_ref[...]
    h = _rms_mod(x, g_ref[...], m_ref[:, 0:D], m_ref[:, D:2 * D]).astype(BF16)
    vg = _dot(h, wvg_ref[...])
    u_ref[...] = vg[:, 0:CONV_DIM] * _sigmoid(vg[:, CONV_DIM:Q_OFF])
    qf = _dot_nt(wqt_ref[...], h)
    ct, s1t, s2t = ct_ref[...], s1t_ref[...], s2t_ref[...]
    for hd in range(ATT_HEADS):
        xh = qf[hd * HEAD_DIM:(hd + 1) * HEAD_DIM]
        up = jnp.concatenate([xh[ROPE_FREQS:], xh[:ROPE_FREQS]], axis=0)
        dn = jnp.concatenate([xh[HEAD_DIM - ROPE_FREQS:], xh[:HEAD_DIM - ROPE_FREQS]], axis=0)
        r = xh * ct + up * s1t + dn * s2t
        qt_ref[hd * HEAD_DIM:(hd + 1) * HEAD_DIM, :] = (r * (HEAD_DIM ** -0.5)).astype(BF16)
    kf = _dot(h, wk_ref[...])
    kr = kf * c_ref[...] + pltpu.roll(kf, 128 - ROPE_FREQS, 1) * s1_ref[...] + pltpu.roll(kf, ROPE_FREQS, 1) * s2_ref[...]
    k_ref[...] = kr.astype(BF16)
    vt_ref[...] = _dot_nt(wvt_ref[...], h).astype(BF16)


def _even_in(x, mods, norm_g, w_vg, w_q_t, w_k, w_v_t, rope, rope_t):
    pos = lambda i: jnp.where(i < N_LAT_TILES, i % (SEQ // TM), SEQ // TM)
    rope_spec = lambda: pl.BlockSpec((TM, KV_DIM), lambda i: (pos(i), 0))
    rope_t_spec = lambda: pl.BlockSpec((HEAD_DIM, TM), lambda i: (0, pos(i)))
    row = lambda i: (i, 0)
    col = lambda i: (0, i)
    return pl.pallas_call(
        _even_in_kernel,
        grid=(N_ALL_TILES,),
        in_specs=[
            pl.BlockSpec((TM, D), row),
            _mod_spec(1),
            _resident((1, D)),
            _resident((D, Q_OFF)),
            _resident((ATT_DIM, D)),
            _resident((D, KV_DIM)),
            _resident((KV_DIM, D)),
            rope_spec(), rope_spec(), rope_spec(),
            rope_t_spec(), rope_t_spec(), rope_t_spec(),
        ],
        out_specs=[
            pl.BlockSpec((TM, CONV_DIM), row),
            pl.BlockSpec((ATT_DIM, TM), col),
            pl.BlockSpec((TM, KV_DIM), row),
            pl.BlockSpec((KV_DIM, TM), col),
        ],
        out_shape=[
            jax.ShapeDtypeStruct((T_ALL, CONV_DIM), F32),
            jax.ShapeDtypeStruct((ATT_DIM, T_ALL), BF16),
            jax.ShapeDtypeStruct((T_ALL, KV_DIM), BF16),
            jax.ShapeDtypeStruct((KV_DIM, T_ALL), BF16),
        ],
        compiler_params=_params(1),
        name="even_in",
    )(x, mods, norm_g.reshape(1, D), w_vg, w_q_t, w_k, w_v_t, *rope, *rope_t)


def _rope_tables():
    rows = SEQ // GRID_W
    row = jnp.repeat(jnp.arange(rows, dtype=F32), GRID_W)
    col = jnp.tile(jnp.arange(GRID_W, dtype=F32), rows)
    inv = ROPE_BASE ** (-jnp.arange(ROPE_FREQS, dtype=F32) / ROPE_FREQS)
    ang = jnp.concatenate([row[:, None] * inv, col[:, None] * inv], axis=-1)
    cos, sin = jnp.cos(ang), jnp.sin(ang)
    d = np.arange(128) % HEAD_DIM
    src = (d // 32) * ROPE_FREQS + d % ROPE_FREQS
    first = jnp.asarray(((d % 32) // ROPE_FREQS) == 0)
    c = cos[:, src]
    s = sin[:, src]
    s1 = jnp.where(first, -s, 0.0)
    s2 = jnp.where(first, 0.0, s)
    pad = lambda t, v: jnp.concatenate([t, jnp.full((TM, 128), v, F32)], axis=0)
    tok = (pad(c, 1.0), pad(s1, 0.0), pad(s2, 0.0))
    return tok, tuple(t[:, :HEAD_DIM].T for t in tok)


NB = SEQ // BLK
NCB = CTX // BLK
GROUP = ATT_HEADS // KV_HEADS


def _attn_kernel(qt_ref, kp_ref, kc_ref, kn_ref, kx_ref, vp_ref, vc_ref, vn_ref, vx_ref,
                 sink_ref, o_ref):
    i = pl.program_id(1)
    kk = lax.broadcasted_iota(jnp.int32, (3 * BLK, BLK), 0)
    qq = lax.broadcasted_iota(jnp.int32, (3 * BLK, BLK), 1)
    rel = kk - BLK - qq
    kpos = (i - 1) * BLK + kk
    ok = (jnp.abs(rel) <= WINDOW) & (kpos >= 0) & (kpos < SEQ) & (i < NB)
    bias1 = jnp.where(ok, 0.0, NEG).astype(F32)
    bias = jnp.concatenate([bias1] * GROUP, axis=1)
    keys = jnp.concatenate([kp_ref[...], kc_ref[...], kn_ref[...], kx_ref[...]], axis=0)
    vt = jnp.concatenate([vp_ref[...], vc_ref[...], vn_ref[...], vx_ref[...]], axis=1)
    qt = qt_ref[...]
    zero = jnp.zeros((HEAD_DIM, BLK), BF16)
    outs = []
    for g in range(KV_HEADS):
        cols = []
        for hh in range(GROUP):
            hd = g * GROUP + hh
            qh = qt[hd * HEAD_DIM:(hd + 1) * HEAD_DIM]
            cols.append(jnp.concatenate([qh, zero] if g == 0 else [zero, qh], axis=0))
        s = _dot(keys, jnp.concatenate(cols, axis=1))
        s_b = s[0:3 * BLK] + bias
        s_x = s[3 * BLK:]
        sink = sink_ref[:, g * GROUP * BLK:(g + 1) * GROUP * BLK]
        m = jnp.maximum(jnp.maximum(jnp.max(s_b, axis=0, keepdims=True),
                                    jnp.max(s_x, axis=0, keepdims=True)), sink)
        p_b = jnp.exp(s_b - m)
        p_x = jnp.exp(s_x - m)
        den = (jnp.sum(p_b, axis=0, keepdims=True) + jnp.sum(p_x, axis=0, keepdims=True)
               + jnp.exp(sink - m))
        p = jnp.concatenate([p_b, p_x], axis=0).astype(BF16)
        ot = _dot(vt[g * HEAD_DIM:(g + 1) * HEAD_DIM], p) * (1.0 / den)
        outs += [ot[:, hh * BLK:(hh + 1) * BLK] for hh in range(GROUP)]
    o_ref[...] = jnp.concatenate(outs, axis=0).T.astype(BF16)


def _attention(qt, k, vt, sink):
    lat_blocks = T_LAT // BLK
    q_blk = lambda b, i: jnp.where(i < NB, b * NB + i, lat_blocks + b * NCB + (i - NB))
    band = lambda off: (lambda b, i: b * NB + jnp.clip(i + off, 0, NB - 1))
    ctx_blk = lambda b, i: T_LAT // CTX + b
    k_spec = lambda n, f: pl.BlockSpec((n, KV_DIM), lambda b, i: (f(b, i), 0))
    v_spec = lambda n, f: pl.BlockSpec((KV_DIM, n), lambda b, i: (0, f(b, i)))
    return pl.pallas_call(
        _attn_kernel,
        grid=(BATCH, NB + NCB),
        in_specs=[
            pl.BlockSpec((ATT_DIM, BLK), lambda b, i: (0, q_blk(b, i))),
            k_spec(BLK, band(-1)), k_spec(BLK, band(0)), k_spec(BLK, band(1)), k_spec(CTX, ctx_blk),
            v_spec(BLK, band(-1)), v_spec(BLK, band(0)), v_spec(BLK, band(1)), v_spec(CTX, ctx_blk),
            pl.BlockSpec((1, ATT_HEADS * BLK), lambda b, i: (0, 0)),
        ],
        out_specs=pl.BlockSpec((BLK, ATT_DIM), lambda b, i: (q_blk(b, i), 0)),
        out_shape=jax.ShapeDtypeStruct((T_ALL, ATT_DIM), BF16),
        compiler_params=_params(2),
        name="window_attn",
    )(qt, k, k, k, k, vt, vt, vt, vt, sink)


CONV_ROWS = 64


def _even_out_kernel(x_ref, m_ref, u_ref, up_ref, un_ref, att_ref, cw_ref, cb_ref,
                     lg_ref, lb_ref, wa_ref, wb_ref, o_ref, ext_ref, win_ref, a_ref, *, tm, tiles_per_seq):
    t = pl.program_id(0)
    first = (t % tiles_per_seq) == 0
    last = (t % tiles_per_seq) == tiles_per_seq - 1
    ext_ref[0:HALO, :] = jnp.where(first, 0.0, up_ref[...])
    ext_ref[HALO:HALO + tm, :] = u_ref[...]
    ext_ref[HALO + tm:, :] = jnp.where(last, 0.0, un_ref[...])
    cb = cb_ref[...]
    lg = lg_ref[...]
    lb = lb_ref[...]

    def body(rb, carry):
        base = pl.multiple_of(rb * CONV_ROWS, CONV_ROWS)
        win_ref[...] = ext_ref[pl.ds(base, CONV_ROWS + 2 * HALO), :]
        acc = jnp.broadcast_to(cb, (CONV_ROWS, CONV_DIM))
        for k in range(CONV_W):
            acc = acc + cw_ref[k:k + 1, :] * win_ref[HALO - CONV_PAD + k:HALO - CONV_PAD + k + CONV_ROWS, :]
        mu = jnp.mean(acc, axis=-1, keepdims=True)
        cen = acc - mu
        var = jnp.mean(cen * cen, axis=-1, keepdims=True)
        y = cen * lax.rsqrt(var + EPS) * lg + lb
        a_ref[pl.ds(base, CONV_ROWS), :] = _silu(y).astype(BF16)
        return carry

    lax.fori_loop(0, tm // CONV_ROWS, body, 0)
    y = _dot(a_ref[...], wa_ref[...]) + _dot(att_ref[...], wb_ref[...])
    o_ref[...] = x_ref[...] + m_ref[:, 2 * D:3 * D] * y


def _even_out(x, mods, u, att, conv_w, conv_b, ln_g, ln_b, wa, wb, *, tm, row0, n_tiles,
              tiles_per_seq, mod_row, prev=None):
    blk0 = row0 // tm
    hb = tm // HALO
    n_halo = T_ALL // HALO
    row_map = lambda i: (blk0 + i, 0)
    in_specs = [
        pl.BlockSpec((tm, D), row_map),
        pl.BlockSpec((None, 1, 3 * D), lambda i: (mod_row(i), 0, 1)),
        pl.BlockSpec((tm, CONV_DIM), row_map),
        pl.BlockSpec((HALO, CONV_DIM), lambda i: (jnp.maximum((blk0 + i) * hb - 1, 0), 0)),
        pl.BlockSpec((HALO, CONV_DIM), lambda i: (jnp.minimum((blk0 + i + 1) * hb, n_halo - 1), 0)),
        pl.BlockSpec((tm, ATT_DIM), row_map),
        _resident((32, CONV_DIM)),
        _resident((1, CONV_DIM)),
        _resident((1, CONV_DIM)),
        _resident((1, CONV_DIM)),
        _resident((CONV_DIM, D)),
        _resident((ATT_DIM, D)),
    ]
    args = [x, mods, u, u, u, att, conv_w, conv_b, ln_g, ln_b, wa, wb]
    aliases = {}
    if prev is not None:
        in_specs.append(pl.BlockSpec(memory_space=pl.ANY))
        args.append(prev)
        aliases = {len(args) - 1: 0}

    def kern(*refs):
        if prev is not None:
            refs = refs[:12] + refs[13:]
        _even_out_kernel(*refs, tm=tm, tiles_per_seq=tiles_per_seq)

    return pl.pallas_call(
        kern,
        grid=(n_tiles,),
        in_specs=in_specs,
        out_specs=pl.BlockSpec((tm, D), row_map),
        out_shape=jax.ShapeDtypeStruct((T_ALL, D), F32),
        scratch_shapes=[pltpu.VMEM((tm + 2 * HALO, CONV_DIM), F32),
                        pltpu.VMEM((CONV_ROWS + 2 * HALO, CONV_DIM), F32),
                        pltpu.VMEM((tm, CONV_DIM), BF16)],
        input_output_aliases=aliases,
        compiler_params=_params(1),
        name="even_out_ctx" if prev is not None else "even_out",
    )(*args)


N_GATE = 4 * ML_HEADS


N_CHAIN = 2 * ML_HEADS


def _log_sigmoid(x):
    return jnp.minimum(x, 0.0) - jnp.log(1.0 + jnp.exp(-jnp.abs(x)))


def _odd_in_kernel(x_ref, m_ref, g_ref, wqt_ref, wk_ref, wvt_ref, wgt_ref, bgt_ref,
                   qt_ref, k_ref, vt_ref, li_ref, lf_ref):
    x = x_ref[...]
    hf = _rms_mod(x, g_ref[...], m_ref[:, 0:D], m_ref[:, D:2 * D])
    h = hf.astype(BF16)
    qt_ref[...] = _dot_nt(wqt_ref[...], h).astype(BF16)
    k_ref[...] = (_dot(h, wk_ref[...]) * (ML_DH ** -0.5)).astype(BF16)
    vt_ref[...] = _dot_nt(wvt_ref[...], h).astype(BF16)
    g = _dot_nt(wgt_ref[...], hf, precision=lax.Precision.HIGHEST) + bgt_ref[...]
    li_ref[...] = g[0:N_CHAIN]
    lf_ref[...] = _log_sigmoid(g[N_CHAIN:N_GATE])


def _odd_in(x, mods, norm_g, w_q_t, w_k, w_v_t, w_gate_t, b_gate_t):
    row = lambda i: (i, 0)
    col = lambda i: (0, i)
    return pl.pallas_call(
        _odd_in_kernel,
        grid=(N_ALL_TILES,),
        in_specs=[
            pl.BlockSpec((TM, D), row),
            _mod_spec(1),
            _resident((1, D)),
            _resident((ML_DIM, D)),
            _resident((D, ML_DIM)),
            _resident((ML_DIM, D)),
            _resident((N_GATE, D)),
            _resident((N_GATE, 1)),
        ],
        out_specs=[
            pl.BlockSpec((ML_DIM, TM), col),
            pl.BlockSpec((TM, ML_DIM), row),
            pl.BlockSpec((ML_DIM, TM), col),
            pl.BlockSpec((N_CHAIN, TM), col),
            pl.BlockSpec((N_CHAIN, TM), col),
        ],
        out_shape=[
            jax.ShapeDtypeStruct((ML_DIM, T_ALL), BF16),
            jax.ShapeDtypeStruct((T_ALL, ML_DIM), BF16),
            jax.ShapeDtypeStruct((ML_DIM, T_ALL), BF16),
            jax.ShapeDtypeStruct((N_CHAIN, T_ALL), F32),
            jax.ShapeDtypeStruct((N_CHAIN, T_ALL), F32),
        ],
        compiler_params=_params(1),
        name="odd_in",
    )(x, mods, norm_g.reshape(1, D), w_q_t, w_k, w_v_t, w_gate_t, b_gate_t)


N_CTX_CHUNKS = CTX // ML_CHUNK
N_LAT_CHUNKS = SEQ // ML_CHUNK
N_STEPS = N_CTX_CHUNKS + N_LAT_CHUNKS
L = ML_CHUNK


N_AUG = 16


def _lane_scan(x, lane, fwd_rows, combine, fill):
    sh = 1
    while sh < L:
        pre = jnp.where(lane >= sh, pltpu.roll(x, sh, 1), fill)
        suf = jnp.where(lane < L - sh, pltpu.roll(x, L - sh, 1), fill)
        x = combine(x, jnp.where(fwd_rows, pre, suf))
        sh *= 2
    return x


def _mlstm_kernel(qtf_ref, kf_ref, vtf_ref, qtb_ref, kb_ref, vtb_ref,
                  lif_ref, lff_ref, lib_ref, lfb_ref, hf_ref, hb_ref, c_ref, m_ref):
    t = pl.program_id(1)

    @pl.when(t == 0)
    def _():
        c_ref[...] = jnp.zeros_like(c_ref)
        m_ref[...] = jnp.zeros_like(m_ref)

    rowid = lax.broadcasted_iota(jnp.int32, (N_CHAIN, L), 0)
    lane = lax.broadcasted_iota(jnp.int32, (N_CHAIN, L), 1)
    fwd_rows = rowid < ML_HEADS
    li = jnp.where(fwd_rows, lif_ref[...], lib_ref[...])
    lf = jnp.where(fwd_rows, lff_ref[...], lfb_ref[...])
    b = _lane_scan(lf, lane, fwd_rows, jnp.add, 0.0)
    a = li - b
    amax = _lane_scan(a, lane, fwd_rows, jnp.maximum, NEG)
    m_old = m_ref[:, 0:1]
    big = jnp.maximum(m_old, jnp.max(a, axis=1, keepdims=True))
    decay = jnp.exp(m_old - big)
    e = jnp.exp(a - big)
    mm = jnp.maximum(amax, m_old)
    w_inter = jnp.exp(m_old - mm)
    floor = jnp.exp(-(b + mm))
    m_ref[...] = jnp.broadcast_to(jnp.sum(lf, axis=1, keepdims=True) + big, (N_CHAIN, 128))
    kq = lax.broadcasted_iota(jnp.int32, (L, L), 0)
    qq = lax.broadcasted_iota(jnp.int32, (L, L), 1)
    a_col = _dot_nt((kq == qq).astype(F32), a, precision=lax.Precision.HIGHEST)

    for d, (qt_ref, k_ref, vt_ref, h_ref) in enumerate(
            ((qtf_ref, kf_ref, vtf_ref, hf_ref), (qtb_ref, kb_ref, vtb_ref, hb_ref))):
        visible = (kq >= qq) if d == 1 else (kq <= qq)
        for hd in range(ML_HEADS):
            c = d * ML_HEADS + hd
            cs = slice(hd * ML_DH, (hd + 1) * ML_DH)
            qt = qt_ref[cs, :]
            k = k_ref[:, cs]
            vt = vt_ref[cs, :]
            p = jnp.where(visible, jnp.exp(a_col[:, c:c + 1] - mm[c:c + 1, :]), 0.0)
            st = _dot(k, qt) * p
            c_old = c_ref[c]
            cq = _dot(c_old.astype(BF16), qt)
            wi = w_inter[c:c + 1, :]
            num = _dot(vt, st.astype(BF16)) + wi * cq[0:ML_DH]
            den = jnp.sum(st, axis=0, keepdims=True) + wi * cq[ML_DH:ML_DH + 1]
            ht = num * (1.0 / jnp.maximum(jnp.abs(den), floor[c:c + 1, :]))
            h_ref[:, cs] = ht.T
            er = e[c:c + 1, :]
            vte = jnp.concatenate([vt.astype(F32) * er, jnp.broadcast_to(er, (N_AUG, L))], axis=0)
            c_ref[c] = decay[c:c + 1, :] * c_old + _dot(vte.astype(BF16), k)


def _mlstm(qt, k, vt, li, lf):
    lat_chunks = T_LAT // L

    def fwd_in(b, t):
        return jnp.where(t < N_CTX_CHUNKS, lat_chunks + b * N_CTX_CHUNKS + t,
                         b * N_LAT_CHUNKS + (t - N_CTX_CHUNKS))

    def bwd_in(b, t):
        return jnp.where(t < N_CTX_CHUNKS, lat_chunks + b * N_CTX_CHUNKS + (N_CTX_CHUNKS - 1 - t),
                         b * N_LAT_CHUNKS + (N_STEPS - 1 - t))

    def fwd_out(b, t):
        return b * N_LAT_CHUNKS + jnp.maximum(t - N_CTX_CHUNKS, 0)

    def bwd_out(b, t):
        return b * N_LAT_CHUNKS + (N_STEPS - 1 - jnp.maximum(t, N_CTX_CHUNKS))

    rows = lambda f: pl.BlockSpec((L, ML_DIM), lambda b, t: (f(b, t), 0))
    cols = lambda f: pl.BlockSpec((ML_DIM, L), lambda b, t: (0, f(b, t)))
    gate = lambda f: pl.BlockSpec((N_CHAIN, L), lambda b, t: (0, f(b, t)))
    return pl.pallas_call(
        _mlstm_kernel,
        grid=(BATCH, N_STEPS),
        in_specs=[cols(fwd_in), rows(fwd_in), cols(fwd_in),
                  cols(bwd_in), rows(bwd_in), cols(bwd_in),
                  gate(fwd_in), gate(fwd_in), gate(bwd_in), gate(bwd_in)],
        out_specs=[rows(fwd_out), rows(bwd_out)],
        out_shape=[jax.ShapeDtypeStruct((T_LAT, ML_DIM), F32),
                   jax.ShapeDtypeStruct((T_LAT, ML_DIM), F32)],
        scratch_shapes=[pltpu.VMEM((N_CHAIN, ML_DH + N_AUG, ML_DH), F32),
                        pltpu.VMEM((N_CHAIN, 128), F32)],
        compiler_params=_params(2),
        name="mlstm",
    )(qt, k, vt, qt, k, vt, li, lf, li, lf)


def _odd_out_kernel(x_ref, m_ref, g_ref, hf_ref, hb_ref, wo_ref, ng_ref, wout_ref, o_ref):
    x = x_ref[...]
    h = _rms_mod(x, g_ref[...], m_ref[:, 0:D], m_ref[:, D:2 * D]).astype(BF16)
    o = _sigmoid(_dot(h, wo_ref[...]))
    hs = hf_ref[...] + hb_ref[...]
    parts = []
    for hd in range(ML_HEADS):
        p = hs[:, hd * ML_DH:(hd + 1) * ML_DH]
        parts.append(p * lax.rsqrt(jnp.mean(p * p, axis=-1, keepdims=True) + EPS))
    hn = jnp.concatenate(parts, axis=-1) * ng_ref[...]
    y = _dot((o * hn).astype(BF16), wout_ref[...])
    o_ref[...] = x + m_ref[:, 2 * D:3 * D] * y


def _odd_out(x, mods, norm_g, hf, hb, w_o, head_g, w_out):
    row = lambda i: (i, 0)
    return pl.pallas_call(
        _odd_out_kernel,
        grid=(N_LAT_TILES,),
        in_specs=[
            pl.BlockSpec((TM, D), row),
            _mod_spec(1),
            _resident((1, D)),
            pl.BlockSpec((TM, ML_DIM), row),
            pl.BlockSpec((TM, ML_DIM), row),
            _resident((D, ML_DIM)),
            _resident((1, ML_DIM)),
            _resident((ML_DIM, D)),
        ],
        out_specs=pl.BlockSpec((TM, D), row),
        out_shape=jax.ShapeDtypeStruct((T_LAT, D), F32),
        compiler_params=_params(1),
        name="odd_out",
    )(x, mods, norm_g.reshape(1, D), hf, hb, w_o, head_g.reshape(1, ML_DIM), w_out)


def kernel(x, c, ctx, c_ctx, mod_w, mod_b, ffn1_norm, ffn1_w_gu, ffn1_w_d, mix_norm, ffn2_norm,
           ffn2_w_gu, ffn2_w_d, ev_w_in, ev_conv_w, ev_conv_b, ev_conv_ln_g, ev_conv_ln_b, ev_sink,
           ev_w_out, od_w_in, od_b_gate, od_norm_g, od_w_out, final_norm):
    assert DEPTH == 2 and x.shape == (BATCH, SEQ, D) and ctx.shape == (BATCH, CTX, D)
    xs = jnp.concatenate([x.reshape(T_LAT, D), ctx.reshape(T_CTX, D)], axis=0)
    cond = jnp.zeros((MOD_ROWS, D), F32).at[:BATCH].set(c).at[BATCH].set(c_ctx)
    mods = _ada_mods(cond, mod_w, mod_b).reshape(DEPTH, MOD_ROWS, 1, N_MOD * D)
    bf = lambda w: w.astype(BF16)

    m0 = mods[0]
    xs = _ffn(xs, m0, 0, ffn1_norm[0], bf(ffn1_w_gu[0]), bf(ffn1_w_d[0]), N_ALL_TILES)
    w_in = ev_w_in[0]
    rope, rope_t = _rope_tables()
    u, qt, kk, vt = _even_in(xs, m0, mix_norm[0], bf(w_in[:, :Q_OFF]), bf(w_in[:, Q_OFF:K_OFF].T),
                             bf(w_in[:, K_OFF:V_OFF]), bf(w_in[:, V_OFF:].T), rope, rope_t)
    sink = jnp.repeat(ev_sink[0].astype(F32), BLK).reshape(1, ATT_HEADS * BLK)
    att = _attention(qt, kk, vt, sink)
    conv_w = jnp.concatenate([ev_conv_w[0], jnp.zeros((1, CONV_DIM), F32)], axis=0)
    ev_args = (conv_w, ev_conv_b[0].reshape(1, -1), ev_conv_ln_g[0].reshape(1, -1),
               ev_conv_ln_b[0].reshape(1, -1), bf(ev_w_out[0][:CONV_DIM]), bf(ev_w_out[0][CONV_DIM:]))
    xn = _even_out(xs, m0, u, att, *ev_args, tm=TM, row0=0, n_tiles=N_LAT_TILES,
                   tiles_per_seq=SEQ // TM, mod_row=lambda i: i // (SEQ // TM))
    xs = _even_out(xs, m0, u, att, *ev_args, tm=CTX, row0=T_LAT, n_tiles=BATCH,
                   tiles_per_seq=1, mod_row=lambda i: BATCH, prev=xn)
    xs = _ffn(xs, m0, 2, ffn2_norm[0], bf(ffn2_w_gu[0]), bf(ffn2_w_d[0]), N_ALL_TILES)

    m1 = mods[1]
    xs = _ffn(xs, m1, 0, ffn1_norm[1], bf(ffn1_w_gu[1]), bf(ffn1_w_d[1]), N_ALL_TILES)
    w_in = od_w_in[0]
    perm = np.concatenate([np.arange(0, 4), np.arange(8, 12), np.arange(4, 8), np.arange(12, 16)])
    w_gate_t = w_in[:, 4 * ML_DIM:].T[perm]
    b_gate_t = od_b_gate[0][perm].reshape(N_GATE, 1)
    qt, km, vt, li, lf = _odd_in(xs, m1, mix_norm[1], bf(w_in[:, :ML_DIM].T), bf(w_in[:, ML_DIM:2 * ML_DIM]),
                                 bf(w_in[:, 2 * ML_DIM:3 * ML_DIM].T), w_gate_t, b_gate_t)
    hf, hb = _mlstm(qt, km, vt, li, lf)
    xl = _odd_out(xs, m1, mix_norm[1], hf, hb, bf(w_in[:, 3 * ML_DIM:4 * ML_DIM]), od_norm_g[0],
                  bf(od_w_out[0]))
    out = _ffn(xl, m1, 2, ffn2_norm[1], bf(ffn2_w_gu[1]), bf(ffn2_w_d[1]), N_LAT_TILES,
               final_g=final_norm)
    return out.reshape(BATCH, SEQ, D)
```

```python
import functools

import jax
import jax.numpy as jnp
import numpy as np
from jax import lax
from jax.experimental import pallas as pl
from jax.experimental.pallas import tpu as pltpu

D = 1024
BATCH = 4
SEQ = 4096
DEPTH = 2
GRID_W = 64
CTX = 256
N_MOD = 9
D_FF = 2816
EPS = 1e-6
CONV_DIM = 512
CONV_W = 31
CONV_PAD = 15
HEAD_DIM = 64
ATT_HEADS = 8
KV_HEADS = 2
ATT_DIM = 512
KV_DIM = 128
WINDOW = 128
BLK = 128
ROPE_BASE = 10000.0
ROPE_FREQS = 16
Q_OFF = 2 * CONV_DIM
K_OFF = Q_OFF + ATT_DIM
V_OFF = K_OFF + KV_DIM
ML_HEADS = 4
ML_DH = 256
ML_DIM = 1024
ML_CHUNK = 128

T_LAT = BATCH * SEQ
T_CTX = BATCH * CTX
T_ALL = T_LAT + T_CTX
TM = 1024
N_LAT_TILES = T_LAT // TM
N_ALL_TILES = T_ALL // TM
MOD_ROWS = 8
HALO = 16
NEG = -1e30
VMEM_LIMIT = 56 * 1024 * 1024

F32 = jnp.float32
BF16 = jnp.bfloat16


def _sigmoid(x):
    return 1.0 / (1.0 + jnp.exp(-x))


def _silu(x):
    return x * _sigmoid(x)


def _dot(a, b, precision=None):
    return jnp.dot(a, b, preferred_element_type=F32, precision=precision)


def _dot_nt(a, b, precision=None):
    return lax.dot_general(a, b, (((1,), (1,)), ((), ())),
                           preferred_element_type=F32, precision=precision)


def _dot_tn(a, b):
    return lax.dot_general(a, b, (((0,), (0,)), ((), ())), preferred_element_type=F32)


def _rms_mod(x, g, shift, scale):
    y = x * lax.rsqrt(jnp.mean(x * x, axis=-1, keepdims=True) + EPS)
    return (y * g) * (1.0 + scale) + shift


def _resident(shape):
    nd = len(shape)
    return pl.BlockSpec(shape, lambda *_: (0,) * nd, pipeline_mode=pl.Buffered(1))


def _params(n_axes=1):
    return pltpu.CompilerParams(dimension_semantics=("arbitrary",) * n_axes,
                                vmem_limit_bytes=VMEM_LIMIT)


def _mod_kernel(c_ref, w_ref, b_ref, o_ref):
    s = _silu(c_ref[...]).astype(BF16)
    o_ref[...] = _dot(s, w_ref[...].astype(BF16)) + b_ref[...]


def _ada_mods(cond, mod_w, mod_b):
    tn = 1024
    n = N_MOD * D
    return pl.pallas_call(
        _mod_kernel,
        grid=(DEPTH, n // tn),
        in_specs=[
            pl.BlockSpec((MOD_ROWS, D), lambda l, j: (0, 0)),
            pl.BlockSpec((None, D, tn), lambda l, j: (l, 0, j)),
            pl.BlockSpec((None, 1, tn), lambda l, j: (l, 0, j)),
        ],
        out_specs=pl.BlockSpec((None, MOD_ROWS, tn), lambda l, j: (l, 0, j)),
        out_shape=jax.ShapeDtypeStruct((DEPTH, MOD_ROWS, n), F32),
        compiler_params=_params(2),
        name="ada_mods",
    )(cond, mod_w, mod_b.reshape(DEPTH, 1, n))


def _mod_spec(k):
    return pl.BlockSpec((None, 1, 3 * D), lambda i: (i // (SEQ // TM), 0, k))


FF_CHUNKS = tuple((c, min(c + 512, D_FF)) for c in range(0, D_FF, 512))


def _ffn_kernel(x_ref, *rest, split, final):
    if split:
        xc_ref, *rest = rest
    m_ref, g_ref, wgu_ref, wd_ref, *rest = rest
    if final:
        fn_ref, o_ref = rest
    else:
        (o_ref,) = rest
    x = x_ref[...]
    if split:
        x = jnp.where(pl.program_id(0) < N_LAT_TILES, x, xc_ref[...])
    shift = m_ref[:, 0:D]
    scale = m_ref[:, D:2 * D]
    gate = m_ref[:, 2 * D:3 * D]
    h = _rms_mod(x, g_ref[...], shift, scale).astype(BF16)
    acc = None
    for c0, c1 in FF_CHUNKS:
        hg = _dot(h, wgu_ref[:, c0:c1])
        hu = _dot(h, wgu_ref[:, D_FF + c0:D_FF + c1])
        a = (_silu(hg) * hu).astype(BF16)
        p = _dot(a, wd_ref[c0:c1, :])
        acc = p if acc is None else acc + p
    y = x + (0.5 * gate) * acc
    if final:
        y = (y * lax.rsqrt(jnp.mean(y * y, axis=-1, keepdims=True) + EPS)) * fn_ref[...]
    o_ref[...] = y


def _ffn(x, mods, k, norm_g, w_gu, w_d, n_tiles, final_g=None, x_ctx=None):
    final = final_g is not None
    split = x_ctx is not None
    if split:
        in_specs = [pl.BlockSpec((TM, D), lambda i: (jnp.minimum(i, N_LAT_TILES - 1), 0)),
                    pl.BlockSpec((TM, D), lambda i: (0, 0))]
        args = [x, x_ctx]
    else:
        in_specs = [pl.BlockSpec((TM, D), lambda i: (i, 0))]
        args = [x]
    in_specs += [
        _mod_spec(k),
        _resident((1, D)),
        _resident((D, 2 * D_FF)),
        _resident((D_FF, D)),
    ]
    args += [mods, norm_g.reshape(1, D), w_gu, w_d]
    if final:
        in_specs.append(_resident((1, D)))
        args.append(final_g.reshape(1, D))
    return pl.pallas_call(
        functools.partial(_ffn_kernel, split=split, final=final),
        grid=(n_tiles,),
        in_specs=in_specs,
        out_specs=pl.BlockSpec((TM, D), lambda i: (i, 0)),
        out_shape=jax.ShapeDtypeStruct((n_tiles * TM, D), F32),
        compiler_params=_params(1),
        name="ffn_final" if final else ("ffn_first" if split else "ffn"),
    )(*args)


def _even_in_kernel(x_ref, m_ref, g_ref, wvg_ref, wqt_ref, wk_ref, wvt_ref,
                    c_ref, s1_ref, s2_ref, ct_ref, s1t_ref, s2t_ref,
                    u_ref, qt_ref, k_ref, vt_ref):
    x = x_ref[...]
    h = _rms_mod(x, g_ref[...], m_ref[:, 0:D], m_ref[:, D:2 * D]).astype(BF16)
    vg = _dot(h, wvg_ref[...])
    u_ref[...] = vg[:, 0:CONV_DIM] * _sigmoid(vg[:, CONV_DIM:Q_OFF])
    qf = _dot_nt(wqt_ref[...], h)
    ct, s1t, s2t = ct_ref[...], s1t_ref[...], s2t_ref[...]
    for hd in range(ATT_HEADS):
        xh = qf[hd * HEAD_DIM:(hd + 1) * HEAD_DIM]
        up = jnp.concatenate([xh[ROPE_FREQS:], xh[:ROPE_FREQS]], axis=0)
        dn = jnp.concatenate([xh[HEAD_DIM - ROPE_FREQS:], xh[:HEAD_DIM - ROPE_FREQS]], axis=0)
        r = xh * ct + up * s1t + dn * s2t
        qt_ref[hd * HEAD_DIM:(hd + 1) * HEAD_DIM, :] = (r * (HEAD_DIM ** -0.5)).astype(BF16)
    kf = _dot(h, wk_ref[...])
    kr = kf * c_ref[...] + pltpu.roll(kf, 128 - ROPE_FREQS, 1) * s1_ref[...] + pltpu.roll(kf, ROPE_FREQS, 1) * s2_ref[...]
    k_ref[...] = kr.astype(BF16)
    vt_ref[...] = _dot_nt(wvt_ref[...], h).astype(BF16)


def _even_in(x, mods, norm_g, w_vg, w_q_t, w_k, w_v_t, rope, rope_t):
    pos = lambda i: jnp.where(i < N_LAT_TILES, i % (SEQ // TM), SEQ // TM)
    rope_spec = lambda: pl.BlockSpec((TM, KV_DIM), lambda i: (pos(i), 0))
    rope_t_spec = lambda: pl.BlockSpec((HEAD_DIM, TM), lambda i: (0, pos(i)))
    row = lambda i: (i, 0)
    col = lambda i: (0, i)
    return pl.pallas_call(
        _even_in_kernel,
        grid=(N_ALL_TILES,),
        in_specs=[
            pl.BlockSpec((TM, D), row),
            _mod_spec(1),
            _resident((1, D)),
            _resident((D, Q_OFF)),
            _resident((ATT_DIM, D)),
            _resident((D, KV_DIM)),
            _resident((KV_DIM, D)),
            rope_spec(), rope_spec(), rope_spec(),
            rope_t_spec(), rope_t_spec(), rope_t_spec(),
        ],
        out_specs=[
            pl.BlockSpec((TM, CONV_DIM), row),
            pl.BlockSpec((ATT_DIM, TM), col),
            pl.BlockSpec((TM, KV_DIM), row),
            pl.BlockSpec((KV_DIM, TM), col),
        ],
        out_shape=[
            jax.ShapeDtypeStruct((T_ALL, CONV_DIM), F32),
            jax.ShapeDtypeStruct((ATT_DIM, T_ALL), BF16),
            jax.ShapeDtypeStruct((T_ALL, KV_DIM), BF16),
            jax.ShapeDtypeStruct((KV_DIM, T_ALL), BF16),
        ],
        compiler_params=_params(1),
        name="even_in",
    )(x, mods, norm_g.reshape(1, D), w_vg, w_q_t, w_k, w_v_t, *rope, *rope_t)


def _rope_tables():
    rows = SEQ // GRID_W
    row = jnp.repeat(jnp.arange(rows, dtype=F32), GRID_W)
    col = jnp.tile(jnp.arange(GRID_W, dtype=F32), rows)
    inv = ROPE_BASE ** (-jnp.arange(ROPE_FREQS, dtype=F32) / ROPE_FREQS)
    ang = jnp.concatenate([row[:, None] * inv, col[:, None] * inv], axis=-1)
    cos, sin = jnp.cos(ang), jnp.sin(ang)
    d = np.arange(128) % HEAD_DIM
    src = (d // 32) * ROPE_FREQS + d % ROPE_FREQS
    first = jnp.asarray(((d % 32) // ROPE_FREQS) == 0)
    c = cos[:, src]
    s = sin[:, src]
    s1 = jnp.where(first, -s, 0.0)
    s2 = jnp.where(first, 0.0, s)
    pad = lambda t, v: jnp.concatenate([t, jnp.full((TM, 128), v, F32)], axis=0)
    tok = (pad(c, 1.0), pad(s1, 0.0), pad(s2, 0.0))
    return tok, tuple(t[:, :HEAD_DIM].T for t in tok)


NB = SEQ // BLK
NCB = CTX // BLK
GROUP = ATT_HEADS // KV_HEADS


def _attn_kernel(qt_ref, kp_ref, kc_ref, kn_ref, kx_ref, vp_ref, vc_ref, vn_ref, vx_ref,
                 sink_ref, o_ref):
    i = pl.program_id(1)
    kk = lax.broadcasted_iota(jnp.int32, (3 * BLK, BLK), 0)
    qq = lax.broadcasted_iota(jnp.int32, (3 * BLK, BLK), 1)
    rel = kk - BLK - qq
    kpos = (i - 1) * BLK + kk
    ok = (jnp.abs(rel) <= WINDOW) & (kpos >= 0) & (kpos < SEQ) & (i < NB)
    bias1 = jnp.where(ok, 0.0, NEG).astype(F32)
    bias = jnp.concatenate([bias1] * GROUP, axis=1)
    keys = jnp.concatenate([kp_ref[...], kc_ref[...], kn_ref[...], kx_ref[...]], axis=0)
    vt = jnp.concatenate([vp_ref[...], vc_ref[...], vn_ref[...], vx_ref[...]], axis=1)
    qt = qt_ref[...]
    zero = jnp.zeros((HEAD_DIM, BLK), BF16)
    outs = []
    for g in range(KV_HEADS):
        cols = []
        for hh in range(GROUP):
            hd = g * GROUP + hh
            qh = qt[hd * HEAD_DIM:(hd + 1) * HEAD_DIM]
            cols.append(jnp.concatenate([qh, zero] if g == 0 else [zero, qh], axis=0))
        s = _dot(keys, jnp.concatenate(cols, axis=1))
        s_b = s[0:3 * BLK] + bias
        s_x = s[3 * BLK:]
        sink = sink_ref[:, g * GROUP * BLK:(g + 1) * GROUP * BLK]
        m = jnp.maximum(jnp.maximum(jnp.max(s_b, axis=0, keepdims=True),
                                    jnp.max(s_x, axis=0, keepdims=True)), sink)
        p_b = jnp.exp(s_b - m)
        p_x = jnp.exp(s_x - m)
        den = (jnp.sum(p_b, axis=0, keepdims=True) + jnp.sum(p_x, axis=0, keepdims=True)
               + jnp.exp(sink - m))
        p = jnp.concatenate([p_b, p_x], axis=0).astype(BF16)
        ot = _dot(vt[g * HEAD_DIM:(g + 1) * HEAD_DIM], p) * (1.0 / den)
        outs += [ot[:, hh * BLK:(hh + 1) * BLK] for hh in range(GROUP)]
    o_ref[...] = jnp.concatenate(outs, axis=0).T.astype(BF16)


def _attention(qt, k, vt, sink):
    lat_blocks = T_LAT // BLK
    q_blk = lambda b, i: jnp.where(i < NB, b * NB + i, lat_blocks + b * NCB + (i - NB))
    band = lambda off: (lambda b, i: b * NB + jnp.clip(i + off, 0, NB - 1))
    ctx_blk = lambda b, i: T_LAT // CTX + b
    k_spec = lambda n, f: pl.BlockSpec((n, KV_DIM), lambda b, i: (f(b, i), 0))
    v_spec = lambda n, f: pl.BlockSpec((KV_DIM, n), lambda b, i: (0, f(b, i)))
    return pl.pallas_call(
        _attn_kernel,
        grid=(BATCH, NB + NCB),
        in_specs=[
            pl.BlockSpec((ATT_DIM, BLK), lambda b, i: (0, q_blk(b, i))),
            k_spec(BLK, band(-1)), k_spec(BLK, band(0)), k_spec(BLK, band(1)), k_spec(CTX, ctx_blk),
            v_spec(BLK, band(-1)), v_spec(BLK, band(0)), v_spec(BLK, band(1)), v_spec(CTX, ctx_blk),
            pl.BlockSpec((1, ATT_HEADS * BLK), lambda b, i: (0, 0)),
        ],
        out_specs=pl.BlockSpec((BLK, ATT_DIM), lambda b, i: (q_blk(b, i), 0)),
        out_shape=jax.ShapeDtypeStruct((T_ALL, ATT_DIM), BF16),
        compiler_params=_params(2),
        name="window_attn",
    )(qt, k, k, k, k, vt, vt, vt, vt, sink)


CONV_S = 4
CONV_ROWS = 8 * CONV_S
N_SLAB = CONV_DIM // 128


def _even_out_kernel(x_ref, m_ref, u_ref, up_ref, un_ref, att_ref, cw_ref, cb_ref,
                     lg_ref, lb_ref, wa_ref, wb_ref, o_ref, ext_ref, a_ref, cva_ref, cvb_ref, *, tm, tiles_per_seq):
    t = pl.program_id(0)
    first = (t % tiles_per_seq) == 0
    last = (t % tiles_per_seq) == tiles_per_seq - 1
    slabs = [slice(s * 128, (s + 1) * 128) for s in range(N_SLAB)]
    for s, ls in enumerate(slabs):
        ext_ref[s, 0:HALO, :] = jnp.where(first, 0.0, up_ref[:, ls])
        ext_ref[s, HALO:HALO + tm, :] = u_ref[:, ls]
        ext_ref[s, HALO + tm:, :] = jnp.where(last, 0.0, un_ref[:, ls])

    def conv_block(blk, cv_ref):
        base = blk * CONV_ROWS
        for s, ls in enumerate(slabs):
            acc = [jnp.broadcast_to(cb_ref[:, ls], (8, 128)) for _ in range(CONV_S)]
            for o in range(CONV_W + CONV_S - 1):
                v = ext_ref[s, pl.ds(base + (HALO - CONV_PAD) + o, 8, stride=CONV_S), :]
                for j in range(CONV_S):
                    k = o - j
                    if 0 <= k < CONV_W:
                        acc[j] = acc[j] + cw_ref[k:k + 1, ls] * v
            for j in range(CONV_S):
                cv_ref[s * CONV_S + j] = acc[j]

    def norm_block(blk, cv_ref):
        base = blk * CONV_ROWS
        for j in range(CONV_S):
            row = [cv_ref[s * CONV_S + j] for s in range(N_SLAB)]
            mu = jnp.sum(sum(row), axis=-1, keepdims=True) * (1.0 / CONV_DIM)
            cen = [r - mu for r in row]
            var = jnp.sum(sum(c * c for c in cen), axis=-1, keepdims=True) * (1.0 / CONV_DIM)
            rs = lax.rsqrt(var + EPS)
            for s, ls in enumerate(slabs):
                y = cen[s] * rs * lg_ref[:, ls] + lb_ref[:, ls]
                a_ref[s, pl.ds(base + j, 8, stride=CONV_S), :] = _silu(y)

    n_blk = tm // CONV_ROWS
    conv_block(0, cva_ref)

    def body(i, carry):
        conv_block(2 * i + 1, cvb_ref)
        norm_block(2 * i, cva_ref)
        conv_block(jnp.minimum(2 * i + 2, n_blk - 1), cva_ref)
        norm_block(2 * i + 1, cvb_ref)
        return carry

    lax.fori_loop(0, n_blk // 2, body, 0)
    a = jnp.concatenate([a_ref[s] for s in range(N_SLAB)], axis=1).astype(BF16)
    y = _dot(a, wa_ref[...]) + _dot(att_ref[...], wb_ref[...])
    o_ref[...] = x_ref[...] + m_ref[:, 2 * D:3 * D] * y


def _even_out(x, mods, u, att, conv_w, conv_b, ln_g, ln_b, wa, wb, *, tm, row0, n_tiles,
              tiles_per_seq, mod_row):
    blk0 = row0 // tm
    hb = tm // HALO
    n_halo = T_ALL // HALO
    row_map = lambda i: (blk0 + i, 0)
    in_specs = [
        pl.BlockSpec((tm, D), row_map),
        pl.BlockSpec((None, 1, 3 * D), lambda i: (mod_row(i), 0, 1)),
        pl.BlockSpec((tm, CONV_DIM), row_map),
        pl.BlockSpec((HALO, CONV_DIM), lambda i: (jnp.maximum((blk0 + i) * hb - 1, 0), 0)),
        pl.BlockSpec((HALO, CONV_DIM), lambda i: (jnp.minimum((blk0 + i + 1) * hb, n_halo - 1), 0)),
        pl.BlockSpec((tm, ATT_DIM), row_map),
        _resident((32, CONV_DIM)),
        _resident((1, CONV_DIM)),
        _resident((1, CONV_DIM)),
        _resident((1, CONV_DIM)),
        _resident((CONV_DIM, D)),
        _resident((ATT_DIM, D)),
    ]
    return pl.pallas_call(
        functools.partial(_even_out_kernel, tm=tm, tiles_per_seq=tiles_per_seq),
        grid=(n_tiles,),
        in_specs=in_specs,
        out_specs=pl.BlockSpec((tm, D), row_map),
        out_shape=jax.ShapeDtypeStruct((T_ALL, D), F32),
        scratch_shapes=[pltpu.VMEM((N_SLAB, tm + 2 * HALO, 128), F32),
                        pltpu.VMEM((N_SLAB, tm, 128), F32),
                        pltpu.VMEM((N_SLAB * CONV_S, 8, 128), F32),
                        pltpu.VMEM((N_SLAB * CONV_S, 8, 128), F32)],
        input_output_aliases={0: 0},
        compiler_params=_params(1),
        name="even_out" if tiles_per_seq > 1 else "even_out_ctx",
    )(x, mods, u, u, u, att, conv_w, conv_b, ln_g, ln_b, wa, wb)


N_GATE = 4 * ML_HEADS
N_CHAIN = 2 * ML_HEADS
L = ML_CHUNK


def _log_sigmoid(x):
    return jnp.minimum(x, 0.0) - jnp.log(1.0 + jnp.exp(-jnp.abs(x)))


def _lane_scan(x, lane, fwd_rows, combine, fill):
    sh = 1
    while sh < L:
        pre = jnp.where(lane >= sh, pltpu.roll(x, sh, 1), fill)
        suf = jnp.where(lane < L - sh, pltpu.roll(x, L - sh, 1), fill)
        x = combine(x, jnp.where(fwd_rows, pre, suf))
        sh *= 2
    return x


def _odd_in_kernel(x_ref, m_ref, g_ref, wqt_ref, wk_ref, wvt_ref, wgt_ref, bgt_ref,
                   qt_ref, k_ref, vt_ref, a_ref, amax_ref, b_ref):
    x = x_ref[...]
    hf = _rms_mod(x, g_ref[...], m_ref[:, 0:D], m_ref[:, D:2 * D])
    h = hf.astype(BF16)
    g = _dot_nt(wgt_ref[...], hf, precision=lax.Precision.HIGHEST) + bgt_ref[...]
    qt_ref[...] = _dot_nt(wqt_ref[...], h).astype(BF16)
    k_ref[...] = (_dot(h, wk_ref[...]) * (ML_DH ** -0.5)).astype(BF16)
    vt_ref[...] = _dot_nt(wvt_ref[...], h).astype(BF16)
    li = g[0:N_CHAIN]
    lf = _log_sigmoid(g[N_CHAIN:N_GATE])
    fwd_rows = lax.broadcasted_iota(jnp.int32, (N_CHAIN, L), 0) < ML_HEADS
    lane = lax.broadcasted_iota(jnp.int32, (N_CHAIN, L), 1)
    for ch in range(TM // L):
        cs = slice(ch * L, (ch + 1) * L)
        b = _lane_scan(lf[:, cs], lane, fwd_rows, jnp.add, 0.0)
        a = li[:, cs] - b
        a_ref[:, cs] = a
        amax_ref[:, cs] = _lane_scan(a, lane, fwd_rows, jnp.maximum, NEG)
        b_ref[:, cs] = b


def _odd_in(x, mods, norm_g, w_q_t, w_k, w_v_t, w_gate_t, b_gate_t):
    row = lambda i: (i, 0)
    col = lambda i: (0, i)
    return pl.pallas_call(
        _odd_in_kernel,
        grid=(N_ALL_TILES,),
        in_specs=[
            pl.BlockSpec((TM, D), row),
            _mod_spec(1),
            _resident((1, D)),
            _resident((ML_DIM, D)),
            _resident((D, ML_DIM)),
            _resident((ML_DIM, D)),
            _resident((N_GATE, D)),
            _resident((N_GATE, 1)),
        ],
        out_specs=[
            pl.BlockSpec((ML_DIM, TM), col),
            pl.BlockSpec((TM, ML_DIM), row),
            pl.BlockSpec((ML_DIM, TM), col),
            pl.BlockSpec((N_CHAIN, TM), col),
            pl.BlockSpec((N_CHAIN, TM), col),
            pl.BlockSpec((N_CHAIN, TM), col),
        ],
        out_shape=[
            jax.ShapeDtypeStruct((ML_DIM, T_ALL), BF16),
            jax.ShapeDtypeStruct((T_ALL, ML_DIM), BF16),
            jax.ShapeDtypeStruct((ML_DIM, T_ALL), BF16),
            jax.ShapeDtypeStruct((N_CHAIN, T_ALL), F32),
            jax.ShapeDtypeStruct((N_CHAIN, T_ALL), F32),
            jax.ShapeDtypeStruct((N_CHAIN, T_ALL), F32),
        ],
        compiler_params=_params(1),
        name="odd_in",
    )(x, mods, norm_g.reshape(1, D), w_q_t, w_k, w_v_t, w_gate_t, b_gate_t)


N_CTX_CHUNKS = CTX // ML_CHUNK
N_LAT_CHUNKS = SEQ // ML_CHUNK
N_STEPS = N_CTX_CHUNKS + N_LAT_CHUNKS
N_AUG = 16


def _mlstm_kernel(qtf_ref, kf_ref, vtf_ref, qtb_ref, kb_ref, vtb_ref,
                  af_ref, amaxf_ref, bf_ref, ab_ref, amaxb_ref, bb_ref, hf_ref, hb_ref, c_ref, m_ref):
    t = pl.program_id(1)

    @pl.when(t == 0)
    def _():
        c_ref[...] = jnp.zeros_like(c_ref)
        m_ref[...] = jnp.zeros_like(m_ref)

    fwd_rows = lax.broadcasted_iota(jnp.int32, (N_CHAIN, L), 0) < ML_HEADS
    a = jnp.where(fwd_rows, af_ref[...], ab_ref[...])
    amax = jnp.where(fwd_rows, amaxf_ref[...], amaxb_ref[...])
    b = jnp.where(fwd_rows, bf_ref[...], bb_ref[...])
    m_old = m_ref[:, 0:1]
    big = jnp.maximum(m_old, jnp.max(amax, axis=1, keepdims=True))
    decay = jnp.exp(m_old - big)
    e = jnp.exp(a - big)
    mm = jnp.maximum(amax, m_old)
    w_inter = jnp.exp(m_old - mm)
    floor = jnp.exp(-(b + mm))
    m_ref[...] = jnp.broadcast_to(jnp.min(b, axis=1, keepdims=True) + big, (N_CHAIN, 128))
    kq = lax.broadcasted_iota(jnp.int32, (L, L), 0)
    qq = lax.broadcasted_iota(jnp.int32, (L, L), 1)
    a_col = _dot_nt((kq == qq).astype(F32), a, precision=lax.Precision.HIGHEST)

    for d, (qt_ref, k_ref, vt_ref, h_ref) in enumerate(
            ((qtf_ref, kf_ref, vtf_ref, hf_ref), (qtb_ref, kb_ref, vtb_ref, hb_ref))):
        visible = (kq >= qq) if d == 1 else (kq <= qq)
        for hd in range(ML_HEADS):
            c = d * ML_HEADS + hd
            cs = slice(hd * ML_DH, (hd + 1) * ML_DH)
            qt = qt_ref[cs, :]
            k = k_ref[:, cs]
            vt = vt_ref[cs, :]
            p = jnp.where(visible, jnp.exp(a_col[:, c:c + 1] - mm[c:c + 1, :]), 0.0)
            st = _dot(k, qt) * p
            c_old = c_ref[c]
            cq = _dot(c_old.astype(BF16), qt)
            wi = w_inter[c:c + 1, :]
            num = _dot(vt, st.astype(BF16)) + wi * cq[0:ML_DH]
            den = jnp.sum(st, axis=0, keepdims=True) + wi * cq[ML_DH:ML_DH + 1]
            ht = num * (1.0 / jnp.maximum(jnp.abs(den), floor[c:c + 1, :]))
            h_ref[:, cs] = ht.T
            er = e[c:c + 1, :]
            vte = jnp.concatenate([vt.astype(F32) * er, jnp.broadcast_to(er, (N_AUG, L))], axis=0)
            c_ref[c] = decay[c:c + 1, :] * c_old + _dot(vte.astype(BF16), k)


def _mlstm(qt, k, vt, a, amax, b):
    lat_chunks = T_LAT // L

    def fwd_in(b, t):
        return jnp.where(t < N_CTX_CHUNKS, lat_chunks + b * N_CTX_CHUNKS + t,
                         b * N_LAT_CHUNKS + (t - N_CTX_CHUNKS))

    def bwd_in(b, t):
        return jnp.where(t < N_CTX_CHUNKS, lat_chunks + b * N_CTX_CHUNKS + (N_CTX_CHUNKS - 1 - t),
                         b * N_LAT_CHUNKS + (N_STEPS - 1 - t))

    def fwd_out(b, t):
        return b * N_LAT_CHUNKS + jnp.maximum(t - N_CTX_CHUNKS, 0)

    def bwd_out(b, t):
        return b * N_LAT_CHUNKS + (N_STEPS - 1 - jnp.maximum(t, N_CTX_CHUNKS))

    rows = lambda f: pl.BlockSpec((L, ML_DIM), lambda b, t: (f(b, t), 0))
    cols = lambda f: pl.BlockSpec((ML_DIM, L), lambda b, t: (0, f(b, t)))
    gate = lambda f: pl.BlockSpec((N_CHAIN, L), lambda b, t: (0, f(b, t)))
    return pl.pallas_call(
        _mlstm_kernel,
        grid=(BATCH, N_STEPS),
        in_specs=[cols(fwd_in), rows(fwd_in), cols(fwd_in),
                  cols(bwd_in), rows(bwd_in), cols(bwd_in),
                  gate(fwd_in), gate(fwd_in), gate(fwd_in), gate(bwd_in), gate(bwd_in), gate(bwd_in)],
        out_specs=[rows(fwd_out), rows(bwd_out)],
        out_shape=[jax.ShapeDtypeStruct((T_LAT, ML_DIM), F32),
                   jax.ShapeDtypeStruct((T_LAT, ML_DIM), F32)],
        scratch_shapes=[pltpu.VMEM((N_CHAIN, ML_DH + N_AUG, ML_DH), F32),
                        pltpu.VMEM((N_CHAIN, 128), F32)],
        compiler_params=_params(2),
        name="mlstm",
    )(qt, k, vt, qt, k, vt, a, amax, b, a, amax, b)


def _odd_out_kernel(x_ref, m_ref, g_ref, hf_ref, hb_ref, wo_ref, ng_ref, wout_ref, o_ref):
    x = x_ref[...]
    h = _rms_mod(x, g_ref[...], m_ref[:, 0:D], m_ref[:, D:2 * D]).astype(BF16)
    o = _sigmoid(_dot(h, wo_ref[...]))
    hs = hf_ref[...] + hb_ref[...]
    parts = []
    for hd in range(ML_HEADS):
        p = hs[:, hd * ML_DH:(hd + 1) * ML_DH]
        parts.append(p * lax.rsqrt(jnp.mean(p * p, axis=-1, keepdims=True) + EPS))
    hn = jnp.concatenate(parts, axis=-1) * ng_ref[...]
    y = _dot((o * hn).astype(BF16), wout_ref[...])
    o_ref[...] = x + m_ref[:, 2 * D:3 * D] * y


def _odd_out(x, mods, norm_g, hf, hb, w_o, head_g, w_out):
    row = lambda i: (i, 0)
    return pl.pallas_call(
        _odd_out_kernel,
        grid=(N_LAT_TILES,),
        in_specs=[
            pl.BlockSpec((TM, D), row),
            _mod_spec(1),
            _resident((1, D)),
            pl.BlockSpec((TM, ML_DIM), row),
            pl.BlockSpec((TM, ML_DIM), row),
            _resident((D, ML_DIM)),
            _resident((1, ML_DIM)),
            _resident((ML_DIM, D)),
        ],
        out_specs=pl.BlockSpec((TM, D), row),
        out_shape=jax.ShapeDtypeStruct((T_LAT, D), F32),
        compiler_params=_params(1),
        name="odd_out",
    )(x, mods, norm_g.reshape(1, D), hf, hb, w_o, head_g.reshape(1, ML_DIM), w_out)


def kernel(x, c, ctx, c_ctx, mod_w, mod_b, ffn1_norm, ffn1_w_gu, ffn1_w_d, mix_norm, ffn2_norm,
           ffn2_w_gu, ffn2_w_d, ev_w_in, ev_conv_w, ev_conv_b, ev_conv_ln_g, ev_conv_ln_b, ev_sink,
           ev_w_out, od_w_in, od_b_gate, od_norm_g, od_w_out, final_norm):
    assert DEPTH == 2 and x.shape == (BATCH, SEQ, D) and ctx.shape == (BATCH, CTX, D)
    cond = jnp.zeros((MOD_ROWS, D), F32).at[:BATCH].set(c).at[BATCH].set(c_ctx)
    mods = _ada_mods(cond, mod_w, mod_b).reshape(DEPTH, MOD_ROWS, 1, N_MOD * D)
    bf = lambda w: w.astype(BF16)

    m0 = mods[0]
    xs = _ffn(x.reshape(T_LAT, D), m0, 0, ffn1_norm[0], bf(ffn1_w_gu[0]), bf(ffn1_w_d[0]), N_ALL_TILES,
              x_ctx=ctx.reshape(T_CTX, D))
    w_in = ev_w_in[0]
    rope, rope_t = _rope_tables()
    u, qt, kk, vt = _even_in(xs, m0, mix_norm[0], bf(w_in[:, :Q_OFF]), bf(w_in[:, Q_OFF:K_OFF].T),
                             bf(w_in[:, K_OFF:V_OFF]), bf(w_in[:, V_OFF:].T), rope, rope_t)
    sink = jnp.repeat(ev_sink[0].astype(F32), BLK).reshape(1, ATT_HEADS * BLK)
    att = _attention(qt, kk, vt, sink)
    conv_w = jnp.concatenate([ev_conv_w[0], jnp.zeros((1, CONV_DIM), F32)], axis=0)
    ev_args = (conv_w, ev_conv_b[0].reshape(1, -1), ev_conv_ln_g[0].reshape(1, -1),
               ev_conv_ln_b[0].reshape(1, -1), bf(ev_w_out[0][:CONV_DIM]), bf(ev_w_out[0][CONV_DIM:]))
    xs = _even_out(xs, m0, u, att, *ev_args, tm=TM, row0=0, n_tiles=N_LAT_TILES,
                   tiles_per_seq=SEQ // TM, mod_row=lambda i: i // (SEQ // TM))
    xs = _even_out(xs, m0, u, att, *ev_args, tm=CTX, row0=T_LAT, n_tiles=BATCH,
                   tiles_per_seq=1, mod_row=lambda i: BATCH)
    xs = _ffn(xs, m0, 2, ffn2_norm[0], bf(ffn2_w_gu[0]), bf(ffn2_w_d[0]), N_ALL_TILES)

    m1 = mods[1]
    xs = _ffn(xs, m1, 0, ffn1_norm[1], bf(ffn1_w_gu[1]), bf(ffn1_w_d[1]), N_ALL_TILES)
    w_in = od_w_in[0]
    perm = np.concatenate([np.arange(0, 4), np.arange(8, 12), np.arange(4, 8), np.arange(12, 16)])
    w_gate_t = w_in[:, 4 * ML_DIM:].T[perm]
    b_gate_t = od_b_gate[0][perm].reshape(N_GATE, 1)
    qt, km, vt, ga, gamax, gb = _odd_in(xs, m1, mix_norm[1], bf(w_in[:, :ML_DIM].T),
                                        bf(w_in[:, ML_DIM:2 * ML_DIM]),
                                        bf(w_in[:, 2 * ML_DIM:3 * ML_DIM].T), w_gate_t, b_gate_t)
    hf, hb = _mlstm(qt, km, vt, ga, gamax, gb)
    xl = _odd_out(xs, m1, mix_norm[1], hf, hb, bf(w_in[:, 3 * ML_DIM:4 * ML_DIM]), od_norm_g[0],
                  bf(od_w_out[0]))
    out = _ffn(xl, m1, 2, ffn2_norm[1], bf(ffn2_w_gu[1]), bf(ffn2_w_d[1]), N_LAT_TILES,
               final_g=final_norm)
    return out.reshape(BATCH, SEQ, D)
```

```python
import functools

import jax
import jax.numpy as jnp
import numpy as np
from jax import lax
from jax.experimental import pallas as pl
from jax.experimental.pallas import tpu as pltpu

D = 1024
BATCH = 4
SEQ = 4096
DEPTH = 2
GRID_W = 64
CTX = 256
N_MOD = 9
D_FF = 2816
EPS = 1e-6
CONV_DIM = 512
CONV_W = 31
CONV_PAD = 15
HEAD_DIM = 64
ATT_HEADS = 8
KV_HEADS = 2
ATT_DIM = 512
KV_DIM = 128
WINDOW = 128
BLK = 128
ROPE_BASE = 10000.0
ROPE_FREQS = 16
Q_OFF = 2 * CONV_DIM
K_OFF = Q_OFF + ATT_DIM
V_OFF = K_OFF + KV_DIM
ML_HEADS = 4
ML_DH = 256
ML_DIM = 1024
ML_CHUNK = 128

T_LAT = BATCH * SEQ
T_CTX = BATCH * CTX
T_ALL = T_LAT + T_CTX
TM = 1024
N_LAT_TILES = T_LAT // TM
N_ALL_TILES = T_ALL // TM
MOD_ROWS = 8
HALO = 16
NEG = -1e30
VMEM_LIMIT = 56 * 1024 * 1024

F32 = jnp.float32
BF16 = jnp.bfloat16


def _sigmoid(x):
    return 1.0 / (1.0 + jnp.exp(-x))


def _silu(x):
    return x * _sigmoid(x)


def _dot(a, b, precision=None):
    return jnp.dot(a, b, preferred_element_type=F32, precision=precision)


def _dot_nt(a, b, precision=None):
    return lax.dot_general(a, b, (((1,), (1,)), ((), ())),
                           preferred_element_type=F32, precision=precision)


def _dot_tn(a, b):
    return lax.dot_general(a, b, (((0,), (0,)), ((), ())), preferred_element_type=F32)


def _rms_mod(x, g, shift, scale):
    y = x * lax.rsqrt(jnp.mean(x * x, axis=-1, keepdims=True) + EPS)
    return (y * g) * (1.0 + scale) + shift


def _resident(shape):
    nd = len(shape)
    return pl.BlockSpec(shape, lambda *_: (0,) * nd, pipeline_mode=pl.Buffered(1))


def _params(n_axes=1):
    return pltpu.CompilerParams(dimension_semantics=("arbitrary",) * n_axes,
                                vmem_limit_bytes=VMEM_LIMIT)


def _mod_kernel(c_ref, w_ref, b_ref, o_ref):
    s = _silu(c_ref[...]).astype(BF16)
    o_ref[...] = _dot(s, w_ref[...].astype(BF16)) + b_ref[...]


def _ada_mods(cond, mod_w, mod_b):
    tn = 1024
    n = N_MOD * D
    return pl.pallas_call(
        _mod_kernel,
        grid=(DEPTH, n // tn),
        in_specs=[
            pl.BlockSpec((MOD_ROWS, D), lambda l, j: (0, 0)),
            pl.BlockSpec((None, D, tn), lambda l, j: (l, 0, j)),
            pl.BlockSpec((None, 1, tn), lambda l, j: (l, 0, j)),
        ],
        out_specs=pl.BlockSpec((None, MOD_ROWS, tn), lambda l, j: (l, 0, j)),
        out_shape=jax.ShapeDtypeStruct((DEPTH, MOD_ROWS, n), F32),
        compiler_params=_params(2),
        name="ada_mods",
    )(cond, mod_w, mod_b.reshape(DEPTH, 1, n))


def _mod_spec(k):
    return pl.BlockSpec((None, 1, 3 * D), lambda i: (i // (SEQ // TM), 0, k))


FF_CHUNKS = tuple((c, min(c + 512, D_FF)) for c in range(0, D_FF, 512))


STAGE = 256


def _stage_weights(src_ref, dst_ref, stage_ref, sem_ref, chunks):
    def copy(n):
        return pltpu.make_async_copy(src_ref.at[chunks[n][0]], stage_ref.at[n % 2], sem_ref.at[n % 2])

    copy(0).start()
    for n in range(len(chunks)):
        if n + 1 < len(chunks):
            copy(n + 1).start()
        copy(n).wait()
        dst_ref[chunks[n][1]] = stage_ref[n % 2].astype(BF16)


def _ffn_kernel(x_ref, *rest, layer, split, final):
    if split:
        xc_ref, *rest = rest
    m_ref, g_ref, wgu_hbm, wd_hbm, *rest = rest
    if final:
        fn_ref, *rest = rest
    o_ref, wgu_ref, wd_ref, sgu_ref, sd_ref, sem_ref = rest

    @pl.when(pl.program_id(0) == 0)
    def _():
        _stage_weights(wgu_hbm, wgu_ref, sgu_ref, sem_ref.at[0],
                       [((layer, slice(None), pl.ds(c, STAGE)), (slice(None), pl.ds(c, STAGE)))
                        for c in range(0, 2 * D_FF, STAGE)])
        _stage_weights(wd_hbm, wd_ref, sd_ref, sem_ref.at[1],
                       [((layer, pl.ds(r, STAGE), slice(None)), (pl.ds(r, STAGE), slice(None)))
                        for r in range(0, D_FF, STAGE)])

    x = x_ref[...]
    if split:
        x = jnp.where(pl.program_id(0) < N_LAT_TILES, x, xc_ref[...])
    shift = m_ref[:, 0:D]
    scale = m_ref[:, D:2 * D]
    gate = m_ref[:, 2 * D:3 * D]
    h = _rms_mod(x, g_ref[...], shift, scale).astype(BF16)
    acc = None
    for c0, c1 in FF_CHUNKS:
        hg = _dot(h, wgu_ref[:, c0:c1])
        hu = _dot(h, wgu_ref[:, D_FF + c0:D_FF + c1])
        a = (_silu(hg) * hu).astype(BF16)
        p = _dot(a, wd_ref[c0:c1, :])
        acc = p if acc is None else acc + p
    y = x + (0.5 * gate) * acc
    if final:
        y = (y * lax.rsqrt(jnp.mean(y * y, axis=-1, keepdims=True) + EPS)) * fn_ref[...]
    o_ref[...] = y


def _ffn(x, mods, k, norm_g, w_gu, w_d, layer, n_tiles, final_g=None, x_ctx=None):
    final = final_g is not None
    split = x_ctx is not None
    if split:
        in_specs = [pl.BlockSpec((TM, D), lambda i: (jnp.minimum(i, N_LAT_TILES - 1), 0)),
                    pl.BlockSpec((TM, D), lambda i: (0, 0))]
        args = [x, x_ctx]
    else:
        in_specs = [pl.BlockSpec((TM, D), lambda i: (i, 0))]
        args = [x]
    in_specs += [
        _mod_spec(k),
        _resident((1, D)),
        pl.BlockSpec(memory_space=pl.ANY),
        pl.BlockSpec(memory_space=pl.ANY),
    ]
    args += [mods, norm_g.reshape(1, D), w_gu, w_d]
    if final:
        in_specs.append(_resident((1, D)))
        args.append(final_g.reshape(1, D))
    return pl.pallas_call(
        functools.partial(_ffn_kernel, layer=layer, split=split, final=final),
        grid=(n_tiles,),
        in_specs=in_specs,
        out_specs=pl.BlockSpec((TM, D), lambda i: (i, 0)),
        out_shape=jax.ShapeDtypeStruct((n_tiles * TM, D), F32),
        scratch_shapes=[pltpu.VMEM((D, 2 * D_FF), BF16),
                        pltpu.VMEM((D_FF, D), BF16),
                        pltpu.VMEM((2, D, STAGE), F32),
                        pltpu.VMEM((2, STAGE, D), F32),
                        pltpu.SemaphoreType.DMA((2, 2))],
        compiler_params=_params(1),
        name="ffn_final" if final else ("ffn_first" if split else "ffn"),
    )(*args)


def _even_in_kernel(x_ref, m_ref, g_ref, wvg_ref, wqt_ref, wk_ref, wvt_ref,
                    c_ref, s1_ref, s2_ref, ct_ref, s1t_ref, s2t_ref,
                    u_ref, qt_ref, k_ref, vt_ref):
    x = x_ref[...]
    h = _rms_mod(x, g_ref[...], m_ref[:, 0:D], m_ref[:, D:2 * D]).astype(BF16)
    vg = _dot(h, wvg_ref[...])
    u_ref[...] = vg[:, 0:CONV_DIM] * _sigmoid(vg[:, CONV_DIM:Q_OFF])
    qf = _dot_nt(wqt_ref[...], h)
    ct, s1t, s2t = ct_ref[...], s1t_ref[...], s2t_ref[...]
    for hd in range(ATT_HEADS):
        xh = qf[hd * HEAD_DIM:(hd + 1) * HEAD_DIM]
        up = jnp.concatenate([xh[ROPE_FREQS:], xh[:ROPE_FREQS]], axis=0)
        dn = jnp.concatenate([xh[HEAD_DIM - ROPE_FREQS:], xh[:HEAD_DIM - ROPE_FREQS]], axis=0)
        r = xh * ct + up * s1t + dn * s2t
        qt_ref[hd * HEAD_DIM:(hd + 1) * HEAD_DIM, :] = (r * (HEAD_DIM ** -0.5)).astype(BF16)
    kf = _dot(h, wk_ref[...])
    kr = kf * c_ref[...] + pltpu.roll(kf, 128 - ROPE_FREQS, 1) * s1_ref[...] + pltpu.roll(kf, ROPE_FREQS, 1) * s2_ref[...]
    k_ref[...] = kr.astype(BF16)
    vt_ref[...] = _dot_nt(wvt_ref[...], h).astype(BF16)


def _even_in(x, mods, norm_g, w_vg, w_q_t, w_k, w_v_t, rope, rope_t):
    pos = lambda i: jnp.where(i < N_LAT_TILES, i % (SEQ // TM), SEQ // TM)
    rope_spec = lambda: pl.BlockSpec((TM, KV_DIM), lambda i: (pos(i), 0))
    rope_t_spec = lambda: pl.BlockSpec((HEAD_DIM, TM), lambda i: (0, pos(i)))
    row = lambda i: (i, 0)
    col = lambda i: (0, i)
    return pl.pallas_call(
        _even_in_kernel,
        grid=(N_ALL_TILES,),
        in_specs=[
            pl.BlockSpec((TM, D), row),
            _mod_spec(1),
            _resident((1, D)),
            _resident((D, Q_OFF)),
            _resident((ATT_DIM, D)),
            _resident((D, KV_DIM)),
            _resident((KV_DIM, D)),
            rope_spec(), rope_spec(), rope_spec(),
            rope_t_spec(), rope_t_spec(), rope_t_spec(),
        ],
        out_specs=[
            pl.BlockSpec((TM, CONV_DIM), row),
            pl.BlockSpec((ATT_DIM, TM), col),
            pl.BlockSpec((TM, KV_DIM), row),
            pl.BlockSpec((KV_DIM, TM), col),
        ],
        out_shape=[
            jax.ShapeDtypeStruct((T_ALL, CONV_DIM), F32),
            jax.ShapeDtypeStruct((ATT_DIM, T_ALL), BF16),
            jax.ShapeDtypeStruct((T_ALL, KV_DIM), BF16),
            jax.ShapeDtypeStruct((KV_DIM, T_ALL), BF16),
        ],
        compiler_params=_params(1),
        name="even_in",
    )(x, mods, norm_g.reshape(1, D), w_vg, w_q_t, w_k, w_v_t, *rope, *rope_t)


def _rope_tables():
    rows = SEQ // GRID_W
    row = jnp.repeat(jnp.arange(rows, dtype=F32), GRID_W)
    col = jnp.tile(jnp.arange(GRID_W, dtype=F32), rows)
    inv = ROPE_BASE ** (-jnp.arange(ROPE_FREQS, dtype=F32) / ROPE_FREQS)
    ang = jnp.concatenate([row[:, None] * inv, col[:, None] * inv], axis=-1)
    cos, sin = jnp.cos(ang), jnp.sin(ang)
    d = np.arange(128) % HEAD_DIM
    src = (d // 32) * ROPE_FREQS + d % ROPE_FREQS
    first = jnp.asarray(((d % 32) // ROPE_FREQS) == 0)
    c = cos[:, src]
    s = sin[:, src]
    s1 = jnp.where(first, -s, 0.0)
    s2 = jnp.where(first, 0.0, s)
    pad = lambda t, v: jnp.concatenate([t, jnp.full((TM, 128), v, F32)], axis=0)
    tok = (pad(c, 1.0), pad(s1, 0.0), pad(s2, 0.0))
    return tok, tuple(t[:, :HEAD_DIM].T for t in tok)


NB = SEQ // BLK
NCB = CTX // BLK
GROUP = ATT_HEADS // KV_HEADS


def _attn_kernel(qt_ref, kp_ref, kc_ref, kn_ref, kx_ref, vp_ref, vc_ref, vn_ref, vx_ref,
                 sink_ref, o_ref):
    i = pl.program_id(1)
    kk = lax.broadcasted_iota(jnp.int32, (3 * BLK, BLK), 0)
    qq = lax.broadcasted_iota(jnp.int32, (3 * BLK, BLK), 1)
    rel = kk - BLK - qq
    kpos = (i - 1) * BLK + kk
    ok = (jnp.abs(rel) <= WINDOW) & (kpos >= 0) & (kpos < SEQ) & (i < NB)
    bias1 = jnp.where(ok, 0.0, NEG).astype(F32)
    bias = jnp.concatenate([bias1] * GROUP, axis=1)
    keys = jnp.concatenate([kp_ref[...], kc_ref[...], kn_ref[...], kx_ref[...]], axis=0)
    vt = jnp.concatenate([vp_ref[...], vc_ref[...], vn_ref[...], vx_ref[...]], axis=1)
    qt = qt_ref[...]
    zero = jnp.zeros((HEAD_DIM, BLK), BF16)
    outs = []
    for g in range(KV_HEADS):
        cols = []
        for hh in range(GROUP):
            hd = g * GROUP + hh
            qh = qt[hd * HEAD_DIM:(hd + 1) * HEAD_DIM]
            cols.append(jnp.concatenate([qh, zero] if g == 0 else [zero, qh], axis=0))
        s = _dot(keys, jnp.concatenate(cols, axis=1))
        s_b = s[0:3 * BLK] + bias
        s_x = s[3 * BLK:]
        sink = sink_ref[:, g * GROUP * BLK:(g + 1) * GROUP * BLK]
        m = jnp.maximum(jnp.maximum(jnp.max(s_b, axis=0, keepdims=True),
                                    jnp.max(s_x, axis=0, keepdims=True)), sink)
        p_b = jnp.exp(s_b - m)
        p_x = jnp.exp(s_x - m)
        den = (jnp.sum(p_b, axis=0, keepdims=True) + jnp.sum(p_x, axis=0, keepdims=True)
               + jnp.exp(sink - m))
        p = jnp.concatenate([p_b, p_x], axis=0).astype(BF16)
        ot = _dot(vt[g * HEAD_DIM:(g + 1) * HEAD_DIM], p) * (1.0 / den)
        outs += [ot[:, hh * BLK:(hh + 1) * BLK] for hh in range(GROUP)]
    o_ref[...] = jnp.concatenate(outs, axis=0).T.astype(BF16)


def _attention(qt, k, vt, sink):
    lat_blocks = T_LAT // BLK
    q_blk = lambda b, i: jnp.where(i < NB, b * NB + i, lat_blocks + b * NCB + (i - NB))
    band = lambda off: (lambda b, i: b * NB + jnp.clip(i + off, 0, NB - 1))
    ctx_blk = lambda b, i: T_LAT // CTX + b
    k_spec = lambda n, f: pl.BlockSpec((n, KV_DIM), lambda b, i: (f(b, i), 0))
    v_spec = lambda n, f: pl.BlockSpec((KV_DIM, n), lambda b, i: (0, f(b, i)))
    return pl.pallas_call(
        _attn_kernel,
        grid=(BATCH, NB + NCB),
        in_specs=[
            pl.BlockSpec((ATT_DIM, BLK), lambda b, i: (0, q_blk(b, i))),
            k_spec(BLK, band(-1)), k_spec(BLK, band(0)), k_spec(BLK, band(1)), k_spec(CTX, ctx_blk),
            v_spec(BLK, band(-1)), v_spec(BLK, band(0)), v_spec(BLK, band(1)), v_spec(CTX, ctx_blk),
            pl.BlockSpec((1, ATT_HEADS * BLK), lambda b, i: (0, 0)),
        ],
        out_specs=pl.BlockSpec((BLK, ATT_DIM), lambda b, i: (q_blk(b, i), 0)),
        out_shape=jax.ShapeDtypeStruct((T_ALL, ATT_DIM), BF16),
        compiler_params=_params(2),
        name="window_attn",
    )(qt, k, k, k, k, vt, vt, vt, vt, sink)


CONV_S = 4
CONV_ROWS = 8 * CONV_S
N_SLAB = CONV_DIM // 128


def _even_out_kernel(x_ref, m_ref, u_ref, up_ref, un_ref, att_ref, cw_ref, cb_ref,
                     lg_ref, lb_ref, wa_ref, wb_ref, o_ref, ext_ref, a_ref, cva_ref, cvb_ref, *, tm, tiles_per_seq):
    t = pl.program_id(0)
    first = (t % tiles_per_seq) == 0
    last = (t % tiles_per_seq) == tiles_per_seq - 1
    slabs = [slice(s * 128, (s + 1) * 128) for s in range(N_SLAB)]
    for s, ls in enumerate(slabs):
        ext_ref[s, 0:HALO, :] = jnp.where(first, 0.0, up_ref[:, ls])
        ext_ref[s, HALO:HALO + tm, :] = u_ref[:, ls]
        ext_ref[s, HALO + tm:, :] = jnp.where(last, 0.0, un_ref[:, ls])

    def conv_block(blk, cv_ref):
        base = blk * CONV_ROWS
        for s, ls in enumerate(slabs):
            acc = [jnp.broadcast_to(cb_ref[:, ls], (8, 128)) for _ in range(CONV_S)]
            for o in range(CONV_W + CONV_S - 1):
                v = ext_ref[s, pl.ds(base + (HALO - CONV_PAD) + o, 8, stride=CONV_S), :]
                for j in range(CONV_S):
                    k = o - j
                    if 0 <= k < CONV_W:
                        acc[j] = acc[j] + cw_ref[k:k + 1, ls] * v
            for j in range(CONV_S):
                cv_ref[s * CONV_S + j] = acc[j]

    def norm_block(blk, cv_ref):
        base = blk * CONV_ROWS
        for j in range(CONV_S):
            row = [cv_ref[s * CONV_S + j] for s in range(N_SLAB)]
            mu = jnp.sum(sum(row), axis=-1, keepdims=True) * (1.0 / CONV_DIM)
            cen = [r - mu for r in row]
            var = jnp.sum(sum(c * c for c in cen), axis=-1, keepdims=True) * (1.0 / CONV_DIM)
            rs = lax.rsqrt(var + EPS)
            for s, ls in enumerate(slabs):
                y = cen[s] * rs * lg_ref[:, ls] + lb_ref[:, ls]
                a_ref[s, pl.ds(base + j, 8, stride=CONV_S), :] = _silu(y)

    n_blk = tm // CONV_ROWS
    conv_block(0, cva_ref)

    def body(i, carry):
        conv_block(2 * i + 1, cvb_ref)
        norm_block(2 * i, cva_ref)
        conv_block(jnp.minimum(2 * i + 2, n_blk - 1), cva_ref)
        norm_block(2 * i + 1, cvb_ref)
        return carry

    lax.fori_loop(0, n_blk // 2, body, 0)
    a = jnp.concatenate([a_ref[s] for s in range(N_SLAB)], axis=1).astype(BF16)
    y = _dot(a, wa_ref[...]) + _dot(att_ref[...], wb_ref[...])
    o_ref[...] = x_ref[...] + m_ref[:, 2 * D:3 * D] * y


def _even_out(x, mods, u, att, conv_w, conv_b, ln_g, ln_b, wa, wb, *, tm, row0, n_tiles,
              tiles_per_seq, mod_row):
    blk0 = row0 // tm
    hb = tm // HALO
    n_halo = T_ALL // HALO
    row_map = lambda i: (blk0 + i, 0)
    in_specs = [
        pl.BlockSpec((tm, D), row_map),
        pl.BlockSpec((None, 1, 3 * D), lambda i: (mod_row(i), 0, 1)),
        pl.BlockSpec((tm, CONV_DIM), row_map),
        pl.BlockSpec((HALO, CONV_DIM), lambda i: (jnp.maximum((blk0 + i) * hb - 1, 0), 0)),
        pl.BlockSpec((HALO, CONV_DIM), lambda i: (jnp.minimum((blk0 + i + 1) * hb, n_halo - 1), 0)),
        pl.BlockSpec((tm, ATT_DIM), row_map),
        _resident((32, CONV_DIM)),
        _resident((1, CONV_DIM)),
        _resident((1, CONV_DIM)),
        _resident((1, CONV_DIM)),
        _resident((CONV_DIM, D)),
        _resident((ATT_DIM, D)),
    ]
    return pl.pallas_call(
        functools.partial(_even_out_kernel, tm=tm, tiles_per_seq=tiles_per_seq),
        grid=(n_tiles,),
        in_specs=in_specs,
        out_specs=pl.BlockSpec((tm, D), row_map),
        out_shape=jax.ShapeDtypeStruct((T_ALL, D), F32),
        scratch_shapes=[pltpu.VMEM((N_SLAB, tm + 2 * HALO, 128), F32),
                        pltpu.VMEM((N_SLAB, tm, 128), F32),
                        pltpu.VMEM((N_SLAB * CONV_S, 8, 128), F32),
                        pltpu.VMEM((N_SLAB * CONV_S, 8, 128), F32)],
        input_output_aliases={0: 0},
        compiler_params=_params(1),
        name="even_out" if tiles_per_seq > 1 else "even_out_ctx",
    )(x, mods, u, u, u, att, conv_w, conv_b, ln_g, ln_b, wa, wb)


N_GATE = 4 * ML_HEADS
N_CHAIN = 2 * ML_HEADS
L = ML_CHUNK


def _log_sigmoid(x):
    return jnp.minimum(x, 0.0) - jnp.log(1.0 + jnp.exp(-jnp.abs(x)))


def _lane_scan(x, lane, fwd_rows, combine, fill):
    sh = 1
    while sh < L:
        pre = jnp.where(lane >= sh, pltpu.roll(x, sh, 1), fill)
        suf = jnp.where(lane < L - sh, pltpu.roll(x, L - sh, 1), fill)
        x = combine(x, jnp.where(fwd_rows, pre, suf))
        sh *= 2
    return x


def _odd_in_kernel(x_ref, m_ref, g_ref, wqt_ref, wk_ref, wvt_ref, wgt_ref, bgt_ref,
                   qt_ref, k_ref, vt_ref, a_ref, amax_ref, b_ref):
    x = x_ref[...]
    hf = _rms_mod(x, g_ref[...], m_ref[:, 0:D], m_ref[:, D:2 * D])
    h = hf.astype(BF16)
    g = _dot_nt(wgt_ref[...], hf, precision=lax.Precision.HIGHEST) + bgt_ref[...]
    qt_ref[...] = _dot_nt(wqt_ref[...], h).astype(BF16)
    k_ref[...] = (_dot(h, wk_ref[...]) * (ML_DH ** -0.5)).astype(BF16)
    vt_ref[...] = _dot_nt(wvt_ref[...], h).astype(BF16)
    li = g[0:N_CHAIN]
    lf = _log_sigmoid(g[N_CHAIN:N_GATE])
    fwd_rows = lax.broadcasted_iota(jnp.int32, (N_CHAIN, L), 0) < ML_HEADS
    lane = lax.broadcasted_iota(jnp.int32, (N_CHAIN, L), 1)
    for ch in range(TM // L):
        cs = slice(ch * L, (ch + 1) * L)
        b = _lane_scan(lf[:, cs], lane, fwd_rows, jnp.add, 0.0)
        a = li[:, cs] - b
        a_ref[:, cs] = a
        amax_ref[:, cs] = _lane_scan(a, lane, fwd_rows, jnp.maximum, NEG)
        b_ref[:, cs] = b


def _odd_in(x, mods, norm_g, w_q_t, w_k, w_v_t, w_gate_t, b_gate_t):
    row = lambda i: (i, 0)
    col = lambda i: (0, i)
    return pl.pallas_call(
        _odd_in_kernel,
        grid=(N_ALL_TILES,),
        in_specs=[
            pl.BlockSpec((TM, D), row),
            _mod_spec(1),
            _resident((1, D)),
            _resident((ML_DIM, D)),
            _resident((D, ML_DIM)),
            _resident((ML_DIM, D)),
            _resident((N_GATE, D)),
            _resident((N_GATE, 1)),
        ],
        out_specs=[
            pl.BlockSpec((ML_DIM, TM), col),
            pl.BlockSpec((TM, ML_DIM), row),
            pl.BlockSpec((ML_DIM, TM), col),
            pl.BlockSpec((N_CHAIN, TM), col),
            pl.BlockSpec((N_CHAIN, TM), col),
            pl.BlockSpec((N_CHAIN, TM), col),
        ],
        out_shape=[
            jax.ShapeDtypeStruct((ML_DIM, T_ALL), BF16),
            jax.ShapeDtypeStruct((T_ALL, ML_DIM), BF16),
            jax.ShapeDtypeStruct((ML_DIM, T_ALL), BF16),
            jax.ShapeDtypeStruct((N_CHAIN, T_ALL), F32),
            jax.ShapeDtypeStruct((N_CHAIN, T_ALL), F32),
            jax.ShapeDtypeStruct((N_CHAIN, T_ALL), F32),
        ],
        compiler_params=_params(1),
        name="odd_in",
    )(x, mods, norm_g.reshape(1, D), w_q_t, w_k, w_v_t, w_gate_t, b_gate_t)


N_CTX_CHUNKS = CTX // ML_CHUNK
N_LAT_CHUNKS = SEQ // ML_CHUNK
N_STEPS = N_CTX_CHUNKS + N_LAT_CHUNKS
N_AUG = 16


def _mlstm_kernel(qtf_ref, kf_ref, vtf_ref, qtb_ref, kb_ref, vtb_ref,
                  af_ref, amaxf_ref, bf_ref, ab_ref, amaxb_ref, bb_ref, hf_ref, hb_ref, c_ref, m_ref):
    t = pl.program_id(1)

    @pl.when(t == 0)
    def _():
        c_ref[...] = jnp.zeros_like(c_ref)
        m_ref[...] = jnp.zeros_like(m_ref)

    fwd_rows = lax.broadcasted_iota(jnp.int32, (N_CHAIN, L), 0) < ML_HEADS
    a = jnp.where(fwd_rows, af_ref[...], ab_ref[...])
    amax = jnp.where(fwd_rows, amaxf_ref[...], amaxb_ref[...])
    b = jnp.where(fwd_rows, bf_ref[...], bb_ref[...])
    m_old = m_ref[:, 0:1]
    big = jnp.maximum(m_old, jnp.max(amax, axis=1, keepdims=True))
    decay = jnp.exp(m_old - big)
    e = jnp.exp(a - big)
    mm = jnp.maximum(amax, m_old)
    w_inter = jnp.exp(m_old - mm)
    floor = jnp.exp(-(b + mm))
    m_ref[...] = jnp.broadcast_to(jnp.min(b, axis=1, keepdims=True) + big, (N_CHAIN, 128))
    kq = lax.broadcasted_iota(jnp.int32, (L, L), 0)
    qq = lax.broadcasted_iota(jnp.int32, (L, L), 1)
    a_col = _dot_nt((kq == qq).astype(F32), a, precision=lax.Precision.HIGHEST)

    for d, (qt_ref, k_ref, vt_ref, h_ref) in enumerate(
            ((qtf_ref, kf_ref, vtf_ref, hf_ref), (qtb_ref, kb_ref, vtb_ref, hb_ref))):
        visible = (kq >= qq) if d == 1 else (kq <= qq)
        for hd in range(ML_HEADS):
            c = d * ML_HEADS + hd
            cs = slice(hd * ML_DH, (hd + 1) * ML_DH)
            qt = qt_ref[cs, :]
            k = k_ref[:, cs]
            vt = vt_ref[cs, :]
            p = jnp.where(visible, jnp.exp(a_col[:, c:c + 1] - mm[c:c + 1, :]), 0.0)
            st = _dot(k, qt) * p
            c_old = c_ref[c]
            cq = _dot(c_old.astype(BF16), qt)
            wi = w_inter[c:c + 1, :]
            num = _dot(vt, st.astype(BF16)) + wi * cq[0:ML_DH]
            den = jnp.sum(st, axis=0, keepdims=True) + wi * cq[ML_DH:ML_DH + 1]
            ht = num * (1.0 / jnp.maximum(jnp.abs(den), floor[c:c + 1, :]))
            h_ref[:, cs] = ht.T
            er = e[c:c + 1, :]
            vte = jnp.concatenate([vt.astype(F32) * er, jnp.broadcast_to(er, (N_AUG, L))], axis=0)
            c_ref[c] = decay[c:c + 1, :] * c_old + _dot(vte.astype(BF16), k)


def _mlstm(qt, k, vt, a, amax, b):
    lat_chunks = T_LAT // L

    def fwd_in(b, t):
        return jnp.where(t < N_CTX_CHUNKS, lat_chunks + b * N_CTX_CHUNKS + t,
                         b * N_LAT_CHUNKS + (t - N_CTX_CHUNKS))

    def bwd_in(b, t):
        return jnp.where(t < N_CTX_CHUNKS, lat_chunks + b * N_CTX_CHUNKS + (N_CTX_CHUNKS - 1 - t),
                         b * N_LAT_CHUNKS + (N_STEPS - 1 - t))

    def fwd_out(b, t):
        return b * N_LAT_CHUNKS + jnp.maximum(t - N_CTX_CHUNKS, 0)

    def bwd_out(b, t):
        return b * N_LAT_CHUNKS + (N_STEPS - 1 - jnp.maximum(t, N_CTX_CHUNKS))

    rows = lambda f: pl.BlockSpec((L, ML_DIM), lambda b, t: (f(b, t), 0))
    cols = lambda f: pl.BlockSpec((ML_DIM, L), lambda b, t: (0, f(b, t)))
    gate = lambda f: pl.BlockSpec((N_CHAIN, L), lambda b, t: (0, f(b, t)))
    return pl.pallas_call(
        _mlstm_kernel,
        grid=(BATCH, N_STEPS),
        in_specs=[cols(fwd_in), rows(fwd_in), cols(fwd_in),
                  cols(bwd_in), rows(bwd_in), cols(bwd_in),
                  gate(fwd_in), gate(fwd_in), gate(fwd_in), gate(bwd_in), gate(bwd_in), gate(bwd_in)],
        out_specs=[rows(fwd_out), rows(bwd_out)],
        out_shape=[jax.ShapeDtypeStruct((T_LAT, ML_DIM), F32),
                   jax.ShapeDtypeStruct((T_LAT, ML_DIM), F32)],
        scratch_shapes=[pltpu.VMEM((N_CHAIN, ML_DH + N_AUG, ML_DH), F32),
                        pltpu.VMEM((N_CHAIN, 128), F32)],
        compiler_params=_params(2),
        name="mlstm",
    )(qt, k, vt, qt, k, vt, a, amax, b, a, amax, b)


def _odd_out_kernel(x_ref, m_ref, g_ref, hf_ref, hb_ref, wo_ref, ng_ref, wout_ref, o_ref):
    x = x_ref[...]
    h = _rms_mod(x, g_ref[...], m_ref[:, 0:D], m_ref[:, D:2 * D]).astype(BF16)
    o = _sigmoid(_dot(h, wo_ref[...]))
    hs = hf_ref[...] + hb_ref[...]
    parts = []
    for hd in range(ML_HEADS):
        p = hs[:, hd * ML_DH:(hd + 1) * ML_DH]
        parts.append(p * lax.rsqrt(jnp.mean(p * p, axis=-1, keepdims=True) + EPS))
    hn = jnp.concatenate(parts, axis=-1) * ng_ref[...]
    y = _dot((o * hn).astype(BF16), wout_ref[...])
    o_ref[...] = x + m_ref[:, 2 * D:3 * D] * y


def _odd_out(x, mods, norm_g, hf, hb, w_o, head_g, w_out):
    row = lambda i: (i, 0)
    return pl.pallas_call(
        _odd_out_kernel,
        grid=(N_LAT_TILES,),
        in_specs=[
            pl.BlockSpec((TM, D), row),
            _mod_spec(1),
            _resident((1, D)),
            pl.BlockSpec((TM, ML_DIM), row),
            pl.BlockSpec((TM, ML_DIM), row),
            _resident((D, ML_DIM)),
            _resident((1, ML_DIM)),
            _resident((ML_DIM, D)),
        ],
        out_specs=pl.BlockSpec((TM, D), row),
        out_shape=jax.ShapeDtypeStruct((T_LAT, D), F32),
        compiler_params=_params(1),
        name="odd_out",
    )(x, mods, norm_g.reshape(1, D), hf, hb, w_o, head_g.reshape(1, ML_DIM), w_out)


def kernel(x, c, ctx, c_ctx, mod_w, mod_b, ffn1_norm, ffn1_w_gu, ffn1_w_d, mix_norm, ffn2_norm,
           ffn2_w_gu, ffn2_w_d, ev_w_in, ev_conv_w, ev_conv_b, ev_conv_ln_g, ev_conv_ln_b, ev_sink,
           ev_w_out, od_w_in, od_b_gate, od_norm_g, od_w_out, final_norm):
    assert DEPTH == 2 and x.shape == (BATCH, SEQ, D) and ctx.shape == (BATCH, CTX, D)
    cond = jnp.zeros((MOD_ROWS, D), F32).at[:BATCH].set(c).at[BATCH].set(c_ctx)
    mods = _ada_mods(cond, mod_w, mod_b).reshape(DEPTH, MOD_ROWS, 1, N_MOD * D)
    bf = lambda w: w.astype(BF16)

    m0 = mods[0]
    xs = _ffn(x.reshape(T_LAT, D), m0, 0, ffn1_norm[0], ffn1_w_gu, ffn1_w_d, 0, N_ALL_TILES,
              x_ctx=ctx.reshape(T_CTX, D))
    w_in = ev_w_in[0]
    rope, rope_t = _rope_tables()
    u, qt, kk, vt = _even_in(xs, m0, mix_norm[0], bf(w_in[:, :Q_OFF]), bf(w_in[:, Q_OFF:K_OFF].T),
                             bf(w_in[:, K_OFF:V_OFF]), bf(w_in[:, V_OFF:].T), rope, rope_t)
    sink = jnp.repeat(ev_sink[0].astype(F32), BLK).reshape(1, ATT_HEADS * BLK)
    att = _attention(qt, kk, vt, sink)
    conv_w = jnp.concatenate([ev_conv_w[0], jnp.zeros((1, CONV_DIM), F32)], axis=0)
    ev_args = (conv_w, ev_conv_b[0].reshape(1, -1), ev_conv_ln_g[0].reshape(1, -1),
               ev_conv_ln_b[0].reshape(1, -1), bf(ev_w_out[0][:CONV_DIM]), bf(ev_w_out[0][CONV_DIM:]))
    xs = _even_out(xs, m0, u, att, *ev_args, tm=TM, row0=0, n_tiles=N_LAT_TILES,
                   tiles_per_seq=SEQ // TM, mod_row=lambda i: i // (SEQ // TM))
    xs = _even_out(xs, m0, u, att, *ev_args, tm=CTX, row0=T_LAT, n_tiles=BATCH,
                   tiles_per_seq=1, mod_row=lambda i: BATCH)
    xs = _ffn(xs, m0, 2, ffn2_norm[0], ffn2_w_gu, ffn2_w_d, 0, N_ALL_TILES)

    m1 = mods[1]
    xs = _ffn(xs, m1, 0, ffn1_norm[1], ffn1_w_gu, ffn1_w_d, 1, N_ALL_TILES)
    w_in = od_w_in[0]
    perm = np.concatenate([np.arange(0, 4), np.arange(8, 12), np.arange(4, 8), np.arange(12, 16)])
    w_gate_t = w_in[:, 4 * ML_DIM:].T[perm]
    b_gate_t = od_b_gate[0][perm].reshape(N_GATE, 1)
    qt, km, vt, ga, gamax, gb = _odd_in(xs, m1, mix_norm[1], bf(w_in[:, :ML_DIM].T),
                                        bf(w_in[:, ML_DIM:2 * ML_DIM]),
                                        bf(w_in[:, 2 * ML_DIM:3 * ML_DIM].T), w_gate_t, b_gate_t)
    hf, hb = _mlstm(qt, km, vt, ga, gamax, gb)
    xl = _odd_out(xs, m1, mix_norm[1], hf, hb, bf(w_in[:, 3 * ML_DIM:4 * ML_DIM]), od_norm_g[0],
                  bf(od_w_out[0]))
    out = _ffn(xl, m1, 2, ffn2_norm[1], ffn2_w_gu, ffn2_w_d, 1, N_LAT_TILES,
               final_g=final_norm)
    return out.reshape(BATCH, SEQ, D)
```

```python
import functools

import jax
import jax.numpy as jnp
import numpy as np
from jax import lax
from jax.experimental import pallas as pl
from jax.experimental.pallas import tpu as pltpu

D = 1024
BATCH = 4
SEQ = 4096
DEPTH = 2
GRID_W = 64
CTX = 256
N_MOD = 9
D_FF = 2816
EPS = 1e-6
CONV_DIM = 512
CONV_W = 31
CONV_PAD = 15
HEAD_DIM = 64
ATT_HEADS = 8
KV_HEADS = 2
ATT_DIM = 512
KV_DIM = 128
WINDOW = 128
BLK = 128
ROPE_BASE = 10000.0
ROPE_FREQS = 16
Q_OFF = 2 * CONV_DIM
K_OFF = Q_OFF + ATT_DIM
V_OFF = K_OFF + KV_DIM
ML_HEADS = 4
ML_DH = 256
ML_DIM = 1024
ML_CHUNK = 128

T_LAT = BATCH * SEQ
T_CTX = BATCH * CTX
T_ALL = T_LAT + T_CTX
TM = 1024
N_LAT_TILES = T_LAT // TM
N_ALL_TILES = T_ALL // TM
MOD_ROWS = 8
HALO = 16
NEG = -1e30
VMEM_LIMIT = 56 * 1024 * 1024
VMEM_LIMIT_FFN = 60 * 1024 * 1024

F32 = jnp.float32
BF16 = jnp.bfloat16


def _sigmoid(x):
    return 1.0 / (1.0 + jnp.exp(-x))


def _silu(x):
    return x * _sigmoid(x)


def _dot(a, b, precision=None):
    return jnp.dot(a, b, preferred_element_type=F32, precision=precision)


def _dot_nt(a, b, precision=None):
    return lax.dot_general(a, b, (((1,), (1,)), ((), ())),
                           preferred_element_type=F32, precision=precision)


def _dot_tn(a, b):
    return lax.dot_general(a, b, (((0,), (0,)), ((), ())), preferred_element_type=F32)


def _rms_mod(x, g, shift, scale):
    y = x * lax.rsqrt(jnp.mean(x * x, axis=-1, keepdims=True) + EPS)
    return (y * g) * (1.0 + scale) + shift


def _resident(shape):
    nd = len(shape)
    return pl.BlockSpec(shape, lambda *_: (0,) * nd, pipeline_mode=pl.Buffered(1))


def _params(n_axes=1, vmem_limit=VMEM_LIMIT):
    return pltpu.CompilerParams(dimension_semantics=("arbitrary",) * n_axes,
                                vmem_limit_bytes=vmem_limit)


def _mod_kernel(c_ref, w_ref, b_ref, o_ref):
    s = _silu(c_ref[...]).astype(BF16)
    o_ref[...] = _dot(s, w_ref[...].astype(BF16)) + b_ref[...]


def _ada_mods(cond, mod_w, mod_b):
    tn = 1024
    n = N_MOD * D
    return pl.pallas_call(
        _mod_kernel,
        grid=(DEPTH, n // tn),
        in_specs=[
            pl.BlockSpec((MOD_ROWS, D), lambda l, j: (0, 0)),
            pl.BlockSpec((None, D, tn), lambda l, j: (l, 0, j)),
            pl.BlockSpec((None, 1, tn), lambda l, j: (l, 0, j)),
        ],
        out_specs=pl.BlockSpec((None, MOD_ROWS, tn), lambda l, j: (l, 0, j)),
        out_shape=jax.ShapeDtypeStruct((DEPTH, MOD_ROWS, n), F32),
        compiler_params=_params(2),
        name="ada_mods",
    )(cond, mod_w, mod_b.reshape(DEPTH, 1, n))


def _mod_spec(k):
    return pl.BlockSpec((None, 1, 3 * D), lambda i: (i // (SEQ // TM), 0, k))


FF_CHUNKS = tuple((c, min(c + 512, D_FF)) for c in range(0, D_FF, 512))


STAGE = 256
N_STAGE = 3


def _stage_weights(src_ref, dst_ref, stage_ref, sem_ref, chunks):
    def copy(n):
        slot = n % N_STAGE
        return pltpu.make_async_copy(src_ref.at[chunks[n][0]], stage_ref.at[slot], sem_ref.at[slot])

    for n in range(min(N_STAGE, len(chunks))):
        copy(n).start()
    for n in range(len(chunks)):
        copy(n).wait()
        dst_ref[chunks[n][1]] = stage_ref[n % N_STAGE].astype(BF16)
        if n + N_STAGE < len(chunks):
            copy(n + N_STAGE).start()


def _ffn_kernel(x_ref, *rest, layer, split, final):
    if split:
        xc_ref, *rest = rest
    m_ref, g_ref, wgu_hbm, wd_hbm, *rest = rest
    if final:
        fn_ref, *rest = rest
    o_ref, wgu_ref, wd_ref, sgu_ref, sd_ref, sem_gu, sem_d = rest

    @pl.when(pl.program_id(0) == 0)
    def _():
        _stage_weights(wgu_hbm, wgu_ref, sgu_ref, sem_gu,
                       [((layer, slice(None), pl.ds(c, STAGE)), (slice(None), pl.ds(c, STAGE)))
                        for c in range(0, 2 * D_FF, STAGE)])
        _stage_weights(wd_hbm, wd_ref, sd_ref, sem_d,
                       [((layer, pl.ds(r, STAGE), slice(None)), (pl.ds(r, STAGE), slice(None)))
                        for r in range(0, D_FF, STAGE)])

    x = x_ref[...]
    if split:
        x = jnp.where(pl.program_id(0) < N_LAT_TILES, x, xc_ref[...])
    shift = m_ref[:, 0:D]
    scale = m_ref[:, D:2 * D]
    gate = m_ref[:, 2 * D:3 * D]
    h = _rms_mod(x, g_ref[...], shift, scale).astype(BF16)
    acc = None
    for c0, c1 in FF_CHUNKS:
        hg = _dot(h, wgu_ref[:, c0:c1])
        hu = _dot(h, wgu_ref[:, D_FF + c0:D_FF + c1])
        a = (_silu(hg) * hu).astype(BF16)
        p = _dot(a, wd_ref[c0:c1, :])
        acc = p if acc is None else acc + p
    y = x + (0.5 * gate) * acc
    if final:
        y = (y * lax.rsqrt(jnp.mean(y * y, axis=-1, keepdims=True) + EPS)) * fn_ref[...]
    o_ref[...] = y


def _ffn(x, mods, k, norm_g, w_gu, w_d, layer, n_tiles, final_g=None, x_ctx=None):
    final = final_g is not None
    split = x_ctx is not None
    if split:
        in_specs = [pl.BlockSpec((TM, D), lambda i: (jnp.minimum(i, N_LAT_TILES - 1), 0)),
                    _resident((TM, D))]
        args = [x, x_ctx]
    else:
        in_specs = [pl.BlockSpec((TM, D), lambda i: (i, 0))]
        args = [x]
    in_specs += [
        _mod_spec(k),
        _resident((1, D)),
        pl.BlockSpec(memory_space=pl.ANY),
        pl.BlockSpec(memory_space=pl.ANY),
    ]
    args += [mods, norm_g.reshape(1, D), w_gu, w_d]
    if final:
        in_specs.append(_resident((1, D)))
        args.append(final_g.reshape(1, D))
    return pl.pallas_call(
        functools.partial(_ffn_kernel, layer=layer, split=split, final=final),
        grid=(n_tiles,),
        in_specs=in_specs,
        out_specs=pl.BlockSpec((TM, D), lambda i: (i, 0)),
        out_shape=jax.ShapeDtypeStruct((n_tiles * TM, D), F32),
        scratch_shapes=[pltpu.VMEM((D, 2 * D_FF), BF16),
                        pltpu.VMEM((D_FF, D), BF16),
                        pltpu.VMEM((N_STAGE, D, STAGE), F32),
                        pltpu.VMEM((N_STAGE, STAGE, D), F32),
                        pltpu.SemaphoreType.DMA((N_STAGE,)),
                        pltpu.SemaphoreType.DMA((N_STAGE,))],
        compiler_params=_params(1, VMEM_LIMIT_FFN),
        name="ffn_final" if final else ("ffn_first" if split else "ffn"),
    )(*args)


def _even_in_kernel(x_ref, m_ref, g_ref, wvg_ref, wqt_ref, wk_ref, wvt_ref,
                    c_ref, s1_ref, s2_ref, ct_ref, s1t_ref, s2t_ref,
                    u_ref, qt_ref, k_ref, vt_ref):
    x = x_ref[...]
    h = _rms_mod(x, g_ref[...], m_ref[:, 0:D], m_ref[:, D:2 * D]).astype(BF16)
    vg = _dot(h, wvg_ref[...])
    u_ref[...] = vg[:, 0:CONV_DIM] * _sigmoid(vg[:, CONV_DIM:Q_OFF])
    qf = _dot_nt(wqt_ref[...], h)
    ct, s1t, s2t = ct_ref[...], s1t_ref[...], s2t_ref[...]
    for hd in range(ATT_HEADS):
        xh = qf[hd * HEAD_DIM:(hd + 1) * HEAD_DIM]
        up = jnp.concatenate([xh[ROPE_FREQS:], xh[:ROPE_FREQS]], axis=0)
        dn = jnp.concatenate([xh[HEAD_DIM - ROPE_FREQS:], xh[:HEAD_DIM - ROPE_FREQS]], axis=0)
        r = xh * ct + up * s1t + dn * s2t
        qt_ref[hd * HEAD_DIM:(hd + 1) * HEAD_DIM, :] = (r * (HEAD_DIM ** -0.5)).astype(BF16)
    kf = _dot(h, wk_ref[...])
    kr = kf * c_ref[...] + pltpu.roll(kf, 128 - ROPE_FREQS, 1) * s1_ref[...] + pltpu.roll(kf, ROPE_FREQS, 1) * s2_ref[...]
    k_ref[...] = kr.astype(BF16)
    vt_ref[...] = _dot_nt(wvt_ref[...], h).astype(BF16)


def _even_in(x, mods, norm_g, w_vg, w_q_t, w_k, w_v_t, rope, rope_t):
    pos = lambda i: jnp.where(i < N_LAT_TILES, i % (SEQ // TM), SEQ // TM)
    rope_spec = lambda: pl.BlockSpec((TM, KV_DIM), lambda i: (pos(i), 0))
    rope_t_spec = lambda: pl.BlockSpec((HEAD_DIM, TM), lambda i: (0, pos(i)))
    row = lambda i: (i, 0)
    col = lambda i: (0, i)
    return pl.pallas_call(
        _even_in_kernel,
        grid=(N_ALL_TILES,),
        in_specs=[
            pl.BlockSpec((TM, D), row),
            _mod_spec(1),
            _resident((1, D)),
            _resident((D, Q_OFF)),
            _resident((ATT_DIM, D)),
            _resident((D, KV_DIM)),
            _resident((KV_DIM, D)),
            rope_spec(), rope_spec(), rope_spec(),
            rope_t_spec(), rope_t_spec(), rope_t_spec(),
        ],
        out_specs=[
            pl.BlockSpec((TM, CONV_DIM), row),
            pl.BlockSpec((ATT_DIM, TM), col),
            pl.BlockSpec((TM, KV_DIM), row),
            pl.BlockSpec((KV_DIM, TM), col),
        ],
        out_shape=[
            jax.ShapeDtypeStruct((T_ALL, CONV_DIM), F32),
            jax.ShapeDtypeStruct((ATT_DIM, T_ALL), BF16),
            jax.ShapeDtypeStruct((T_ALL, KV_DIM), BF16),
            jax.ShapeDtypeStruct((KV_DIM, T_ALL), BF16),
        ],
        compiler_params=_params(1),
        name="even_in",
    )(x, mods, norm_g.reshape(1, D), w_vg, w_q_t, w_k, w_v_t, *rope, *rope_t)


def _rope_tables():
    rows = SEQ // GRID_W
    row = jnp.repeat(jnp.arange(rows, dtype=F32), GRID_W)
    col = jnp.tile(jnp.arange(GRID_W, dtype=F32), rows)
    inv = ROPE_BASE ** (-jnp.arange(ROPE_FREQS, dtype=F32) / ROPE_FREQS)
    ang = jnp.concatenate([row[:, None] * inv, col[:, None] * inv], axis=-1)
    cos, sin = jnp.cos(ang), jnp.sin(ang)
    d = np.arange(128) % HEAD_DIM
    src = (d // 32) * ROPE_FREQS + d % ROPE_FREQS
    first = jnp.asarray(((d % 32) // ROPE_FREQS) == 0)
    c = cos[:, src]
    s = sin[:, src]
    s1 = jnp.where(first, -s, 0.0)
    s2 = jnp.where(first, 0.0, s)
    pad = lambda t, v: jnp.concatenate([t, jnp.full((TM, 128), v, F32)], axis=0)
    tok = (pad(c, 1.0), pad(s1, 0.0), pad(s2, 0.0))
    return tok, tuple(t[:, :HEAD_DIM].T for t in tok)


NB = SEQ // BLK
NCB = CTX // BLK
GROUP = ATT_HEADS // KV_HEADS


def _attn_kernel(qt_ref, kp_ref, kc_ref, kn_ref, kx_ref, vp_ref, vc_ref, vn_ref, vx_ref,
                 sink_ref, o_ref):
    i = pl.program_id(1)
    kk = lax.broadcasted_iota(jnp.int32, (3 * BLK, BLK), 0)
    qq = lax.broadcasted_iota(jnp.int32, (3 * BLK, BLK), 1)
    rel = kk - BLK - qq
    kpos = (i - 1) * BLK + kk
    ok = (jnp.abs(rel) <= WINDOW) & (kpos >= 0) & (kpos < SEQ) & (i < NB)
    bias1 = jnp.where(ok, 0.0, NEG).astype(F32)
    bias = jnp.concatenate([bias1] * GROUP, axis=1)
    keys = jnp.concatenate([kp_ref[...], kc_ref[...], kn_ref[...], kx_ref[...]], axis=0)
    vt = jnp.concatenate([vp_ref[...], vc_ref[...], vn_ref[...], vx_ref[...]], axis=1)
    qt = qt_ref[...]
    zero = jnp.zeros((HEAD_DIM, BLK), BF16)
    outs = []
    for g in range(KV_HEADS):
        cols = []
        for hh in range(GROUP):
            hd = g * GROUP + hh
            qh = qt[hd * HEAD_DIM:(hd + 1) * HEAD_DIM]
            cols.append(jnp.concatenate([qh, zero] if g == 0 else [zero, qh], axis=0))
        s = _dot(keys, jnp.concatenate(cols, axis=1))
        s_b = s[0:3 * BLK] + bias
        s_x = s[3 * BLK:]
        sink = sink_ref[:, g * GROUP * BLK:(g + 1) * GROUP * BLK]
        m = jnp.maximum(jnp.maximum(jnp.max(s_b, axis=0, keepdims=True),
                                    jnp.max(s_x, axis=0, keepdims=True)), sink)
        p_b = jnp.exp(s_b - m)
        p_x = jnp.exp(s_x - m)
        den = (jnp.sum(p_b, axis=0, keepdims=True) + jnp.sum(p_x, axis=0, keepdims=True)
               + jnp.exp(sink - m))
        p = jnp.concatenate([p_b, p_x], axis=0).astype(BF16)
        ot = _dot(vt[g * HEAD_DIM:(g + 1) * HEAD_DIM], p) * (1.0 / den)
        outs += [ot[:, hh * BLK:(hh + 1) * BLK] for hh in range(GROUP)]
    o_ref[...] = jnp.concatenate(outs, axis=0).T.astype(BF16)


def _attention(qt, k, vt, sink):
    lat_blocks = T_LAT // BLK
    q_blk = lambda b, i: jnp.where(i < NB, b * NB + i, lat_blocks + b * NCB + (i - NB))
    band = lambda off: (lambda b, i: b * NB + jnp.clip(i + off, 0, NB - 1))
    ctx_blk = lambda b, i: T_LAT // CTX + b
    k_spec = lambda n, f: pl.BlockSpec((n, KV_DIM), lambda b, i: (f(b, i), 0))
    v_spec = lambda n, f: pl.BlockSpec((KV_DIM, n), lambda b, i: (0, f(b, i)))
    return pl.pallas_call(
        _attn_kernel,
        grid=(BATCH, NB + NCB),
        in_specs=[
            pl.BlockSpec((ATT_DIM, BLK), lambda b, i: (0, q_blk(b, i))),
            k_spec(BLK, band(-1)), k_spec(BLK, band(0)), k_spec(BLK, band(1)), k_spec(CTX, ctx_blk),
            v_spec(BLK, band(-1)), v_spec(BLK, band(0)), v_spec(BLK, band(1)), v_spec(CTX, ctx_blk),
            pl.BlockSpec((1, ATT_HEADS * BLK), lambda b, i: (0, 0)),
        ],
        out_specs=pl.BlockSpec((BLK, ATT_DIM), lambda b, i: (q_blk(b, i), 0)),
        out_shape=jax.ShapeDtypeStruct((T_ALL, ATT_DIM), BF16),
        compiler_params=_params(2),
        name="window_attn",
    )(qt, k, k, k, k, vt, vt, vt, vt, sink)


CONV_S = 4
CONV_ROWS = 8 * CONV_S
N_SLAB = CONV_DIM // 128


def _even_out_kernel(x_ref, m_ref, u_ref, up_ref, un_ref, att_ref, cw_ref, cb_ref,
                     lg_ref, lb_ref, wa_ref, wb_ref, o_ref, ext_ref, a_ref, cva_ref, cvb_ref, *, tm, tiles_per_seq):
    t = pl.program_id(0)
    first = (t % tiles_per_seq) == 0
    last = (t % tiles_per_seq) == tiles_per_seq - 1
    slabs = [slice(s * 128, (s + 1) * 128) for s in range(N_SLAB)]
    for s, ls in enumerate(slabs):
        ext_ref[s, 0:HALO, :] = jnp.where(first, 0.0, up_ref[:, ls])
        ext_ref[s, HALO:HALO + tm, :] = u_ref[:, ls]
        ext_ref[s, HALO + tm:, :] = jnp.where(last, 0.0, un_ref[:, ls])

    def conv_block(blk, cv_ref):
        base = blk * CONV_ROWS
        for s, ls in enumerate(slabs):
            acc = [jnp.broadcast_to(cb_ref[:, ls], (8, 128)) for _ in range(CONV_S)]
            for o in range(CONV_W + CONV_S - 1):
                v = ext_ref[s, pl.ds(base + (HALO - CONV_PAD) + o, 8, stride=CONV_S), :]
                for j in range(CONV_S):
                    k = o - j
                    if 0 <= k < CONV_W:
                        acc[j] = acc[j] + cw_ref[k:k + 1, ls] * v
            for j in range(CONV_S):
                cv_ref[s * CONV_S + j] = acc[j]

    def norm_block(blk, cv_ref):
        base = blk * CONV_ROWS
        for j in range(CONV_S):
            row = [cv_ref[s * CONV_S + j] for s in range(N_SLAB)]
            mu = jnp.sum(sum(row), axis=-1, keepdims=True) * (1.0 / CONV_DIM)
            cen = [r - mu for r in row]
            var = jnp.sum(sum(c * c for c in cen), axis=-1, keepdims=True) * (1.0 / CONV_DIM)
            rs = lax.rsqrt(var + EPS)
            for s, ls in enumerate(slabs):
                y = cen[s] * rs * lg_ref[:, ls] + lb_ref[:, ls]
                a_ref[s, pl.ds(base + j, 8, stride=CONV_S), :] = _silu(y)

    n_blk = tm // CONV_ROWS
    conv_block(0, cva_ref)

    def body(i, carry):
        conv_block(2 * i + 1, cvb_ref)
        norm_block(2 * i, cva_ref)
        conv_block(jnp.minimum(2 * i + 2, n_blk - 1), cva_ref)
        norm_block(2 * i + 1, cvb_ref)
        return carry

    lax.fori_loop(0, n_blk // 2, body, 0)
    a = jnp.concatenate([a_ref[s] for s in range(N_SLAB)], axis=1).astype(BF16)
    y = _dot(a, wa_ref[...]) + _dot(att_ref[...], wb_ref[...])
    o_ref[...] = x_ref[...] + m_ref[:, 2 * D:3 * D] * y


def _even_out(x, mods, u, att, conv_w, conv_b, ln_g, ln_b, wa, wb, *, tm, row0, n_tiles,
              tiles_per_seq, mod_row):
    blk0 = row0 // tm
    hb = tm // HALO
    n_halo = T_ALL // HALO
    row_map = lambda i: (blk0 + i, 0)
    in_specs = [
        pl.BlockSpec((tm, D), row_map),
        pl.BlockSpec((None, 1, 3 * D), lambda i: (mod_row(i), 0, 1)),
        pl.BlockSpec((tm, CONV_DIM), row_map),
        pl.BlockSpec((HALO, CONV_DIM), lambda i: (jnp.maximum((blk0 + i) * hb - 1, 0), 0)),
        pl.BlockSpec((HALO, CONV_DIM), lambda i: (jnp.minimum((blk0 + i + 1) * hb, n_halo - 1), 0)),
        pl.BlockSpec((tm, ATT_DIM), row_map),
        _resident((32, CONV_DIM)),
        _resident((1, CONV_DIM)),
        _resident((1, CONV_DIM)),
        _resident((1, CONV_DIM)),
        _resident((CONV_DIM, D)),
        _resident((ATT_DIM, D)),
    ]
    return pl.pallas_call(
        functools.partial(_even_out_kernel, tm=tm, tiles_per_seq=tiles_per_seq),
        grid=(n_tiles,),
        in_specs=in_specs,
        out_specs=pl.BlockSpec((tm, D), row_map),
        out_shape=jax.ShapeDtypeStruct((T_ALL, D), F32),
        scratch_shapes=[pltpu.VMEM((N_SLAB, tm + 2 * HALO, 128), F32),
                        pltpu.VMEM((N_SLAB, tm, 128), F32),
                        pltpu.VMEM((N_SLAB * CONV_S, 8, 128), F32),
                        pltpu.VMEM((N_SLAB * CONV_S, 8, 128), F32)],
        input_output_aliases={0: 0},
        compiler_params=_params(1),
        name="even_out" if tiles_per_seq > 1 else "even_out_ctx",
    )(x, mods, u, u, u, att, conv_w, conv_b, ln_g, ln_b, wa, wb)


N_GATE = 4 * ML_HEADS
N_CHAIN = 2 * ML_HEADS
L = ML_CHUNK


def _log_sigmoid(x):
    return jnp.minimum(x, 0.0) - jnp.log(1.0 + jnp.exp(-jnp.abs(x)))


def _lane_scan(x, lane, fwd_rows, combine, fill):
    sh = 1
    while sh < L:
        pre = jnp.where(lane >= sh, pltpu.roll(x, sh, 1), fill)
        suf = jnp.where(lane < L - sh, pltpu.roll(x, L - sh, 1), fill)
        x = combine(x, jnp.where(fwd_rows, pre, suf))
        sh *= 2
    return x


def _odd_in_kernel(x_ref, m_ref, g_ref, wqt_ref, wk_ref, wvt_ref, wgt_ref, bgt_ref,
                   qt_ref, k_ref, vt_ref, a_ref, amax_ref, b_ref):
    x = x_ref[...]
    hf = _rms_mod(x, g_ref[...], m_ref[:, 0:D], m_ref[:, D:2 * D])
    h = hf.astype(BF16)
    g = _dot_nt(wgt_ref[...], hf, precision=lax.Precision.HIGHEST) + bgt_ref[...]
    qt_ref[...] = _dot_nt(wqt_ref[...], h).astype(BF16)
    k_ref[...] = (_dot(h, wk_ref[...]) * (ML_DH ** -0.5)).astype(BF16)
    vt_ref[...] = _dot_nt(wvt_ref[...], h).astype(BF16)
    li = g[0:N_CHAIN]
    lf = _log_sigmoid(g[N_CHAIN:N_GATE])
    fwd_rows = lax.broadcasted_iota(jnp.int32, (N_CHAIN, L), 0) < ML_HEADS
    lane = lax.broadcasted_iota(jnp.int32, (N_CHAIN, L), 1)
    for ch in range(TM // L):
        cs = slice(ch * L, (ch + 1) * L)
        b = _lane_scan(lf[:, cs], lane, fwd_rows, jnp.add, 0.0)
        a = li[:, cs] - b
        a_ref[:, cs] = a
        amax_ref[:, cs] = _lane_scan(a, lane, fwd_rows, jnp.maximum, NEG)
        b_ref[:, cs] = b


def _odd_in(x, mods, norm_g, w_q_t, w_k, w_v_t, w_gate_t, b_gate_t):
    row = lambda i: (i, 0)
    col = lambda i: (0, i)
    return pl.pallas_call(
        _odd_in_kernel,
        grid=(N_ALL_TILES,),
        in_specs=[
            pl.BlockSpec((TM, D), row),
            _mod_spec(1),
            _resident((1, D)),
            _resident((ML_DIM, D)),
            _resident((D, ML_DIM)),
            _resident((ML_DIM, D)),
            _resident((N_GATE, D)),
            _resident((N_GATE, 1)),
        ],
        out_specs=[
            pl.BlockSpec((ML_DIM, TM), col),
            pl.BlockSpec((TM, ML_DIM), row),
            pl.BlockSpec((ML_DIM, TM), col),
            pl.BlockSpec((N_CHAIN, TM), col),
            pl.BlockSpec((N_CHAIN, TM), col),
            pl.BlockSpec((N_CHAIN, TM), col),
        ],
        out_shape=[
            jax.ShapeDtypeStruct((ML_DIM, T_ALL), BF16),
            jax.ShapeDtypeStruct((T_ALL, ML_DIM), BF16),
            jax.ShapeDtypeStruct((ML_DIM, T_ALL), BF16),
            jax.ShapeDtypeStruct((N_CHAIN, T_ALL), F32),
            jax.ShapeDtypeStruct((N_CHAIN, T_ALL), F32),
            jax.ShapeDtypeStruct((N_CHAIN, T_ALL), F32),
        ],
        compiler_params=_params(1),
        name="odd_in",
    )(x, mods, norm_g.reshape(1, D), w_q_t, w_k, w_v_t, w_gate_t, b_gate_t)


N_CTX_CHUNKS = CTX // ML_CHUNK
N_LAT_CHUNKS = SEQ // ML_CHUNK
N_STEPS = N_CTX_CHUNKS + N_LAT_CHUNKS
N_AUG = 16


def _mlstm_kernel(qtf_ref, kf_ref, vtf_ref, qtb_ref, kb_ref, vtb_ref,
                  af_ref, amaxf_ref, bf_ref, ab_ref, amaxb_ref, bb_ref, hf_ref, hb_ref, c_ref, m_ref):
    t = pl.program_id(1)

    @pl.when(t == 0)
    def _():
        c_ref[...] = jnp.zeros_like(c_ref)
        m_ref[...] = jnp.zeros_like(m_ref)

    fwd_rows = lax.broadcasted_iota(jnp.int32, (N_CHAIN, L), 0) < ML_HEADS
    a = jnp.where(fwd_rows, af_ref[...], ab_ref[...])
    amax = jnp.where(fwd_rows, amaxf_ref[...], amaxb_ref[...])
    b = jnp.where(fwd_rows, bf_ref[...], bb_ref[...])
    m_old = m_ref[:, 0:1]
    big = jnp.maximum(m_old, jnp.max(amax, axis=1, keepdims=True))
    decay = jnp.exp(m_old - big)
    e = jnp.exp(a - big)
    mm = jnp.maximum(amax, m_old)
    w_inter = jnp.exp(m_old - mm)
    floor = jnp.exp(-(b + mm))
    m_ref[...] = jnp.broadcast_to(jnp.min(b, axis=1, keepdims=True) + big, (N_CHAIN, 128))
    kq = lax.broadcasted_iota(jnp.int32, (L, L), 0)
    qq = lax.broadcasted_iota(jnp.int32, (L, L), 1)
    a_col = _dot_nt((kq == qq).astype(F32), a, precision=lax.Precision.HIGHEST)

    for d, (qt_ref, k_ref, vt_ref, h_ref) in enumerate(
            ((qtf_ref, kf_ref, vtf_ref, hf_ref), (qtb_ref, kb_ref, vtb_ref, hb_ref))):
        visible = (kq >= qq) if d == 1 else (kq <= qq)
        for hd in range(ML_HEADS):
            c = d * ML_HEADS + hd
            cs = slice(hd * ML_DH, (hd + 1) * ML_DH)
            qt = qt_ref[cs, :]
            k = k_ref[:, cs]
            vt = vt_ref[cs, :]
            p = jnp.where(visible, jnp.exp(a_col[:, c:c + 1] - mm[c:c + 1, :]), 0.0)
            c_old = c_ref[c]
            kcq = _dot(jnp.concatenate([k, c_old.astype(BF16)], axis=0), qt)
            st = kcq[0:L] * p
            cq = kcq[L:]
            wi = w_inter[c:c + 1, :]
            num = _dot(vt, st.astype(BF16)) + wi * cq[0:ML_DH]
            den = jnp.sum(st, axis=0, keepdims=True) + wi * cq[ML_DH:ML_DH + 1]
            ht = num * (1.0 / jnp.maximum(jnp.abs(den), floor[c:c + 1, :]))
            h_ref[:, cs] = ht.T.astype(BF16)
            er = e[c:c + 1, :]
            vte = jnp.concatenate([vt.astype(F32) * er, jnp.broadcast_to(er, (N_AUG, L))], axis=0)
            c_ref[c] = decay[c:c + 1, :] * c_old + _dot(vte.astype(BF16), k)


def _mlstm(qt, k, vt, a, amax, b):
    lat_chunks = T_LAT // L

    def fwd_in(b, t):
        return jnp.where(t < N_CTX_CHUNKS, lat_chunks + b * N_CTX_CHUNKS + t,
                         b * N_LAT_CHUNKS + (t - N_CTX_CHUNKS))

    def bwd_in(b, t):
        return jnp.where(t < N_CTX_CHUNKS, lat_chunks + b * N_CTX_CHUNKS + (N_CTX_CHUNKS - 1 - t),
                         b * N_LAT_CHUNKS + (N_STEPS - 1 - t))

    def fwd_out(b, t):
        return b * N_LAT_CHUNKS + jnp.maximum(t - N_CTX_CHUNKS, 0)

    def bwd_out(b, t):
        return b * N_LAT_CHUNKS + (N_STEPS - 1 - jnp.maximum(t, N_CTX_CHUNKS))

    rows = lambda f: pl.BlockSpec((L, ML_DIM), lambda b, t: (f(b, t), 0))
    cols = lambda f: pl.BlockSpec((ML_DIM, L), lambda b, t: (0, f(b, t)))
    gate = lambda f: pl.BlockSpec((N_CHAIN, L), lambda b, t: (0, f(b, t)))
    return pl.pallas_call(
        _mlstm_kernel,
        grid=(BATCH, N_STEPS),
        in_specs=[cols(fwd_in), rows(fwd_in), cols(fwd_in),
                  cols(bwd_in), rows(bwd_in), cols(bwd_in),
                  gate(fwd_in), gate(fwd_in), gate(fwd_in), gate(bwd_in), gate(bwd_in), gate(bwd_in)],
        out_specs=[rows(fwd_out), rows(bwd_out)],
        out_shape=[jax.ShapeDtypeStruct((T_LAT, ML_DIM), BF16),
                   jax.ShapeDtypeStruct((T_LAT, ML_DIM), BF16)],
        scratch_shapes=[pltpu.VMEM((N_CHAIN, ML_DH + N_AUG, ML_DH), F32),
                        pltpu.VMEM((N_CHAIN, 128), F32)],
        compiler_params=_params(2),
        name="mlstm",
    )(qt, k, vt, qt, k, vt, a, amax, b, a, amax, b)


def _odd_out_kernel(x_ref, m_ref, g_ref, hf_ref, hb_ref, wo_ref, ng_ref, wout_ref, o_ref):
    x = x_ref[...]
    h = _rms_mod(x, g_ref[...], m_ref[:, 0:D], m_ref[:, D:2 * D]).astype(BF16)
    o = _sigmoid(_dot(h, wo_ref[...]))
    hs = hf_ref[...].astype(F32) + hb_ref[...].astype(F32)
    parts = []
    for hd in range(ML_HEADS):
        p = hs[:, hd * ML_DH:(hd + 1) * ML_DH]
        parts.append(p * lax.rsqrt(jnp.mean(p * p, axis=-1, keepdims=True) + EPS))
    hn = jnp.concatenate(parts, axis=-1) * ng_ref[...]
    y = _dot((o * hn).astype(BF16), wout_ref[...])
    o_ref[...] = x + m_ref[:, 2 * D:3 * D] * y


def _odd_out(x, mods, norm_g, hf, hb, w_o, head_g, w_out):
    row = lambda i: (i, 0)
    return pl.pallas_call(
        _odd_out_kernel,
        grid=(N_LAT_TILES,),
        in_specs=[
            pl.BlockSpec((TM, D), row),
            _mod_spec(1),
            _resident((1, D)),
            pl.BlockSpec((TM, ML_DIM), row),
            pl.BlockSpec((TM, ML_DIM), row),
            _resident((D, ML_DIM)),
            _resident((1, ML_DIM)),
            _resident((ML_DIM, D)),
        ],
        out_specs=pl.BlockSpec((TM, D), row),
        out_shape=jax.ShapeDtypeStruct((T_LAT, D), F32),
        compiler_params=_params(1),
        name="odd_out",
    )(x, mods, norm_g.reshape(1, D), hf, hb, w_o, head_g.reshape(1, ML_DIM), w_out)


def kernel(x, c, ctx, c_ctx, mod_w, mod_b, ffn1_norm, ffn1_w_gu, ffn1_w_d, mix_norm, ffn2_norm,
           ffn2_w_gu, ffn2_w_d, ev_w_in, ev_conv_w, ev_conv_b, ev_conv_ln_g, ev_conv_ln_b, ev_sink,
           ev_w_out, od_w_in, od_b_gate, od_norm_g, od_w_out, final_norm):
    assert DEPTH == 2 and x.shape == (BATCH, SEQ, D) and ctx.shape == (BATCH, CTX, D)
    cond = jnp.zeros((MOD_ROWS, D), F32).at[:BATCH].set(c).at[BATCH].set(c_ctx)
    mods = _ada_mods(cond, mod_w, mod_b).reshape(DEPTH, MOD_ROWS, 1, N_MOD * D)
    bf = lambda w: w.astype(BF16)

    m0 = mods[0]
    xs = _ffn(x.reshape(T_LAT, D), m0, 0, ffn1_norm[0], ffn1_w_gu, ffn1_w_d, 0, N_ALL_TILES,
              x_ctx=ctx.reshape(T_CTX, D))
    w_in = ev_w_in[0]
    rope, rope_t = _rope_tables()
    u, qt, kk, vt = _even_in(xs, m0, mix_norm[0], bf(w_in[:, :Q_OFF]), bf(w_in[:, Q_OFF:K_OFF].T),
                             bf(w_in[:, K_OFF:V_OFF]), bf(w_in[:, V_OFF:].T), rope, rope_t)
    sink = jnp.repeat(ev_sink[0].astype(F32), BLK).reshape(1, ATT_HEADS * BLK)
    att = _attention(qt, kk, vt, sink)
    conv_w = jnp.concatenate([ev_conv_w[0], jnp.zeros((1, CONV_DIM), F32)], axis=0)
    ev_args = (conv_w, ev_conv_b[0].reshape(1, -1), ev_conv_ln_g[0].reshape(1, -1),
               ev_conv_ln_b[0].reshape(1, -1), bf(ev_w_out[0][:CONV_DIM]), bf(ev_w_out[0][CONV_DIM:]))
    xs = _even_out(xs, m0, u, att, *ev_args, tm=TM, row0=0, n_tiles=N_LAT_TILES,
                   tiles_per_seq=SEQ // TM, mod_row=lambda i: i // (SEQ // TM))
    xs = _even_out(xs, m0, u, att, *ev_args, tm=CTX, row0=T_LAT, n_tiles=BATCH,
                   tiles_per_seq=1, mod_row=lambda i: BATCH)
    xs = _ffn(xs, m0, 2, ffn2_norm[0], ffn2_w_gu, ffn2_w_d, 0, N_ALL_TILES)

    m1 = mods[1]
    xs = _ffn(xs, m1, 0, ffn1_norm[1], ffn1_w_gu, ffn1_w_d, 1, N_ALL_TILES)
    w_in = od_w_in[0]
    perm = np.concatenate([np.arange(0, 4), np.arange(8, 12), np.arange(4, 8), np.arange(12, 16)])
    w_gate_t = w_in[:, 4 * ML_DIM:].T[perm]
    b_gate_t = od_b_gate[0][perm].reshape(N_GATE, 1)
    qt, km, vt, ga, gamax, gb = _odd_in(xs, m1, mix_norm[1], bf(w_in[:, :ML_DIM].T),
                                        bf(w_in[:, ML_DIM:2 * ML_DIM]),
                                        bf(w_in[:, 2 * ML_DIM:3 * ML_DIM].T), w_gate_t, b_gate_t)
    hf, hb = _mlstm(qt, km, vt, ga, gamax, gb)
    xl = _odd_out(xs, m1, mix_norm[1], hf, hb, bf(w_in[:, 3 * ML_DIM:4 * ML_DIM]), od_norm_g[0],
                  bf(od_w_out[0]))
    out = _ffn(xl, m1, 2, ffn2_norm[1], ffn2_w_gu, ffn2_w_d, 1, N_LAT_TILES,
               final_g=final_norm)
    return out.reshape(BATCH, SEQ, D)
```

```python
import functools

import jax
import jax.numpy as jnp
import numpy as np
from jax import lax
from jax.experimental import pallas as pl
from jax.experimental.pallas import tpu as pltpu

D = 1024
BATCH = 4
SEQ = 4096
DEPTH = 2
GRID_W = 64
CTX = 256
N_MOD = 9
D_FF = 2816
EPS = 1e-6
CONV_DIM = 512
CONV_W = 31
CONV_PAD = 15
HEAD_DIM = 64
ATT_HEADS = 8
KV_HEADS = 2
ATT_DIM = 512
KV_DIM = 128
WINDOW = 128
BLK = 128
ROPE_BASE = 10000.0
ROPE_FREQS = 16
Q_OFF = 2 * CONV_DIM
K_OFF = Q_OFF + ATT_DIM
V_OFF = K_OFF + KV_DIM
ML_HEADS = 4
ML_DH = 256
ML_DIM = 1024
ML_CHUNK = 128

T_LAT = BATCH * SEQ
T_CTX = BATCH * CTX
T_ALL = T_LAT + T_CTX
TM = 1024
N_LAT_TILES = T_LAT // TM
N_ALL_TILES = T_ALL // TM
MOD_ROWS = 8
HALO = 16
NEG = -1e30
VMEM_LIMIT = 56 * 1024 * 1024
VMEM_LIMIT_FFN = 60 * 1024 * 1024

F32 = jnp.float32
BF16 = jnp.bfloat16


def _sigmoid(x):
    return 1.0 / (1.0 + jnp.exp(-x))


def _silu(x):
    return x * _sigmoid(x)


def _dot(a, b, precision=None):
    return jnp.dot(a, b, preferred_element_type=F32, precision=precision)


def _dot_nt(a, b, precision=None):
    return lax.dot_general(a, b, (((1,), (1,)), ((), ())),
                           preferred_element_type=F32, precision=precision)


def _dot_tn(a, b):
    return lax.dot_general(a, b, (((0,), (0,)), ((), ())), preferred_element_type=F32)


def _rms_mod(x, g, shift, scale):
    y = x * lax.rsqrt(jnp.mean(x * x, axis=-1, keepdims=True) + EPS)
    return (y * g) * (1.0 + scale) + shift


def _resident(shape):
    nd = len(shape)
    return pl.BlockSpec(shape, lambda *_: (0,) * nd, pipeline_mode=pl.Buffered(1))


def _params(n_axes=1, vmem_limit=VMEM_LIMIT):
    return pltpu.CompilerParams(dimension_semantics=("arbitrary",) * n_axes,
                                vmem_limit_bytes=vmem_limit)


def _mod_kernel(c_ref, w_ref, b_ref, o_ref):
    s = _silu(c_ref[...]).astype(BF16)
    o_ref[...] = _dot(s, w_ref[...].astype(BF16)) + b_ref[...]


def _ada_mods(cond, mod_w, mod_b):
    tn = 1024
    n = N_MOD * D
    return pl.pallas_call(
        _mod_kernel,
        grid=(DEPTH, n // tn),
        in_specs=[
            pl.BlockSpec((MOD_ROWS, D), lambda l, j: (0, 0)),
            pl.BlockSpec((None, D, tn), lambda l, j: (l, 0, j)),
            pl.BlockSpec((None, 1, tn), lambda l, j: (l, 0, j)),
        ],
        out_specs=pl.BlockSpec((None, MOD_ROWS, tn), lambda l, j: (l, 0, j)),
        out_shape=jax.ShapeDtypeStruct((DEPTH, MOD_ROWS, n), F32),
        compiler_params=_params(2),
        name="ada_mods",
    )(cond, mod_w, mod_b.reshape(DEPTH, 1, n))


def _mod_spec(k):
    return pl.BlockSpec((None, 1, 3 * D), lambda i: (i // (SEQ // TM), 0, k))


FF_CHUNKS = tuple((c, min(c + 512, D_FF)) for c in range(0, D_FF, 512))


STAGE = 256
N_STAGE = 3


def _stage_weights(src_ref, dst_ref, stage_ref, sem_ref, chunks):
    def copy(n):
        slot = n % N_STAGE
        return pltpu.make_async_copy(src_ref.at[chunks[n][0]], stage_ref.at[slot], sem_ref.at[slot])

    for n in range(min(N_STAGE, len(chunks))):
        copy(n).start()
    for n in range(len(chunks)):
        copy(n).wait()
        dst_ref[chunks[n][1]] = stage_ref[n % N_STAGE].astype(BF16)
        if n + N_STAGE < len(chunks):
            copy(n + N_STAGE).start()


def _ffn_kernel(x_ref, *rest, layer, split, final):
    if split:
        xc_ref, *rest = rest
    m_ref, g_ref, wgu_hbm, wd_hbm, *rest = rest
    if final:
        fn_ref, *rest = rest
    o_ref, wgu_ref, wd_ref, sgu_ref, sd_ref, sem_gu, sem_d = rest

    @pl.when(pl.program_id(0) == 0)
    def _():
        _stage_weights(wgu_hbm, wgu_ref, sgu_ref, sem_gu,
                       [((layer, slice(None), pl.ds(c, STAGE)), (slice(None), pl.ds(c, STAGE)))
                        for c in range(0, 2 * D_FF, STAGE)])
        _stage_weights(wd_hbm, wd_ref, sd_ref, sem_d,
                       [((layer, pl.ds(r, STAGE), slice(None)), (pl.ds(r, STAGE), slice(None)))
                        for r in range(0, D_FF, STAGE)])

    x = x_ref[...]
    if split:
        x = jnp.where(pl.program_id(0) < N_LAT_TILES, x, xc_ref[...])
    shift = m_ref[:, 0:D]
    scale = m_ref[:, D:2 * D]
    gate = m_ref[:, 2 * D:3 * D]
    h = _rms_mod(x, g_ref[...], shift, scale).astype(BF16)
    acc = None
    for c0, c1 in FF_CHUNKS:
        hg = _dot(h, wgu_ref[:, c0:c1])
        hu = _dot(h, wgu_ref[:, D_FF + c0:D_FF + c1])
        a = (_silu(hg) * hu).astype(BF16)
        p = _dot(a, wd_ref[c0:c1, :])
        acc = p if acc is None else acc + p
    y = x + (0.5 * gate) * acc
    if final:
        y = (y * lax.rsqrt(jnp.mean(y * y, axis=-1, keepdims=True) + EPS)) * fn_ref[...]
    o_ref[...] = y


def _ffn(x, mods, k, norm_g, w_gu, w_d, layer, n_tiles, final_g=None, x_ctx=None):
    final = final_g is not None
    split = x_ctx is not None
    if split:
        in_specs = [pl.BlockSpec((TM, D), lambda i: (jnp.minimum(i, N_LAT_TILES - 1), 0)),
                    _resident((TM, D))]
        args = [x, x_ctx]
    else:
        in_specs = [pl.BlockSpec((TM, D), lambda i: (i, 0))]
        args = [x]
    in_specs += [
        _mod_spec(k),
        _resident((1, D)),
        pl.BlockSpec(memory_space=pl.ANY),
        pl.BlockSpec(memory_space=pl.ANY),
    ]
    args += [mods, norm_g.reshape(1, D), w_gu, w_d]
    if final:
        in_specs.append(_resident((1, D)))
        args.append(final_g.reshape(1, D))
    return pl.pallas_call(
        functools.partial(_ffn_kernel, layer=layer, split=split, final=final),
        grid=(n_tiles,),
        in_specs=in_specs,
        out_specs=pl.BlockSpec((TM, D), lambda i: (i, 0)),
        out_shape=jax.ShapeDtypeStruct((n_tiles * TM, D), F32),
        scratch_shapes=[pltpu.VMEM((D, 2 * D_FF), BF16),
                        pltpu.VMEM((D_FF, D), BF16),
                        pltpu.VMEM((N_STAGE, D, STAGE), F32),
                        pltpu.VMEM((N_STAGE, STAGE, D), F32),
                        pltpu.SemaphoreType.DMA((N_STAGE,)),
                        pltpu.SemaphoreType.DMA((N_STAGE,))],
        compiler_params=_params(1, VMEM_LIMIT_FFN),
        name="ffn_final" if final else ("ffn_first" if split else "ffn"),
    )(*args)


def _even_in_kernel(x_ref, m_ref, g_ref, wvg_ref, wqt_ref, wk_ref, wvt_ref,
                    c_ref, s1_ref, s2_ref, ct_ref, s1t_ref, s2t_ref,
                    u_ref, qt_ref, k_ref, vt_ref):
    x = x_ref[...]
    h = _rms_mod(x, g_ref[...], m_ref[:, 0:D], m_ref[:, D:2 * D]).astype(BF16)
    vg = _dot(h, wvg_ref[...])
    u_ref[...] = vg[:, 0:CONV_DIM] * _sigmoid(vg[:, CONV_DIM:Q_OFF])
    qf = _dot_nt(wqt_ref[...], h)
    ct, s1t, s2t = ct_ref[...], s1t_ref[...], s2t_ref[...]
    for hd in range(ATT_HEADS):
        xh = qf[hd * HEAD_DIM:(hd + 1) * HEAD_DIM]
        up = jnp.concatenate([xh[ROPE_FREQS:], xh[:ROPE_FREQS]], axis=0)
        dn = jnp.concatenate([xh[HEAD_DIM - ROPE_FREQS:], xh[:HEAD_DIM - ROPE_FREQS]], axis=0)
        r = xh * ct + up * s1t + dn * s2t
        qt_ref[hd * HEAD_DIM:(hd + 1) * HEAD_DIM, :] = (r * (HEAD_DIM ** -0.5)).astype(BF16)
    kf = _dot(h, wk_ref[...])
    kr = kf * c_ref[...] + pltpu.roll(kf, 128 - ROPE_FREQS, 1) * s1_ref[...] + pltpu.roll(kf, ROPE_FREQS, 1) * s2_ref[...]
    k_ref[...] = kr.astype(BF16)
    vt_ref[...] = _dot_nt(wvt_ref[...], h).astype(BF16)


def _even_in(x, mods, norm_g, w_vg, w_q_t, w_k, w_v_t, rope, rope_t):
    pos = lambda i: jnp.where(i < N_LAT_TILES, i % (SEQ // TM), SEQ // TM)
    rope_spec = lambda: pl.BlockSpec((TM, KV_DIM), lambda i: (pos(i), 0))
    rope_t_spec = lambda: pl.BlockSpec((HEAD_DIM, TM), lambda i: (0, pos(i)))
    row = lambda i: (i, 0)
    col = lambda i: (0, i)
    return pl.pallas_call(
        _even_in_kernel,
        grid=(N_ALL_TILES,),
        in_specs=[
            pl.BlockSpec((TM, D), row),
            _mod_spec(1),
            _resident((1, D)),
            _resident((D, Q_OFF)),
            _resident((ATT_DIM, D)),
            _resident((D, KV_DIM)),
            _resident((KV_DIM, D)),
            rope_spec(), rope_spec(), rope_spec(),
            rope_t_spec(), rope_t_spec(), rope_t_spec(),
        ],
        out_specs=[
            pl.BlockSpec((TM, CONV_DIM), row),
            pl.BlockSpec((ATT_DIM, TM), col),
            pl.BlockSpec((TM, KV_DIM), row),
            pl.BlockSpec((KV_DIM, TM), col),
        ],
        out_shape=[
            jax.ShapeDtypeStruct((T_ALL, CONV_DIM), F32),
            jax.ShapeDtypeStruct((ATT_DIM, T_ALL), BF16),
            jax.ShapeDtypeStruct((T_ALL, KV_DIM), BF16),
            jax.ShapeDtypeStruct((KV_DIM, T_ALL), BF16),
        ],
        compiler_params=_params(1),
        name="even_in",
    )(x, mods, norm_g.reshape(1, D), w_vg, w_q_t, w_k, w_v_t, *rope, *rope_t)


def _rope_tables():
    rows = SEQ // GRID_W
    row = jnp.repeat(jnp.arange(rows, dtype=F32), GRID_W)
    col = jnp.tile(jnp.arange(GRID_W, dtype=F32), rows)
    inv = ROPE_BASE ** (-jnp.arange(ROPE_FREQS, dtype=F32) / ROPE_FREQS)
    ang = jnp.concatenate([row[:, None] * inv, col[:, None] * inv], axis=-1)
    cos, sin = jnp.cos(ang), jnp.sin(ang)
    d = np.arange(128) % HEAD_DIM
    src = (d // 32) * ROPE_FREQS + d % ROPE_FREQS
    first = jnp.asarray(((d % 32) // ROPE_FREQS) == 0)
    c = cos[:, src]
    s = sin[:, src]
    s1 = jnp.where(first, -s, 0.0)
    s2 = jnp.where(first, 0.0, s)
    pad = lambda t, v: jnp.concatenate([t, jnp.full((TM, 128), v, F32)], axis=0)
    tok = (pad(c, 1.0), pad(s1, 0.0), pad(s2, 0.0))
    return tok, tuple(t[:, :HEAD_DIM].T for t in tok)


NB = SEQ // BLK
QB = 2 * BLK
GROUP = ATT_HEADS // KV_HEADS
assert CTX == QB


def _attn_kernel(qt_ref, kp_ref, kc_ref, kn_ref, kx_ref, vp_ref, vc_ref, vn_ref, vx_ref,
                 sink_ref, o_ref):
    for sub in range(QB // BLK):
        ks = [kp_ref[...], kc_ref[0:BLK], kc_ref[BLK:2 * BLK], kn_ref[...]][sub:sub + 3]
        vs = [vp_ref[...], vc_ref[:, 0:BLK], vc_ref[:, BLK:2 * BLK], vn_ref[...]][sub:sub + 3]
        o_ref[sub * BLK:(sub + 1) * BLK, :] = _attn_block(
            (QB // BLK) * pl.program_id(1) + sub, qt_ref[:, sub * BLK:(sub + 1) * BLK],
            ks + [kx_ref[...]], vs + [vx_ref[...]], sink_ref)


def _attn_block(i, qt, ks, vs, sink_ref):
    kk = lax.broadcasted_iota(jnp.int32, (3 * BLK, BLK), 0)
    qq = lax.broadcasted_iota(jnp.int32, (3 * BLK, BLK), 1)
    rel = kk - BLK - qq
    kpos = (i - 1) * BLK + kk
    ok = (jnp.abs(rel) <= WINDOW) & (kpos >= 0) & (kpos < SEQ) & (i < NB)
    bias1 = jnp.where(ok, 0.0, NEG).astype(F32)
    bias = jnp.concatenate([bias1] * GROUP, axis=1)
    keys = jnp.concatenate(ks, axis=0)
    vt = jnp.concatenate(vs, axis=1)
    zero = jnp.zeros((HEAD_DIM, BLK), BF16)
    outs = []
    for g in range(KV_HEADS):
        cols = []
        for hh in range(GROUP):
            hd = g * GROUP + hh
            qh = qt[hd * HEAD_DIM:(hd + 1) * HEAD_DIM]
            cols.append(jnp.concatenate([qh, zero] if g == 0 else [zero, qh], axis=0))
        s = _dot(keys, jnp.concatenate(cols, axis=1))
        s_b = s[0:3 * BLK] + bias
        s_x = s[3 * BLK:]
        sink = sink_ref[:, g * GROUP * BLK:(g + 1) * GROUP * BLK]
        m = jnp.maximum(jnp.maximum(jnp.max(s_b, axis=0, keepdims=True),
                                    jnp.max(s_x, axis=0, keepdims=True)), sink)
        p_b = jnp.exp(s_b - m)
        p_x = jnp.exp(s_x - m)
        den = (jnp.sum(p_b, axis=0, keepdims=True) + jnp.sum(p_x, axis=0, keepdims=True)
               + jnp.exp(sink - m))
        p = jnp.concatenate([p_b, p_x], axis=0).astype(BF16)
        ot = _dot(vt[g * HEAD_DIM:(g + 1) * HEAD_DIM], p) * (1.0 / den)
        outs += [ot[:, hh * BLK:(hh + 1) * BLK] for hh in range(GROUP)]
    return jnp.concatenate(outs, axis=0).T.astype(BF16)


def _attention(qt, k, vt, sink):
    n_pairs = SEQ // QB
    q_blk = lambda b, j: jnp.where(j < n_pairs, b * n_pairs + j, T_LAT // QB + b)
    pair = lambda b, j: b * n_pairs + jnp.minimum(j, n_pairs - 1)
    edge = lambda off: (lambda b, j: b * NB + jnp.clip((QB // BLK) * j + off, 0, NB - 1))
    ctx_blk = lambda b, j: T_LAT // CTX + b
    k_spec = lambda n, f: pl.BlockSpec((n, KV_DIM), lambda b, j: (f(b, j), 0))
    v_spec = lambda n, f: pl.BlockSpec((KV_DIM, n), lambda b, j: (0, f(b, j)))
    return pl.pallas_call(
        _attn_kernel,
        grid=(BATCH, n_pairs + 1),
        in_specs=[
            pl.BlockSpec((ATT_DIM, QB), lambda b, j: (0, q_blk(b, j))),
            k_spec(BLK, edge(-1)), k_spec(QB, pair), k_spec(BLK, edge(QB // BLK)), k_spec(CTX, ctx_blk),
            v_spec(BLK, edge(-1)), v_spec(QB, pair), v_spec(BLK, edge(QB // BLK)), v_spec(CTX, ctx_blk),
            pl.BlockSpec((1, ATT_HEADS * BLK), lambda b, j: (0, 0)),
        ],
        out_specs=pl.BlockSpec((QB, ATT_DIM), lambda b, j: (q_blk(b, j), 0)),
        out_shape=jax.ShapeDtypeStruct((T_ALL, ATT_DIM), BF16),
        compiler_params=_params(2),
        name="window_attn",
    )(qt, k, k, k, k, vt, vt, vt, vt, sink)


CONV_S = 4
CONV_ROWS = 8 * CONV_S
N_SLAB = CONV_DIM // 128


def _even_out_kernel(x_ref, m_ref, u_ref, up_ref, un_ref, att_ref, cw_ref, cb_ref,
                     lg_ref, lb_ref, wa_ref, wb_ref, o_ref, ext_ref, a_ref, cva_ref, cvb_ref, *, tm, tiles_per_seq):
    t = pl.program_id(0)
    first = (t % tiles_per_seq) == 0
    last = (t % tiles_per_seq) == tiles_per_seq - 1
    slabs = [slice(s * 128, (s + 1) * 128) for s in range(N_SLAB)]
    for s, ls in enumerate(slabs):
        ext_ref[s, 0:HALO, :] = jnp.where(first, 0.0, up_ref[:, ls])
        ext_ref[s, HALO:HALO + tm, :] = u_ref[:, ls]
        ext_ref[s, HALO + tm:, :] = jnp.where(last, 0.0, un_ref[:, ls])

    def conv_block(blk, cv_ref):
        base = blk * CONV_ROWS
        for s, ls in enumerate(slabs):
            acc = [jnp.broadcast_to(cb_ref[:, ls], (8, 128)) for _ in range(CONV_S)]
            for o in range(CONV_W + CONV_S - 1):
                v = ext_ref[s, pl.ds(base + (HALO - CONV_PAD) + o, 8, stride=CONV_S), :]
                for j in range(CONV_S):
                    k = o - j
                    if 0 <= k < CONV_W:
                        acc[j] = acc[j] + cw_ref[k:k + 1, ls] * v
            for j in range(CONV_S):
                cv_ref[s * CONV_S + j] = acc[j]

    def norm_block(blk, cv_ref):
        base = blk * CONV_ROWS
        for j in range(CONV_S):
            row = [cv_ref[s * CONV_S + j] for s in range(N_SLAB)]
            mu = jnp.sum(sum(row), axis=-1, keepdims=True) * (1.0 / CONV_DIM)
            cen = [r - mu for r in row]
            var = jnp.sum(sum(c * c for c in cen), axis=-1, keepdims=True) * (1.0 / CONV_DIM)
            rs = lax.rsqrt(var + EPS)
            for s, ls in enumerate(slabs):
                y = cen[s] * rs * lg_ref[:, ls] + lb_ref[:, ls]
                a_ref[s, pl.ds(base + j, 8, stride=CONV_S), :] = _silu(y)

    n_blk = tm // CONV_ROWS
    conv_block(0, cva_ref)

    def body(i, carry):
        conv_block(2 * i + 1, cvb_ref)
        norm_block(2 * i, cva_ref)
        conv_block(jnp.minimum(2 * i + 2, n_blk - 1), cva_ref)
        norm_block(2 * i + 1, cvb_ref)
        return carry

    lax.fori_loop(0, n_blk // 2, body, 0)
    a = jnp.concatenate([a_ref[s] for s in range(N_SLAB)], axis=1).astype(BF16)
    y = _dot(a, wa_ref[...]) + _dot(att_ref[...], wb_ref[...])
    o_ref[...] = x_ref[...] + m_ref[:, 2 * D:3 * D] * y


def _even_out(x, mods, u, att, conv_w, conv_b, ln_g, ln_b, wa, wb, *, tm, row0, n_tiles,
              tiles_per_seq, mod_row):
    blk0 = row0 // tm
    hb = tm // HALO
    n_halo = T_ALL // HALO
    row_map = lambda i: (blk0 + i, 0)
    in_specs = [
        pl.BlockSpec((tm, D), row_map),
        pl.BlockSpec((None, 1, 3 * D), lambda i: (mod_row(i), 0, 1)),
        pl.BlockSpec((tm, CONV_DIM), row_map),
        pl.BlockSpec((HALO, CONV_DIM), lambda i: (jnp.maximum((blk0 + i) * hb - 1, 0), 0)),
        pl.BlockSpec((HALO, CONV_DIM), lambda i: (jnp.minimum((blk0 + i + 1) * hb, n_halo - 1), 0)),
        pl.BlockSpec((tm, ATT_DIM), row_map),
        _resident((32, CONV_DIM)),
        _resident((1, CONV_DIM)),
        _resident((1, CONV_DIM)),
        _resident((1, CONV_DIM)),
        _resident((CONV_DIM, D)),
        _resident((ATT_DIM, D)),
    ]
    return pl.pallas_call(
        functools.partial(_even_out_kernel, tm=tm, tiles_per_seq=tiles_per_seq),
        grid=(n_tiles,),
        in_specs=in_specs,
        out_specs=pl.BlockSpec((tm, D), row_map),
        out_shape=jax.ShapeDtypeStruct((T_ALL, D), F32),
        scratch_shapes=[pltpu.VMEM((N_SLAB, tm + 2 * HALO, 128), F32),
                        pltpu.VMEM((N_SLAB, tm, 128), F32),
                        pltpu.VMEM((N_SLAB * CONV_S, 8, 128), F32),
                        pltpu.VMEM((N_SLAB * CONV_S, 8, 128), F32)],
        input_output_aliases={0: 0},
        compiler_params=_params(1),
        name="even_out" if tiles_per_seq > 1 else "even_out_ctx",
    )(x, mods, u, u, u, att, conv_w, conv_b, ln_g, ln_b, wa, wb)


N_GATE = 4 * ML_HEADS
N_CHAIN = 2 * ML_HEADS
L = ML_CHUNK


def _log_sigmoid(x):
    return jnp.minimum(x, 0.0) - jnp.log(1.0 + jnp.exp(-jnp.abs(x)))


def _lane_scan(x, lane, fwd_rows, combine, fill):
    sh = 1
    while sh < L:
        pre = jnp.where(lane >= sh, pltpu.roll(x, sh, 1), fill)
        suf = jnp.where(lane < L - sh, pltpu.roll(x, L - sh, 1), fill)
        x = combine(x, jnp.where(fwd_rows, pre, suf))
        sh *= 2
    return x


def _odd_in_kernel(x_ref, m_ref, g_ref, wqt_ref, wk_ref, wvt_ref, bgt_ref,
                   qt_ref, k_ref, vt_ref, a_ref, amax_ref, b_ref):
    x = x_ref[...]
    hf = _rms_mod(x, g_ref[...], m_ref[:, 0:D], m_ref[:, D:2 * D])
    h = hf.astype(BF16)
    h_lo = (hf - h.astype(F32)).astype(BF16)
    qg = _dot_nt(wqt_ref[...], h)
    qt_ref[...] = qg[0:ML_DIM].astype(BF16)
    g = (qg[ML_DIM:ML_DIM + N_GATE] + qg[ML_DIM + N_GATE:]
         + _dot_nt(wqt_ref[ML_DIM:ML_DIM + N_GATE, :], h_lo) + bgt_ref[...])
    k_ref[...] = (_dot(h, wk_ref[...]) * (ML_DH ** -0.5)).astype(BF16)
    vt_ref[...] = _dot_nt(wvt_ref[...], h).astype(BF16)
    li = g[0:N_CHAIN]
    lf = _log_sigmoid(g[N_CHAIN:N_GATE])
    fwd_rows = lax.broadcasted_iota(jnp.int32, (N_CHAIN, L), 0) < ML_HEADS
    lane = lax.broadcasted_iota(jnp.int32, (N_CHAIN, L), 1)
    for ch in range(TM // L):
        cs = slice(ch * L, (ch + 1) * L)
        b = _lane_scan(lf[:, cs], lane, fwd_rows, jnp.add, 0.0)
        a = li[:, cs] - b
        a_ref[:, cs] = a
        amax_ref[:, cs] = _lane_scan(a, lane, fwd_rows, jnp.maximum, NEG)
        b_ref[:, cs] = b


def _odd_in(x, mods, norm_g, w_qg_t, w_k, w_v_t, b_gate_t):
    row = lambda i: (i, 0)
    col = lambda i: (0, i)
    return pl.pallas_call(
        _odd_in_kernel,
        grid=(N_ALL_TILES,),
        in_specs=[
            pl.BlockSpec((TM, D), row),
            _mod_spec(1),
            _resident((1, D)),
            _resident((ML_DIM + 2 * N_GATE, D)),
            _resident((D, ML_DIM)),
            _resident((ML_DIM, D)),
            _resident((N_GATE, 1)),
        ],
        out_specs=[
            pl.BlockSpec((ML_DIM, TM), col),
            pl.BlockSpec((TM, ML_DIM), row),
            pl.BlockSpec((ML_DIM, TM), col),
            pl.BlockSpec((N_CHAIN, TM), col),
            pl.BlockSpec((N_CHAIN, TM), col),
            pl.BlockSpec((N_CHAIN, TM), col),
        ],
        out_shape=[
            jax.ShapeDtypeStruct((ML_DIM, T_ALL), BF16),
            jax.ShapeDtypeStruct((T_ALL, ML_DIM), BF16),
            jax.ShapeDtypeStruct((ML_DIM, T_ALL), BF16),
            jax.ShapeDtypeStruct((N_CHAIN, T_ALL), F32),
            jax.ShapeDtypeStruct((N_CHAIN, T_ALL), F32),
            jax.ShapeDtypeStruct((N_CHAIN, T_ALL), F32),
        ],
        compiler_params=_params(1),
        name="odd_in",
    )(x, mods, norm_g.reshape(1, D), w_qg_t, w_k, w_v_t, b_gate_t)


SUBS = 2
ML_BLK = SUBS * ML_CHUNK
N_CTX_CHUNKS = CTX // ML_BLK
N_LAT_CHUNKS = SEQ // ML_BLK
N_STEPS = N_CTX_CHUNKS + N_LAT_CHUNKS
N_AUG = 16


def _mlstm_kernel(qtf_ref, kf_ref, vtf_ref, qtb_ref, kb_ref, vtb_ref,
                  af_ref, amaxf_ref, bf_ref, ab_ref, amaxb_ref, bb_ref, hf_ref, hb_ref, c_ref, m_ref):
    t = pl.program_id(1)

    @pl.when(t == 0)
    def _():
        c_ref[...] = jnp.zeros_like(c_ref)
        m_ref[...] = jnp.zeros_like(m_ref)

    for sub in range(SUBS):
        _mlstm_chunk(slice(sub * L, (sub + 1) * L), slice((SUBS - 1 - sub) * L, (SUBS - sub) * L),
                     qtf_ref, kf_ref, vtf_ref, qtb_ref, kb_ref, vtb_ref,
                     af_ref, amaxf_ref, bf_ref, ab_ref, amaxb_ref, bb_ref, hf_ref, hb_ref, c_ref, m_ref)


def _mlstm_chunk(fs, bs, qtf_ref, kf_ref, vtf_ref, qtb_ref, kb_ref, vtb_ref,
                 af_ref, amaxf_ref, bf_ref, ab_ref, amaxb_ref, bb_ref, hf_ref, hb_ref, c_ref, m_ref):
    fwd_rows = lax.broadcasted_iota(jnp.int32, (N_CHAIN, L), 0) < ML_HEADS
    a = jnp.where(fwd_rows, af_ref[:, fs], ab_ref[:, bs])
    amax = jnp.where(fwd_rows, amaxf_ref[:, fs], amaxb_ref[:, bs])
    b = jnp.where(fwd_rows, bf_ref[:, fs], bb_ref[:, bs])
    m_old = m_ref[:, 0:1]
    big = jnp.maximum(m_old, jnp.max(amax, axis=1, keepdims=True))
    decay = jnp.exp(m_old - big)
    e = jnp.exp(a - big)
    mm = jnp.maximum(amax, m_old)
    w_inter = jnp.exp(m_old - mm)
    floor = jnp.exp(-(b + mm))
    m_ref[...] = jnp.broadcast_to(jnp.min(b, axis=1, keepdims=True) + big, (N_CHAIN, 128))
    kq = lax.broadcasted_iota(jnp.int32, (L, L), 0)
    qq = lax.broadcasted_iota(jnp.int32, (L, L), 1)
    a_col = _dot_nt((kq == qq).astype(F32), a, precision=lax.Precision.HIGHEST)

    for d, (qt_ref, k_ref, vt_ref, h_ref, ts) in enumerate(
            ((qtf_ref, kf_ref, vtf_ref, hf_ref, fs), (qtb_ref, kb_ref, vtb_ref, hb_ref, bs))):
        visible = (kq >= qq) if d == 1 else (kq <= qq)
        for hd in range(ML_HEADS):
            c = d * ML_HEADS + hd
            cs = slice(hd * ML_DH, (hd + 1) * ML_DH)
            qt = qt_ref[cs, ts]
            k = k_ref[ts, cs]
            vt = vt_ref[cs, ts]
            p = jnp.where(visible, jnp.exp(a_col[:, c:c + 1] - mm[c:c + 1, :]), 0.0)
            c_old = c_ref[c]
            kcq = _dot(jnp.concatenate([k, c_old.astype(BF16)], axis=0), qt)
            st = kcq[0:L] * p
            cq = kcq[L:]
            wi = w_inter[c:c + 1, :]
            num = _dot(vt, st.astype(BF16)) + wi * cq[0:ML_DH]
            den = jnp.sum(st, axis=0, keepdims=True) + wi * cq[ML_DH:ML_DH + 1]
            ht = num * (1.0 / jnp.maximum(jnp.abs(den), floor[c:c + 1, :]))
            h_ref[ts, cs] = ht.T.astype(BF16)
            er = e[c:c + 1, :]
            vte = jnp.concatenate([vt.astype(F32) * er, jnp.broadcast_to(er, (N_AUG, L))], axis=0)
            c_ref[c] = decay[c:c + 1, :] * c_old + _dot(vte.astype(BF16), k)


def _mlstm(qt, k, vt, a, amax, b):
    lat_chunks = T_LAT // ML_BLK

    def fwd_in(b, t):
        return jnp.where(t < N_CTX_CHUNKS, lat_chunks + b * N_CTX_CHUNKS + t,
                         b * N_LAT_CHUNKS + (t - N_CTX_CHUNKS))

    def bwd_in(b, t):
        return jnp.where(t < N_CTX_CHUNKS, lat_chunks + b * N_CTX_CHUNKS + (N_CTX_CHUNKS - 1 - t),
                         b * N_LAT_CHUNKS + (N_STEPS - 1 - t))

    def fwd_out(b, t):
        return b * N_LAT_CHUNKS + jnp.maximum(t - N_CTX_CHUNKS, 0)

    def bwd_out(b, t):
        return b * N_LAT_CHUNKS + (N_STEPS - 1 - jnp.maximum(t, N_CTX_CHUNKS))

    rows = lambda f: pl.BlockSpec((ML_BLK, ML_DIM), lambda b, t: (f(b, t), 0))
    cols = lambda f: pl.BlockSpec((ML_DIM, ML_BLK), lambda b, t: (0, f(b, t)))
    gate = lambda f: pl.BlockSpec((N_CHAIN, ML_BLK), lambda b, t: (0, f(b, t)))
    return pl.pallas_call(
        _mlstm_kernel,
        grid=(BATCH, N_STEPS),
        in_specs=[cols(fwd_in), rows(fwd_in), cols(fwd_in),
                  cols(bwd_in), rows(bwd_in), cols(bwd_in),
                  gate(fwd_in), gate(fwd_in), gate(fwd_in), gate(bwd_in), gate(bwd_in), gate(bwd_in)],
        out_specs=[rows(fwd_out), rows(bwd_out)],
        out_shape=[jax.ShapeDtypeStruct((T_LAT, ML_DIM), BF16),
                   jax.ShapeDtypeStruct((T_LAT, ML_DIM), BF16)],
        scratch_shapes=[pltpu.VMEM((N_CHAIN, ML_DH + N_AUG, ML_DH), F32),
                        pltpu.VMEM((N_CHAIN, 128), F32)],
        compiler_params=_params(2),
        name="mlstm",
    )(qt, k, vt, qt, k, vt, a, amax, b, a, amax, b)


def _odd_out_kernel(x_ref, m_ref, g_ref, hf_ref, hb_ref, wo_ref, ng_ref, wout_ref, o_ref):
    x = x_ref[...]
    h = _rms_mod(x, g_ref[...], m_ref[:, 0:D], m_ref[:, D:2 * D]).astype(BF16)
    o = _sigmoid(_dot(h, wo_ref[...]))
    hs = hf_ref[...].astype(F32) + hb_ref[...].astype(F32)
    parts = []
    for hd in range(ML_HEADS):
        p = hs[:, hd * ML_DH:(hd + 1) * ML_DH]
        parts.append(p * lax.rsqrt(jnp.mean(p * p, axis=-1, keepdims=True) + EPS))
    hn = jnp.concatenate(parts, axis=-1) * ng_ref[...]
    y = _dot((o * hn).astype(BF16), wout_ref[...])
    o_ref[...] = x + m_ref[:, 2 * D:3 * D] * y


def _odd_out(x, mods, norm_g, hf, hb, w_o, head_g, w_out):
    row = lambda i: (i, 0)
    return pl.pallas_call(
        _odd_out_kernel,
        grid=(N_LAT_TILES,),
        in_specs=[
            pl.BlockSpec((TM, D), row),
            _mod_spec(1),
            _resident((1, D)),
            pl.BlockSpec((TM, ML_DIM), row),
            pl.BlockSpec((TM, ML_DIM), row),
            _resident((D, ML_DIM)),
            _resident((1, ML_DIM)),
            _resident((ML_DIM, D)),
        ],
        out_specs=pl.BlockSpec((TM, D), row),
        out_shape=jax.ShapeDtypeStruct((T_LAT, D), F32),
        compiler_params=_params(1),
        name="odd_out",
    )(x, mods, norm_g.reshape(1, D), hf, hb, w_o, head_g.reshape(1, ML_DIM), w_out)


def kernel(x, c, ctx, c_ctx, mod_w, mod_b, ffn1_norm, ffn1_w_gu, ffn1_w_d, mix_norm, ffn2_norm,
           ffn2_w_gu, ffn2_w_d, ev_w_in, ev_conv_w, ev_conv_b, ev_conv_ln_g, ev_conv_ln_b, ev_sink,
           ev_w_out, od_w_in, od_b_gate, od_norm_g, od_w_out, final_norm):
    assert DEPTH == 2 and x.shape == (BATCH, SEQ, D) and ctx.shape == (BATCH, CTX, D)
    cond = jnp.zeros((MOD_ROWS, D), F32).at[:BATCH].set(c).at[BATCH].set(c_ctx)
    mods = _ada_mods(cond, mod_w, mod_b).reshape(DEPTH, MOD_ROWS, 1, N_MOD * D)
    bf = lambda w: w.astype(BF16)

    m0 = mods[0]
    xs = _ffn(x.reshape(T_LAT, D), m0, 0, ffn1_norm[0], ffn1_w_gu, ffn1_w_d, 0, N_ALL_TILES,
              x_ctx=ctx.reshape(T_CTX, D))
    w_in = ev_w_in[0]
    rope, rope_t = _rope_tables()
    u, qt, kk, vt = _even_in(xs, m0, mix_norm[0], bf(w_in[:, :Q_OFF]), bf(w_in[:, Q_OFF:K_OFF].T),
                             bf(w_in[:, K_OFF:V_OFF]), bf(w_in[:, V_OFF:].T), rope, rope_t)
    sink = jnp.repeat(ev_sink[0].astype(F32), BLK).reshape(1, ATT_HEADS * BLK)
    att = _attention(qt, kk, vt, sink)
    conv_w = jnp.concatenate([ev_conv_w[0], jnp.zeros((1, CONV_DIM), F32)], axis=0)
    ev_args = (conv_w, ev_conv_b[0].reshape(1, -1), ev_conv_ln_g[0].reshape(1, -1),
               ev_conv_ln_b[0].reshape(1, -1), bf(ev_w_out[0][:CONV_DIM]), bf(ev_w_out[0][CONV_DIM:]))
    xs = _even_out(xs, m0, u, att, *ev_args, tm=TM, row0=0, n_tiles=N_LAT_TILES,
                   tiles_per_seq=SEQ // TM, mod_row=lambda i: i // (SEQ // TM))
    xs = _even_out(xs, m0, u, att, *ev_args, tm=CTX, row0=T_LAT, n_tiles=BATCH,
                   tiles_per_seq=1, mod_row=lambda i: BATCH)
    xs = _ffn(xs, m0, 2, ffn2_norm[0], ffn2_w_gu, ffn2_w_d, 0, N_ALL_TILES)

    m1 = mods[1]
    xs = _ffn(xs, m1, 0, ffn1_norm[1], ffn1_w_gu, ffn1_w_d, 1, N_ALL_TILES)
    w_in = od_w_in[0]
    perm = np.concatenate([np.arange(0, 4), np.arange(8, 12), np.arange(4, 8), np.arange(12, 16)])
    w_gate_t = w_in[:, 4 * ML_DIM:].T[perm]
    b_gate_t = od_b_gate[0][perm].reshape(N_GATE, 1)
    w_gate_hi = bf(w_gate_t)
    w_gate_lo = bf(w_gate_t - w_gate_hi.astype(F32))
    w_qg_t = jnp.concatenate([bf(w_in[:, :ML_DIM].T), w_gate_hi, w_gate_lo], axis=0)
    qt, km, vt, ga, gamax, gb = _odd_in(xs, m1, mix_norm[1], w_qg_t, bf(w_in[:, ML_DIM:2 * ML_DIM]),
                                        bf(w_in[:, 2 * ML_DIM:3 * ML_DIM].T), b_gate_t)
    hf, hb = _mlstm(qt, km, vt, ga, gamax, gb)
    xl = _odd_out(xs, m1, mix_norm[1], hf, hb, bf(w_in[:, 3 * ML_DIM:4 * ML_DIM]), od_norm_g[0],
                  bf(od_w_out[0]))
    out = _ffn(xl, m1, 2, ffn2_norm[1], ffn2_w_gu, ffn2_w_d, 1, N_LAT_TILES,
               final_g=final_norm)
    return out.reshape(BATCH, SEQ, D)
```

```python
import functools

import jax
import jax.numpy as jnp
import numpy as np
from jax import lax
from jax.experimental import pallas as pl
from jax.experimental.pallas import tpu as pltpu

D = 1024
BATCH = 4
SEQ = 4096
DEPTH = 2
GRID_W = 64
CTX = 256
N_MOD = 9
D_FF = 2816
EPS = 1e-6
CONV_DIM = 512
CONV_W = 31
CONV_PAD = 15
HEAD_DIM = 64
ATT_HEADS = 8
KV_HEADS = 2
ATT_DIM = 512
KV_DIM = 128
WINDOW = 128
BLK = 128
ROPE_BASE = 10000.0
ROPE_FREQS = 16
Q_OFF = 2 * CONV_DIM
K_OFF = Q_OFF + ATT_DIM
V_OFF = K_OFF + KV_DIM
EVEN_IN = V_OFF + KV_DIM
ML_HEADS = 4
ML_DH = 256
ML_DIM = 1024
ML_CHUNK = 128

T_LAT = BATCH * SEQ
T_CTX = BATCH * CTX
T_ALL = T_LAT + T_CTX
TM = 1024
N_LAT_TILES = T_LAT // TM
N_ALL_TILES = T_ALL // TM
MOD_ROWS = 8
HALO = 16
NEG = -1e30
VMEM_LIMIT = 56 * 1024 * 1024
VMEM_LIMIT_FFN = 60 * 1024 * 1024

F32 = jnp.float32
BF16 = jnp.bfloat16


def _sigmoid(x):
    return 1.0 / (1.0 + jnp.exp(-x))


def _silu(x):
    return x * _sigmoid(x)


def _dot(a, b, precision=None):
    return jnp.dot(a, b, preferred_element_type=F32, precision=precision)


def _dot_nt(a, b, precision=None):
    return lax.dot_general(a, b, (((1,), (1,)), ((), ())),
                           preferred_element_type=F32, precision=precision)


def _dot_tn(a, b):
    return lax.dot_general(a, b, (((0,), (0,)), ((), ())), preferred_element_type=F32)


def _rms_mod(x, g, shift, scale):
    y = x * lax.rsqrt(jnp.mean(x * x, axis=-1, keepdims=True) + EPS)
    return (y * g) * (1.0 + scale) + shift


def _resident(shape):
    nd = len(shape)
    return pl.BlockSpec(shape, lambda *_: (0,) * nd, pipeline_mode=pl.Buffered(1))


def _params(n_axes=1, vmem_limit=VMEM_LIMIT):
    return pltpu.CompilerParams(dimension_semantics=("arbitrary",) * n_axes,
                                vmem_limit_bytes=vmem_limit)


def _mod_kernel(c_ref, w_ref, b_ref, o_ref):
    s = _silu(c_ref[...]).astype(BF16)
    o_ref[...] = _dot(s, w_ref[...].astype(BF16)) + b_ref[...]


def _ada_mods(cond, mod_w, mod_b):
    tn = 1024
    n = N_MOD * D
    return pl.pallas_call(
        _mod_kernel,
        grid=(DEPTH, n // tn),
        in_specs=[
            pl.BlockSpec((MOD_ROWS, D), lambda l, j: (0, 0)),
            pl.BlockSpec((None, D, tn), lambda l, j: (l, 0, j)),
            pl.BlockSpec((None, 1, tn), lambda l, j: (l, 0, j)),
        ],
        out_specs=pl.BlockSpec((None, MOD_ROWS, tn), lambda l, j: (l, 0, j)),
        out_shape=jax.ShapeDtypeStruct((DEPTH, MOD_ROWS, n), F32),
        compiler_params=_params(2),
        name="ada_mods",
    )(cond, mod_w, mod_b.reshape(DEPTH, 1, n))


def _mod_spec(k):
    return pl.BlockSpec((None, 1, 3 * D), lambda i: (i // (SEQ // TM), 0, k))


FF_CHUNKS = tuple((c, min(c + 512, D_FF)) for c in range(0, D_FF, 512))


STAGE = 256
N_STAGE = 3


def _stage_weights(src_ref, dst_ref, stage_ref, sem_ref, chunks):
    def copy(n):
        slot = n % N_STAGE
        return pltpu.make_async_copy(src_ref.at[chunks[n][0]], stage_ref.at[slot], sem_ref.at[slot])

    for n in range(min(N_STAGE, len(chunks))):
        copy(n).start()
    for n in range(len(chunks)):
        copy(n).wait()
        dst_ref[chunks[n][1]] = stage_ref[n % N_STAGE].astype(BF16)
        if n + N_STAGE < len(chunks):
            copy(n + N_STAGE).start()


def _ffn_kernel(x_ref, *rest, layer, split, final):
    if split:
        xc_ref, *rest = rest
    m_ref, g_ref, wgu_hbm, wd_hbm, *rest = rest
    if final:
        fn_ref, *rest = rest
    o_ref, wgu_ref, wd_ref, sgu_ref, sd_ref, sem_gu, sem_d = rest

    @pl.when(pl.program_id(0) == 0)
    def _():
        _stage_weights(wgu_hbm, wgu_ref, sgu_ref, sem_gu,
                       [((layer, slice(None), pl.ds(c, STAGE)), (slice(None), pl.ds(c, STAGE)))
                        for c in range(0, 2 * D_FF, STAGE)])
        _stage_weights(wd_hbm, wd_ref, sd_ref, sem_d,
                       [((layer, pl.ds(r, STAGE), slice(None)), (pl.ds(r, STAGE), slice(None)))
                        for r in range(0, D_FF, STAGE)])

    x = x_ref[...]
    if split:
        x = jnp.where(pl.program_id(0) < N_LAT_TILES, x, xc_ref[...])
    shift = m_ref[:, 0:D]
    scale = m_ref[:, D:2 * D]
    gate = m_ref[:, 2 * D:3 * D]
    h = _rms_mod(x, g_ref[...], shift, scale).astype(BF16)
    acc = None
    for c0, c1 in FF_CHUNKS:
        hg = _dot(h, wgu_ref[:, c0:c1])
        hu = _dot(h, wgu_ref[:, D_FF + c0:D_FF + c1])
        a = (_silu(hg) * hu).astype(BF16)
        p = _dot(a, wd_ref[c0:c1, :])
        acc = p if acc is None else acc + p
    y = x + (0.5 * gate) * acc
    if final:
        y = (y * lax.rsqrt(jnp.mean(y * y, axis=-1, keepdims=True) + EPS)) * fn_ref[...]
    o_ref[...] = y


def _ffn(x, mods, k, norm_g, w_gu, w_d, layer, n_tiles, final_g=None, x_ctx=None):
    final = final_g is not None
    split = x_ctx is not None
    if split:
        in_specs = [pl.BlockSpec((TM, D), lambda i: (jnp.minimum(i, N_LAT_TILES - 1), 0)),
                    _resident((TM, D))]
        args = [x, x_ctx]
    else:
        in_specs = [pl.BlockSpec((TM, D), lambda i: (i, 0))]
        args = [x]
    in_specs += [
        _mod_spec(k),
        _resident((1, D)),
        pl.BlockSpec(memory_space=pl.ANY),
        pl.BlockSpec(memory_space=pl.ANY),
    ]
    args += [mods, norm_g.reshape(1, D), w_gu, w_d]
    if final:
        in_specs.append(_resident((1, D)))
        args.append(final_g.reshape(1, D))
    return pl.pallas_call(
        functools.partial(_ffn_kernel, layer=layer, split=split, final=final),
        grid=(n_tiles,),
        in_specs=in_specs,
        out_specs=pl.BlockSpec((TM, D), lambda i: (i, 0)),
        out_shape=jax.ShapeDtypeStruct((n_tiles * TM, D), F32),
        scratch_shapes=[pltpu.VMEM((D, 2 * D_FF), BF16),
                        pltpu.VMEM((D_FF, D), BF16),
                        pltpu.VMEM((N_STAGE, D, STAGE), F32),
                        pltpu.VMEM((N_STAGE, STAGE, D), F32),
                        pltpu.SemaphoreType.DMA((N_STAGE,)),
                        pltpu.SemaphoreType.DMA((N_STAGE,))],
        compiler_params=_params(1, VMEM_LIMIT_FFN),
        name="ffn_final" if final else ("ffn_first" if split else "ffn"),
    )(*args)


def _even_in_kernel(x_ref, m_ref, g_ref, wt_ref,
                    c_ref, s1_ref, s2_ref, ct_ref, s1t_ref, s2t_ref,
                    u_ref, qt_ref, k_ref, vt_ref):
    x = x_ref[...]
    h = _rms_mod(x, g_ref[...], m_ref[:, 0:D], m_ref[:, D:2 * D]).astype(BF16)
    vg = _dot_nt(h, wt_ref[0:Q_OFF, :])
    u_ref[...] = vg[:, 0:CONV_DIM] * _sigmoid(vg[:, CONV_DIM:Q_OFF])
    qf = _dot_nt(wt_ref[Q_OFF:K_OFF, :], h)
    ct, s1t, s2t = ct_ref[...], s1t_ref[...], s2t_ref[...]
    for hd in range(ATT_HEADS):
        xh = qf[hd * HEAD_DIM:(hd + 1) * HEAD_DIM]
        up = jnp.concatenate([xh[ROPE_FREQS:], xh[:ROPE_FREQS]], axis=0)
        dn = jnp.concatenate([xh[HEAD_DIM - ROPE_FREQS:], xh[:HEAD_DIM - ROPE_FREQS]], axis=0)
        r = xh * ct + up * s1t + dn * s2t
        qt_ref[hd * HEAD_DIM:(hd + 1) * HEAD_DIM, :] = (r * (HEAD_DIM ** -0.5)).astype(BF16)
    kf = _dot_nt(h, wt_ref[K_OFF:V_OFF, :])
    kr = kf * c_ref[...] + pltpu.roll(kf, 128 - ROPE_FREQS, 1) * s1_ref[...] + pltpu.roll(kf, ROPE_FREQS, 1) * s2_ref[...]
    k_ref[...] = kr.astype(BF16)
    vt_ref[...] = _dot_nt(wt_ref[V_OFF:EVEN_IN, :], h).astype(BF16)


def _even_in(x, mods, norm_g, w_t, rope, rope_t):
    pos = lambda i: jnp.where(i < N_LAT_TILES, i % (SEQ // TM), SEQ // TM)
    rope_spec = lambda: pl.BlockSpec((TM, KV_DIM), lambda i: (pos(i), 0))
    rope_t_spec = lambda: pl.BlockSpec((HEAD_DIM, TM), lambda i: (0, pos(i)))
    row = lambda i: (i, 0)
    col = lambda i: (0, i)
    return pl.pallas_call(
        _even_in_kernel,
        grid=(N_ALL_TILES,),
        in_specs=[
            pl.BlockSpec((TM, D), row),
            _mod_spec(1),
            _resident((1, D)),
            _resident((EVEN_IN, D)),
            rope_spec(), rope_spec(), rope_spec(),
            rope_t_spec(), rope_t_spec(), rope_t_spec(),
        ],
        out_specs=[
            pl.BlockSpec((TM, CONV_DIM), row),
            pl.BlockSpec((ATT_DIM, TM), col),
            pl.BlockSpec((TM, KV_DIM), row),
            pl.BlockSpec((KV_DIM, TM), col),
        ],
        out_shape=[
            jax.ShapeDtypeStruct((T_ALL, CONV_DIM), F32),
            jax.ShapeDtypeStruct((ATT_DIM, T_ALL), BF16),
            jax.ShapeDtypeStruct((T_ALL, KV_DIM), BF16),
            jax.ShapeDtypeStruct((KV_DIM, T_ALL), BF16),
        ],
        compiler_params=_params(1),
        name="even_in",
    )(x, mods, norm_g.reshape(1, D), w_t, *rope, *rope_t)


def _rope_tables():
    pos = jnp.arange(SEQ, dtype=jnp.int32)
    row = (pos // GRID_W).astype(F32)[:, None]
    col = (pos % GRID_W).astype(F32)[:, None]
    inv = ROPE_BASE ** (-jnp.arange(ROPE_FREQS, dtype=F32) / ROPE_FREQS)
    d = np.arange(128) % HEAD_DIM
    ang = jnp.where(jnp.asarray(d // 32 == 0), row, col) * inv[d % ROPE_FREQS]
    first = jnp.asarray(((d % 32) // ROPE_FREQS) == 0)
    c = jnp.cos(ang)
    s = jnp.sin(ang)
    s1 = jnp.where(first, -s, 0.0)
    s2 = jnp.where(first, 0.0, s)
    pad = lambda t, v: jnp.concatenate([t, jnp.full((TM, 128), v, F32)], axis=0)
    tok = (pad(c, 1.0), pad(s1, 0.0), pad(s2, 0.0))
    return tok, tuple(t[:, :HEAD_DIM].T for t in tok)


NB = SEQ // BLK
QB = 2 * BLK
GROUP = ATT_HEADS // KV_HEADS
assert CTX == QB


def _attn_kernel(qt_ref, kp_ref, kc_ref, kn_ref, kx_ref, vp_ref, vc_ref, vn_ref, vx_ref,
                 sink_ref, o_ref):
    for sub in range(QB // BLK):
        ks = [kp_ref[...], kc_ref[0:BLK], kc_ref[BLK:2 * BLK], kn_ref[...]][sub:sub + 3]
        vs = [vp_ref[...], vc_ref[:, 0:BLK], vc_ref[:, BLK:2 * BLK], vn_ref[...]][sub:sub + 3]
        o_ref[sub * BLK:(sub + 1) * BLK, :] = _attn_block(
            (QB // BLK) * pl.program_id(1) + sub, qt_ref[:, sub * BLK:(sub + 1) * BLK],
            ks + [kx_ref[...]], vs + [vx_ref[...]], sink_ref)


def _attn_block(i, qt, ks, vs, sink_ref):
    kk = lax.broadcasted_iota(jnp.int32, (3 * BLK, BLK), 0)
    qq = lax.broadcasted_iota(jnp.int32, (3 * BLK, BLK), 1)
    rel = kk - BLK - qq
    kpos = (i - 1) * BLK + kk
    ok = (jnp.abs(rel) <= WINDOW) & (kpos >= 0) & (kpos < SEQ) & (i < NB)
    bias1 = jnp.where(ok, 0.0, NEG).astype(F32)
    bias = jnp.concatenate([bias1] * GROUP, axis=1)
    keys = jnp.concatenate(ks, axis=0)
    vt = jnp.concatenate(vs, axis=1)
    zero = jnp.zeros((HEAD_DIM, BLK), BF16)
    outs = []
    for g in range(KV_HEADS):
        cols = []
        for hh in range(GROUP):
            hd = g * GROUP + hh
            qh = qt[hd * HEAD_DIM:(hd + 1) * HEAD_DIM]
            cols.append(jnp.concatenate([qh, zero] if g == 0 else [zero, qh], axis=0))
        s = _dot(keys, jnp.concatenate(cols, axis=1))
        s_b = s[0:3 * BLK] + bias
        s_x = s[3 * BLK:]
        sink = sink_ref[:, g * GROUP * BLK:(g + 1) * GROUP * BLK]
        m = jnp.maximum(jnp.maximum(jnp.max(s_b, axis=0, keepdims=True),
                                    jnp.max(s_x, axis=0, keepdims=True)), sink)
        p_b = jnp.exp(s_b - m)
        p_x = jnp.exp(s_x - m)
        den = (jnp.sum(p_b, axis=0, keepdims=True) + jnp.sum(p_x, axis=0, keepdims=True)
               + jnp.exp(sink - m))
        p = jnp.concatenate([p_b, p_x], axis=0).astype(BF16)
        ot = _dot(vt[g * HEAD_DIM:(g + 1) * HEAD_DIM], p) * (1.0 / den)
        outs += [ot[:, hh * BLK:(hh + 1) * BLK] for hh in range(GROUP)]
    return jnp.concatenate(outs, axis=0).T.astype(BF16)


def _attention(qt, k, vt, sink):
    n_pairs = SEQ // QB
    q_blk = lambda b, j: jnp.where(j < n_pairs, b * n_pairs + j, T_LAT // QB + b)
    pair = lambda b, j: b * n_pairs + jnp.minimum(j, n_pairs - 1)
    edge = lambda off: (lambda b, j: b * NB + jnp.clip((QB // BLK) * j + off, 0, NB - 1))
    ctx_blk = lambda b, j: T_LAT // CTX + b
    k_spec = lambda n, f: pl.BlockSpec((n, KV_DIM), lambda b, j: (f(b, j), 0))
    v_spec = lambda n, f: pl.BlockSpec((KV_DIM, n), lambda b, j: (0, f(b, j)))
    return pl.pallas_call(
        _attn_kernel,
        grid=(BATCH, n_pairs + 1),
        in_specs=[
            pl.BlockSpec((ATT_DIM, QB), lambda b, j: (0, q_blk(b, j))),
            k_spec(BLK, edge(-1)), k_spec(QB, pair), k_spec(BLK, edge(QB // BLK)), k_spec(CTX, ctx_blk),
            v_spec(BLK, edge(-1)), v_spec(QB, pair), v_spec(BLK, edge(QB // BLK)), v_spec(CTX, ctx_blk),
            pl.BlockSpec((1, ATT_HEADS * BLK), lambda b, j: (0, 0)),
        ],
        out_specs=pl.BlockSpec((QB, ATT_DIM), lambda b, j: (q_blk(b, j), 0)),
        out_shape=jax.ShapeDtypeStruct((T_ALL, ATT_DIM), BF16),
        compiler_params=_params(2),
        name="window_attn",
    )(qt, k, k, k, k, vt, vt, vt, vt, sink)


CONV_S = 4
CONV_ROWS = 8 * CONV_S
N_SLAB = CONV_DIM // 128


def _even_out_kernel(x_ref, m_ref, u_ref, up_ref, un_ref, att_ref, cw_ref, cb_ref,
                     lg_ref, lb_ref, wa_ref, wb_ref, o_ref, ext_ref, a_ref, cva_ref, cvb_ref, *, tm, tiles_per_seq):
    t = pl.program_id(0)
    first = (t % tiles_per_seq) == 0
    last = (t % tiles_per_seq) == tiles_per_seq - 1
    slabs = [slice(s * 128, (s + 1) * 128) for s in range(N_SLAB)]
    for s, ls in enumerate(slabs):
        ext_ref[s, 0:HALO, :] = jnp.where(first, 0.0, up_ref[:, ls])
        ext_ref[s, HALO:HALO + tm, :] = u_ref[:, ls]
        ext_ref[s, HALO + tm:, :] = jnp.where(last, 0.0, un_ref[:, ls])

    def conv_block(blk, cv_ref):
        base = blk * CONV_ROWS
        for s, ls in enumerate(slabs):
            acc = [jnp.broadcast_to(cb_ref[:, ls], (8, 128)) for _ in range(CONV_S)]
            for o in range(CONV_W + CONV_S - 1):
                v = ext_ref[s, pl.ds(base + (HALO - CONV_PAD) + o, 8, stride=CONV_S), :]
                for j in range(CONV_S):
                    k = o - j
                    if 0 <= k < CONV_W:
                        acc[j] = acc[j] + cw_ref[k:k + 1, ls] * v
            for j in range(CONV_S):
                cv_ref[s * CONV_S + j] = acc[j]

    def norm_block(blk, cv_ref):
        base = blk * CONV_ROWS
        for j in range(CONV_S):
            row = [cv_ref[s * CONV_S + j] for s in range(N_SLAB)]
            mu = jnp.sum(sum(row), axis=-1, keepdims=True) * (1.0 / CONV_DIM)
            cen = [r - mu for r in row]
            var = jnp.sum(sum(c * c for c in cen), axis=-1, keepdims=True) * (1.0 / CONV_DIM)
            rs = lax.rsqrt(var + EPS)
            for s, ls in enumerate(slabs):
                y = cen[s] * rs * lg_ref[:, ls] + lb_ref[:, ls]
                a_ref[s, pl.ds(base + j, 8, stride=CONV_S), :] = _silu(y)

    n_blk = tm // CONV_ROWS
    conv_block(0, cva_ref)

    def body(i, carry):
        conv_block(2 * i + 1, cvb_ref)
        norm_block(2 * i, cva_ref)
        conv_block(jnp.minimum(2 * i + 2, n_blk - 1), cva_ref)
        norm_block(2 * i + 1, cvb_ref)
        return carry

    lax.fori_loop(0, n_blk // 2, body, 0)
    a = jnp.concatenate([a_ref[s] for s in range(N_SLAB)], axis=1).astype(BF16)
    y = _dot(a, wa_ref[...]) + _dot(att_ref[...], wb_ref[...])
    o_ref[...] = x_ref[...] + m_ref[:, 2 * D:3 * D] * y


def _even_out(x, mods, u, att, conv_w, conv_b, ln_g, ln_b, wa, wb, *, tm, row0, n_tiles,
              tiles_per_seq, mod_row):
    blk0 = row0 // tm
    hb = tm // HALO
    n_halo = T_ALL // HALO
    row_map = lambda i: (blk0 + i, 0)
    in_specs = [
        pl.BlockSpec((tm, D), row_map),
        pl.BlockSpec((None, 1, 3 * D), lambda i: (mod_row(i), 0, 1)),
        pl.BlockSpec((tm, CONV_DIM), row_map),
        pl.BlockSpec((HALO, CONV_DIM), lambda i: (jnp.maximum((blk0 + i) * hb - 1, 0), 0)),
        pl.BlockSpec((HALO, CONV_DIM), lambda i: (jnp.minimum((blk0 + i + 1) * hb, n_halo - 1), 0)),
        pl.BlockSpec((tm, ATT_DIM), row_map),
        _resident((32, CONV_DIM)),
        _resident((1, CONV_DIM)),
        _resident((1, CONV_DIM)),
        _resident((1, CONV_DIM)),
        _resident((CONV_DIM, D)),
        _resident((ATT_DIM, D)),
    ]
    return pl.pallas_call(
        functools.partial(_even_out_kernel, tm=tm, tiles_per_seq=tiles_per_seq),
        grid=(n_tiles,),
        in_specs=in_specs,
        out_specs=pl.BlockSpec((tm, D), row_map),
        out_shape=jax.ShapeDtypeStruct((T_ALL, D), F32),
        scratch_shapes=[pltpu.VMEM((N_SLAB, tm + 2 * HALO, 128), F32),
                        pltpu.VMEM((N_SLAB, tm, 128), F32),
                        pltpu.VMEM((N_SLAB * CONV_S, 8, 128), F32),
                        pltpu.VMEM((N_SLAB * CONV_S, 8, 128), F32)],
        input_output_aliases={0: 0},
        compiler_params=_params(1),
        name="even_out" if tiles_per_seq > 1 else "even_out_ctx",
    )(x, mods, u, u, u, att, conv_w, conv_b, ln_g, ln_b, wa, wb)


N_GATE = 4 * ML_HEADS
N_CHAIN = 2 * ML_HEADS
L = ML_CHUNK


def _log_sigmoid(x):
    return jnp.minimum(x, 0.0) - jnp.log(1.0 + jnp.exp(-jnp.abs(x)))


def _lane_scan(x, lane, fwd_rows, combine, fill):
    sh = 1
    while sh < L:
        pre = jnp.where(lane >= sh, pltpu.roll(x, sh, 1), fill)
        suf = jnp.where(lane < L - sh, pltpu.roll(x, L - sh, 1), fill)
        x = combine(x, jnp.where(fwd_rows, pre, suf))
        sh *= 2
    return x


def _odd_in_kernel(x_ref, m_ref, g_ref, wqt_ref, wkt_ref, wvt_ref, bgt_ref,
                   qt_ref, k_ref, vt_ref, a_ref, amax_ref, b_ref):
    x = x_ref[...]
    hf = _rms_mod(x, g_ref[...], m_ref[:, 0:D], m_ref[:, D:2 * D])
    h = hf.astype(BF16)
    h_lo = (hf - h.astype(F32)).astype(BF16)
    qg = _dot_nt(wqt_ref[...], h)
    qt_ref[...] = qg[0:ML_DIM].astype(BF16)
    g = (qg[ML_DIM:ML_DIM + N_GATE] + qg[ML_DIM + N_GATE:]
         + _dot_nt(wqt_ref[ML_DIM:ML_DIM + N_GATE, :], h_lo) + bgt_ref[...])
    k_ref[...] = (_dot_nt(h, wkt_ref[...]) * (ML_DH ** -0.5)).astype(BF16)
    vt_ref[...] = _dot_nt(wvt_ref[...], h).astype(BF16)
    li = g[0:N_CHAIN]
    lf = _log_sigmoid(g[N_CHAIN:N_GATE])
    fwd_rows = lax.broadcasted_iota(jnp.int32, (N_CHAIN, L), 0) < ML_HEADS
    lane = lax.broadcasted_iota(jnp.int32, (N_CHAIN, L), 1)
    for ch in range(TM // L):
        cs = slice(ch * L, (ch + 1) * L)
        b = _lane_scan(lf[:, cs], lane, fwd_rows, jnp.add, 0.0)
        a = li[:, cs] - b
        a_ref[:, cs] = a
        amax_ref[:, cs] = _lane_scan(a, lane, fwd_rows, jnp.maximum, NEG)
        b_ref[:, cs] = b


def _w_in_t_block(j):
    return pl.BlockSpec((ML_DIM, D), lambda i: (j, 0), pipeline_mode=pl.Buffered(1))


def _odd_in(x, mods, norm_g, w_qg_t, w_in_t, b_gate_t):
    row = lambda i: (i, 0)
    col = lambda i: (0, i)
    return pl.pallas_call(
        _odd_in_kernel,
        grid=(N_ALL_TILES,),
        in_specs=[
            pl.BlockSpec((TM, D), row),
            _mod_spec(1),
            _resident((1, D)),
            _resident((ML_DIM + 2 * N_GATE, D)),
            _w_in_t_block(1),
            _w_in_t_block(2),
            _resident((N_GATE, 1)),
        ],
        out_specs=[
            pl.BlockSpec((ML_DIM, TM), col),
            pl.BlockSpec((TM, ML_DIM), row),
            pl.BlockSpec((ML_DIM, TM), col),
            pl.BlockSpec((N_CHAIN, TM), col),
            pl.BlockSpec((N_CHAIN, TM), col),
            pl.BlockSpec((N_CHAIN, TM), col),
        ],
        out_shape=[
            jax.ShapeDtypeStruct((ML_DIM, T_ALL), BF16),
            jax.ShapeDtypeStruct((T_ALL, ML_DIM), BF16),
            jax.ShapeDtypeStruct((ML_DIM, T_ALL), BF16),
            jax.ShapeDtypeStruct((N_CHAIN, T_ALL), F32),
            jax.ShapeDtypeStruct((N_CHAIN, T_ALL), F32),
            jax.ShapeDtypeStruct((N_CHAIN, T_ALL), F32),
        ],
        compiler_params=_params(1),
        name="odd_in",
    )(x, mods, norm_g.reshape(1, D), w_qg_t, w_in_t, w_in_t, b_gate_t)


SUBS = 2
ML_BLK = SUBS * ML_CHUNK
N_CTX_CHUNKS = CTX // ML_BLK
N_LAT_CHUNKS = SEQ // ML_BLK
N_STEPS = N_CTX_CHUNKS + N_LAT_CHUNKS
N_AUG = 16


def _mlstm_kernel(qtf_ref, kf_ref, vtf_ref, qtb_ref, kb_ref, vtb_ref,
                  af_ref, amaxf_ref, bf_ref, ab_ref, amaxb_ref, bb_ref, hf_ref, hb_ref, c_ref, m_ref):
    t = pl.program_id(1)

    @pl.when(t == 0)
    def _():
        c_ref[...] = jnp.zeros_like(c_ref)
        m_ref[...] = jnp.zeros_like(m_ref)

    for sub in range(SUBS):
        _mlstm_chunk(slice(sub * L, (sub + 1) * L), slice((SUBS - 1 - sub) * L, (SUBS - sub) * L),
                     qtf_ref, kf_ref, vtf_ref, qtb_ref, kb_ref, vtb_ref,
                     af_ref, amaxf_ref, bf_ref, ab_ref, amaxb_ref, bb_ref, hf_ref, hb_ref, c_ref, m_ref)


def _mlstm_chunk(fs, bs, qtf_ref, kf_ref, vtf_ref, qtb_ref, kb_ref, vtb_ref,
                 af_ref, amaxf_ref, bf_ref, ab_ref, amaxb_ref, bb_ref, hf_ref, hb_ref, c_ref, m_ref):
    fwd_rows = lax.broadcasted_iota(jnp.int32, (N_CHAIN, L), 0) < ML_HEADS
    a = jnp.where(fwd_rows, af_ref[:, fs], ab_ref[:, bs])
    amax = jnp.where(fwd_rows, amaxf_ref[:, fs], amaxb_ref[:, bs])
    b = jnp.where(fwd_rows, bf_ref[:, fs], bb_ref[:, bs])
    m_old = m_ref[:, 0:1]
    big = jnp.maximum(m_old, jnp.max(amax, axis=1, keepdims=True))
    decay = jnp.exp(m_old - big)
    e = jnp.exp(a - big)
    mm = jnp.maximum(amax, m_old)
    w_inter = jnp.exp(m_old - mm)
    floor = jnp.exp(-(b + mm))
    m_ref[...] = jnp.broadcast_to(jnp.min(b, axis=1, keepdims=True) + big, (N_CHAIN, 128))
    kq = lax.broadcasted_iota(jnp.int32, (L, L), 0)
    qq = lax.broadcasted_iota(jnp.int32, (L, L), 1)
    a_col = _dot_nt((kq == qq).astype(F32), a, precision=lax.Precision.HIGHEST)

    for d, (qt_ref, k_ref, vt_ref, h_ref, ts) in enumerate(
            ((qtf_ref, kf_ref, vtf_ref, hf_ref, fs), (qtb_ref, kb_ref, vtb_ref, hb_ref, bs))):
        visible = (kq >= qq) if d == 1 else (kq <= qq)
        for hd in range(ML_HEADS):
            c = d * ML_HEADS + hd
            cs = slice(hd * ML_DH, (hd + 1) * ML_DH)
            qt = qt_ref[cs, ts]
            k = k_ref[ts, cs]
            vt = vt_ref[cs, ts]
            p = jnp.where(visible, jnp.exp(a_col[:, c:c + 1] - mm[c:c + 1, :]), 0.0)
            c_old = c_ref[c]
            kcq = _dot(jnp.concatenate([k, c_old.astype(BF16)], axis=0), qt)
            st = kcq[0:L] * p
            cq = kcq[L:]
            wi = w_inter[c:c + 1, :]
            num = _dot(vt, st.astype(BF16)) + wi * cq[0:ML_DH]
            den = jnp.sum(st, axis=0, keepdims=True) + wi * cq[ML_DH:ML_DH + 1]
            ht = num * (1.0 / jnp.maximum(jnp.abs(den), floor[c:c + 1, :]))
            h_ref[ts, cs] = ht.T.astype(BF16)
            er = e[c:c + 1, :]
            vte = jnp.concatenate([vt.astype(F32) * er, jnp.broadcast_to(er, (N_AUG, L))], axis=0)
            c_ref[c] = decay[c:c + 1, :] * c_old + _dot(vte.astype(BF16), k)


def _mlstm(qt, k, vt, a, amax, b):
    lat_chunks = T_LAT // ML_BLK

    def fwd_in(b, t):
        return jnp.where(t < N_CTX_CHUNKS, lat_chunks + b * N_CTX_CHUNKS + t,
                         b * N_LAT_CHUNKS + (t - N_CTX_CHUNKS))

    def bwd_in(b, t):
        return jnp.where(t < N_CTX_CHUNKS, lat_chunks + b * N_CTX_CHUNKS + (N_CTX_CHUNKS - 1 - t),
                         b * N_LAT_CHUNKS + (N_STEPS - 1 - t))

    def fwd_out(b, t):
        return b * N_LAT_CHUNKS + jnp.maximum(t - N_CTX_CHUNKS, 0)

    def bwd_out(b, t):
        return b * N_LAT_CHUNKS + (N_STEPS - 1 - jnp.maximum(t, N_CTX_CHUNKS))

    rows = lambda f: pl.BlockSpec((ML_BLK, ML_DIM), lambda b, t: (f(b, t), 0))
    cols = lambda f: pl.BlockSpec((ML_DIM, ML_BLK), lambda b, t: (0, f(b, t)))
    gate = lambda f: pl.BlockSpec((N_CHAIN, ML_BLK), lambda b, t: (0, f(b, t)))
    return pl.pallas_call(
        _mlstm_kernel,
        grid=(BATCH, N_STEPS),
        in_specs=[cols(fwd_in), rows(fwd_in), cols(fwd_in),
                  cols(bwd_in), rows(bwd_in), cols(bwd_in),
                  gate(fwd_in), gate(fwd_in), gate(fwd_in), gate(bwd_in), gate(bwd_in), gate(bwd_in)],
        out_specs=[rows(fwd_out), rows(bwd_out)],
        out_shape=[jax.ShapeDtypeStruct((T_LAT, ML_DIM), BF16),
                   jax.ShapeDtypeStruct((T_LAT, ML_DIM), BF16)],
        scratch_shapes=[pltpu.VMEM((N_CHAIN, ML_DH + N_AUG, ML_DH), F32),
                        pltpu.VMEM((N_CHAIN, 128), F32)],
        compiler_params=_params(2),
        name="mlstm",
    )(qt, k, vt, qt, k, vt, a, amax, b, a, amax, b)


def _odd_out_kernel(x_ref, m_ref, g_ref, hf_ref, hb_ref, wot_ref, ng_ref, wout_ref, o_ref):
    x = x_ref[...]
    h = _rms_mod(x, g_ref[...], m_ref[:, 0:D], m_ref[:, D:2 * D]).astype(BF16)
    o = _sigmoid(_dot_nt(h, wot_ref[...]))
    hs = hf_ref[...].astype(F32) + hb_ref[...].astype(F32)
    parts = []
    for hd in range(ML_HEADS):
        p = hs[:, hd * ML_DH:(hd + 1) * ML_DH]
        parts.append(p * lax.rsqrt(jnp.mean(p * p, axis=-1, keepdims=True) + EPS))
    hn = jnp.concatenate(parts, axis=-1) * ng_ref[...]
    y = _dot((o * hn).astype(BF16), wout_ref[...])
    o_ref[...] = x + m_ref[:, 2 * D:3 * D] * y


def _odd_out(x, mods, norm_g, hf, hb, w_in_t, head_g, w_out):
    row = lambda i: (i, 0)
    return pl.pallas_call(
        _odd_out_kernel,
        grid=(N_LAT_TILES,),
        in_specs=[
            pl.BlockSpec((TM, D), row),
            _mod_spec(1),
            _resident((1, D)),
            pl.BlockSpec((TM, ML_DIM), row),
            pl.BlockSpec((TM, ML_DIM), row),
            _w_in_t_block(3),
            _resident((1, ML_DIM)),
            _resident((ML_DIM, D)),
        ],
        out_specs=pl.BlockSpec((TM, D), row),
        out_shape=jax.ShapeDtypeStruct((T_LAT, D), F32),
        compiler_params=_params(1),
        name="odd_out",
    )(x, mods, norm_g.reshape(1, D), hf, hb, w_in_t, head_g.reshape(1, ML_DIM), w_out)


def kernel(x, c, ctx, c_ctx, mod_w, mod_b, ffn1_norm, ffn1_w_gu, ffn1_w_d, mix_norm, ffn2_norm,
           ffn2_w_gu, ffn2_w_d, ev_w_in, ev_conv_w, ev_conv_b, ev_conv_ln_g, ev_conv_ln_b, ev_sink,
           ev_w_out, od_w_in, od_b_gate, od_norm_g, od_w_out, final_norm):
    assert DEPTH == 2 and x.shape == (BATCH, SEQ, D) and ctx.shape == (BATCH, CTX, D)
    cond = jnp.zeros((MOD_ROWS, D), F32).at[:BATCH].set(c).at[BATCH].set(c_ctx)
    mods = _ada_mods(cond, mod_w, mod_b).reshape(DEPTH, MOD_ROWS, 1, N_MOD * D)
    bf = lambda w: w.astype(BF16)

    m0 = mods[0]
    xs = _ffn(x.reshape(T_LAT, D), m0, 0, ffn1_norm[0], ffn1_w_gu, ffn1_w_d, 0, N_ALL_TILES,
              x_ctx=ctx.reshape(T_CTX, D))
    w_in = ev_w_in[0]
    rope, rope_t = _rope_tables()
    u, qt, kk, vt = _even_in(xs, m0, mix_norm[0], bf(w_in).T, rope, rope_t)
    sink = jnp.repeat(ev_sink[0].astype(F32), BLK).reshape(1, ATT_HEADS * BLK)
    att = _attention(qt, kk, vt, sink)
    conv_w = jnp.concatenate([ev_conv_w[0], jnp.zeros((1, CONV_DIM), F32)], axis=0)
    ev_args = (conv_w, ev_conv_b[0].reshape(1, -1), ev_conv_ln_g[0].reshape(1, -1),
               ev_conv_ln_b[0].reshape(1, -1), bf(ev_w_out[0][:CONV_DIM]), bf(ev_w_out[0][CONV_DIM:]))
    xs = _even_out(xs, m0, u, att, *ev_args, tm=TM, row0=0, n_tiles=N_LAT_TILES,
                   tiles_per_seq=SEQ // TM, mod_row=lambda i: i // (SEQ // TM))
    xs = _even_out(xs, m0, u, att, *ev_args, tm=CTX, row0=T_LAT, n_tiles=BATCH,
                   tiles_per_seq=1, mod_row=lambda i: BATCH)
    xs = _ffn(xs, m0, 2, ffn2_norm[0], ffn2_w_gu, ffn2_w_d, 0, N_ALL_TILES)

    m1 = mods[1]
    xs = _ffn(xs, m1, 0, ffn1_norm[1], ffn1_w_gu, ffn1_w_d, 1, N_ALL_TILES)
    w_in = od_w_in[0]
    perm = np.concatenate([np.arange(0, 4), np.arange(8, 12), np.arange(4, 8), np.arange(12, 16)])
    w_gate_t = w_in[:, 4 * ML_DIM:].T[perm]
    b_gate_t = od_b_gate[0][perm].reshape(N_GATE, 1)
    w_gate_hi = bf(w_gate_t)
    w_gate_lo = bf(w_gate_t - w_gate_hi.astype(F32))
    w_in_t = bf(w_in[:, :4 * ML_DIM]).T
    w_qg_t = jnp.concatenate([w_in_t[:ML_DIM], w_gate_hi, w_gate_lo], axis=0)
    qt, km, vt, ga, gamax, gb = _odd_in(xs, m1, mix_norm[1], w_qg_t, w_in_t, b_gate_t)
    hf, hb = _mlstm(qt, km, vt, ga, gamax, gb)
    xl = _odd_out(xs, m1, mix_norm[1], hf, hb, w_in_t, od_norm_g[0], bf(od_w_out[0]))
    out = _ffn(xl, m1, 2, ffn2_norm[1], ffn2_w_gu, ffn2_w_d, 1, N_LAT_TILES,
               final_g=final_norm)
    return out.reshape(BATCH, SEQ, D)
```

```python
import functools

import jax
import jax.numpy as jnp
import numpy as np
from jax import lax
from jax.experimental import pallas as pl
from jax.experimental.pallas import tpu as pltpu

D = 1024
BATCH = 4
SEQ = 4096
DEPTH = 2
GRID_W = 64
CTX = 256
N_MOD = 9
D_FF = 2816
EPS = 1e-6
CONV_DIM = 512
CONV_W = 31
CONV_PAD = 15
HEAD_DIM = 64
ATT_HEADS = 8
KV_HEADS = 2
ATT_DIM = 512
KV_DIM = 128
WINDOW = 128
BLK = 128
ROPE_BASE = 10000.0
ROPE_FREQS = 16
Q_OFF = 2 * CONV_DIM
K_OFF = Q_OFF + ATT_DIM
V_OFF = K_OFF + KV_DIM
EVEN_IN = V_OFF + KV_DIM
ML_HEADS = 4
ML_DH = 256
ML_DIM = 1024
ML_CHUNK = 128

T_LAT = BATCH * SEQ
T_CTX = BATCH * CTX
T_ALL = T_LAT + T_CTX
TM = 1024
N_LAT_TILES = T_LAT // TM
N_ALL_TILES = T_ALL // TM
MOD_ROWS = 8
HALO = 16
NEG = -1e30
VMEM_LIMIT = 56 * 1024 * 1024
VMEM_LIMIT_FFN = 60 * 1024 * 1024

F32 = jnp.float32
BF16 = jnp.bfloat16


def _sigmoid(x):
    return 1.0 / (1.0 + jnp.exp(-x))


def _silu(x):
    return x * _sigmoid(x)


def _dot(a, b, precision=None):
    return jnp.dot(a, b, preferred_element_type=F32, precision=precision)


def _dot_nt(a, b, precision=None):
    return lax.dot_general(a, b, (((1,), (1,)), ((), ())),
                           preferred_element_type=F32, precision=precision)


def _dot_tn(a, b):
    return lax.dot_general(a, b, (((0,), (0,)), ((), ())), preferred_element_type=F32)


def _rms_mod(x, g, shift, scale):
    y = x * lax.rsqrt(jnp.mean(x * x, axis=-1, keepdims=True) + EPS)
    return (y * g) * (1.0 + scale) + shift


def _resident(shape):
    nd = len(shape)
    return pl.BlockSpec(shape, lambda *_: (0,) * nd, pipeline_mode=pl.Buffered(1))


def _params(n_axes=1, vmem_limit=VMEM_LIMIT):
    return pltpu.CompilerParams(dimension_semantics=("arbitrary",) * n_axes,
                                vmem_limit_bytes=vmem_limit)


def _mod_kernel(c_ref, w_ref, b_ref, o_ref):
    s = _silu(c_ref[...]).astype(BF16)
    o_ref[...] = _dot(s, w_ref[...].astype(BF16)) + b_ref[...]


def _ada_mods(cond, mod_w, mod_b):
    tn = 1024
    n = N_MOD * D
    return pl.pallas_call(
        _mod_kernel,
        grid=(DEPTH, n // tn),
        in_specs=[
            pl.BlockSpec((MOD_ROWS, D), lambda l, j: (0, 0)),
            pl.BlockSpec((None, D, tn), lambda l, j: (l, 0, j)),
            pl.BlockSpec((None, 1, tn), lambda l, j: (l, 0, j)),
        ],
        out_specs=pl.BlockSpec((None, MOD_ROWS, tn), lambda l, j: (l, 0, j)),
        out_shape=jax.ShapeDtypeStruct((DEPTH, MOD_ROWS, n), F32),
        compiler_params=_params(2),
        name="ada_mods",
    )(cond, mod_w, mod_b.reshape(DEPTH, 1, n))


def _mod_spec(k):
    return pl.BlockSpec((None, 1, 3 * D), lambda i: (i // (SEQ // TM), 0, k))


FF_CHUNKS = tuple((c, min(c + 512, D_FF)) for c in range(0, D_FF, 512))


STAGE = 256
N_STAGE = 3


def _stage_weights(src_ref, dst_ref, stage_ref, sem_ref, chunks):
    def copy(n):
        slot = n % N_STAGE
        return pltpu.make_async_copy(src_ref.at[chunks[n][0]], stage_ref.at[slot], sem_ref.at[slot])

    for n in range(min(N_STAGE, len(chunks))):
        copy(n).start()
    for n in range(len(chunks)):
        copy(n).wait()
        dst_ref[chunks[n][1]] = stage_ref[n % N_STAGE].astype(BF16)
        if n + N_STAGE < len(chunks):
            copy(n + N_STAGE).start()


def _ffn_kernel(x_ref, *rest, layer, split, final):
    if split:
        xc_ref, *rest = rest
    m_ref, g_ref, wgu_hbm, wd_hbm, *rest = rest
    if final:
        fn_ref, *rest = rest
    o_ref, wgu_ref, wd_ref, sgu_ref, sd_ref, sem_gu, sem_d = rest

    @pl.when(pl.program_id(0) == 0)
    def _():
        _stage_weights(wgu_hbm, wgu_ref, sgu_ref, sem_gu,
                       [((layer, slice(None), pl.ds(c, STAGE)), (slice(None), pl.ds(c, STAGE)))
                        for c in range(0, 2 * D_FF, STAGE)])
        _stage_weights(wd_hbm, wd_ref, sd_ref, sem_d,
                       [((layer, pl.ds(r, STAGE), slice(None)), (pl.ds(r, STAGE), slice(None)))
                        for r in range(0, D_FF, STAGE)])

    x = x_ref[...]
    if split:
        x = jnp.where(pl.program_id(0) < N_LAT_TILES, x, xc_ref[...])
    shift = m_ref[:, 0:D]
    scale = m_ref[:, D:2 * D]
    gate = m_ref[:, 2 * D:3 * D]
    h = _rms_mod(x, g_ref[...], shift, scale).astype(BF16)
    acc = None
    for c0, c1 in FF_CHUNKS:
        hg = _dot(h, wgu_ref[:, c0:c1])
        hu = _dot(h, wgu_ref[:, D_FF + c0:D_FF + c1])
        a = (_silu(hg) * hu).astype(BF16)
        p = _dot(a, wd_ref[c0:c1, :])
        acc = p if acc is None else acc + p
    y = x + (0.5 * gate) * acc
    if final:
        y = (y * lax.rsqrt(jnp.mean(y * y, axis=-1, keepdims=True) + EPS)) * fn_ref[...]
    o_ref[...] = y


def _ffn(x, mods, k, norm_g, w_gu, w_d, layer, n_tiles, final_g=None, x_ctx=None):
    final = final_g is not None
    split = x_ctx is not None
    if split:
        in_specs = [pl.BlockSpec((TM, D), lambda i: (jnp.minimum(i, N_LAT_TILES - 1), 0)),
                    _resident((TM, D))]
        args = [x, x_ctx]
    else:
        in_specs = [pl.BlockSpec((TM, D), lambda i: (i, 0))]
        args = [x]
    in_specs += [
        _mod_spec(k),
        _resident((1, D)),
        pl.BlockSpec(memory_space=pl.ANY),
        pl.BlockSpec(memory_space=pl.ANY),
    ]
    args += [mods, norm_g.reshape(1, D), w_gu, w_d]
    if final:
        in_specs.append(_resident((1, D)))
        args.append(final_g.reshape(1, D))
    return pl.pallas_call(
        functools.partial(_ffn_kernel, layer=layer, split=split, final=final),
        grid=(n_tiles,),
        in_specs=in_specs,
        out_specs=pl.BlockSpec((TM, D), lambda i: (i, 0)),
        out_shape=jax.ShapeDtypeStruct((n_tiles * TM, D), F32),
        scratch_shapes=[pltpu.VMEM((D, 2 * D_FF), BF16),
                        pltpu.VMEM((D_FF, D), BF16),
                        pltpu.VMEM((N_STAGE, D, STAGE), F32),
                        pltpu.VMEM((N_STAGE, STAGE, D), F32),
                        pltpu.SemaphoreType.DMA((N_STAGE,)),
                        pltpu.SemaphoreType.DMA((N_STAGE,))],
        compiler_params=_params(1, VMEM_LIMIT_FFN),
        name="ffn_final" if final else ("ffn_first" if split else "ffn"),
    )(*args)


def _even_in_kernel(x_ref, m_ref, g_ref, wt_ref,
                    c_ref, s1_ref, s2_ref, ct_ref, s1t_ref, s2t_ref,
                    u_ref, qt_ref, k_ref, vt_ref):
    x = x_ref[...]
    h = _rms_mod(x, g_ref[...], m_ref[:, 0:D], m_ref[:, D:2 * D]).astype(BF16)
    vg = _dot_nt(h, wt_ref[0:Q_OFF, :])
    u_ref[...] = vg[:, 0:CONV_DIM] * _sigmoid(vg[:, CONV_DIM:Q_OFF])
    qf = _dot_nt(wt_ref[Q_OFF:K_OFF, :], h)
    ct, s1t, s2t = ct_ref[...], s1t_ref[...], s2t_ref[...]
    for hd in range(ATT_HEADS):
        xh = qf[hd * HEAD_DIM:(hd + 1) * HEAD_DIM]
        up = jnp.concatenate([xh[ROPE_FREQS:], xh[:ROPE_FREQS]], axis=0)
        dn = jnp.concatenate([xh[HEAD_DIM - ROPE_FREQS:], xh[:HEAD_DIM - ROPE_FREQS]], axis=0)
        r = xh * ct + up * s1t + dn * s2t
        qt_ref[hd * HEAD_DIM:(hd + 1) * HEAD_DIM, :] = (r * (HEAD_DIM ** -0.5)).astype(BF16)
    kf = _dot_nt(h, wt_ref[K_OFF:V_OFF, :])
    kr = kf * c_ref[...] + pltpu.roll(kf, 128 - ROPE_FREQS, 1) * s1_ref[...] + pltpu.roll(kf, ROPE_FREQS, 1) * s2_ref[...]
    k_ref[...] = kr.astype(BF16)
    vt_ref[...] = _dot_nt(wt_ref[V_OFF:EVEN_IN, :], h).astype(BF16)


def _even_in(x, mods, norm_g, w_t, rope, rope_t):
    pos = lambda i: jnp.where(i < N_LAT_TILES, i % (SEQ // TM), SEQ // TM)
    rope_spec = lambda: pl.BlockSpec((TM, KV_DIM), lambda i: (pos(i), 0))
    rope_t_spec = lambda: pl.BlockSpec((HEAD_DIM, TM), lambda i: (0, pos(i)))
    row = lambda i: (i, 0)
    col = lambda i: (0, i)
    return pl.pallas_call(
        _even_in_kernel,
        grid=(N_ALL_TILES,),
        in_specs=[
            pl.BlockSpec((TM, D), row),
            _mod_spec(1),
            _resident((1, D)),
            _resident((EVEN_IN, D)),
            rope_spec(), rope_spec(), rope_spec(),
            rope_t_spec(), rope_t_spec(), rope_t_spec(),
        ],
        out_specs=[
            pl.BlockSpec((TM, CONV_DIM), row),
            pl.BlockSpec((ATT_DIM, TM), col),
            pl.BlockSpec((TM, KV_DIM), row),
            pl.BlockSpec((KV_DIM, TM), col),
        ],
        out_shape=[
            jax.ShapeDtypeStruct((T_ALL, CONV_DIM), F32),
            jax.ShapeDtypeStruct((ATT_DIM, T_ALL), BF16),
            jax.ShapeDtypeStruct((T_ALL, KV_DIM), BF16),
            jax.ShapeDtypeStruct((KV_DIM, T_ALL), BF16),
        ],
        compiler_params=_params(1),
        name="even_in",
    )(x, mods, norm_g.reshape(1, D), w_t, *rope, *rope_t)


def _rope_tables():
    rows = SEQ // GRID_W
    inv = ROPE_BASE ** (-jnp.arange(ROPE_FREQS, dtype=F32) / ROPE_FREQS)
    d = np.arange(128) % HEAD_DIM
    first = jnp.asarray(((d % 32) // ROPE_FREQS) == 0)
    by_row = jnp.asarray(d // 32 == 0)

    def expand(fn):
        t_row = fn(jnp.arange(rows, dtype=F32)[:, None] * inv)[:, d % ROPE_FREQS]
        t_col = fn(jnp.arange(GRID_W, dtype=F32)[:, None] * inv)[:, d % ROPE_FREQS]
        return jnp.where(by_row, jnp.repeat(t_row, GRID_W, axis=0), jnp.tile(t_col, (rows, 1)))

    c = expand(jnp.cos)
    s = expand(jnp.sin)
    s1 = jnp.where(first, -s, 0.0)
    s2 = jnp.where(first, 0.0, s)
    pad = lambda t, v: jnp.concatenate([t, jnp.full((TM, 128), v, F32)], axis=0)
    tok = (pad(c, 1.0), pad(s1, 0.0), pad(s2, 0.0))
    return tok, tuple(t[:, :HEAD_DIM].T for t in tok)


NB = SEQ // BLK
QB = 2 * BLK
GROUP = ATT_HEADS // KV_HEADS
assert CTX == QB


def _attn_kernel(qt_ref, kp_ref, kc_ref, kn_ref, kx_ref, vp_ref, vc_ref, vn_ref, vx_ref,
                 sink_ref, o_ref):
    for sub in range(QB // BLK):
        ks = [kp_ref[...], kc_ref[0:BLK], kc_ref[BLK:2 * BLK], kn_ref[...]][sub:sub + 3]
        vs = [vp_ref[...], vc_ref[:, 0:BLK], vc_ref[:, BLK:2 * BLK], vn_ref[...]][sub:sub + 3]
        o_ref[sub * BLK:(sub + 1) * BLK, :] = _attn_block(
            (QB // BLK) * pl.program_id(1) + sub, qt_ref[:, sub * BLK:(sub + 1) * BLK],
            ks + [kx_ref[...]], vs + [vx_ref[...]], sink_ref)


def _attn_block(i, qt, ks, vs, sink_ref):
    kk = lax.broadcasted_iota(jnp.int32, (3 * BLK, BLK), 0)
    qq = lax.broadcasted_iota(jnp.int32, (3 * BLK, BLK), 1)
    rel = kk - BLK - qq
    kpos = (i - 1) * BLK + kk
    ok = (jnp.abs(rel) <= WINDOW) & (kpos >= 0) & (kpos < SEQ) & (i < NB)
    bias1 = jnp.where(ok, 0.0, NEG).astype(F32)
    bias = jnp.concatenate([bias1] * GROUP, axis=1)
    keys = jnp.concatenate(ks, axis=0)
    vt = jnp.concatenate(vs, axis=1)
    zero = jnp.zeros((HEAD_DIM, BLK), BF16)
    outs = []
    for g in range(KV_HEADS):
        cols = []
        for hh in range(GROUP):
            hd = g * GROUP + hh
            qh = qt[hd * HEAD_DIM:(hd + 1) * HEAD_DIM]
            cols.append(jnp.concatenate([qh, zero] if g == 0 else [zero, qh], axis=0))
        s = _dot(keys, jnp.concatenate(cols, axis=1))
        s_b = s[0:3 * BLK] + bias
        s_x = s[3 * BLK:]
        sink = sink_ref[:, g * GROUP * BLK:(g + 1) * GROUP * BLK]
        m = jnp.maximum(jnp.maximum(jnp.max(s_b, axis=0, keepdims=True),
                                    jnp.max(s_x, axis=0, keepdims=True)), sink)
        p_b = jnp.exp(s_b - m)
        p_x = jnp.exp(s_x - m)
        den = (jnp.sum(p_b, axis=0, keepdims=True) + jnp.sum(p_x, axis=0, keepdims=True)
               + jnp.exp(sink - m))
        p = jnp.concatenate([p_b, p_x], axis=0).astype(BF16)
        ot = _dot(vt[g * HEAD_DIM:(g + 1) * HEAD_DIM], p) * (1.0 / den)
        outs += [ot[:, hh * BLK:(hh + 1) * BLK] for hh in range(GROUP)]
    return jnp.concatenate(outs, axis=0).T.astype(BF16)


def _attention(qt, k, vt, sink):
    n_pairs = SEQ // QB
    q_blk = lambda b, j: jnp.where(j < n_pairs, b * n_pairs + j, T_LAT // QB + b)
    pair = lambda b, j: b * n_pairs + jnp.minimum(j, n_pairs - 1)
    edge = lambda off: (lambda b, j: b * NB + jnp.clip((QB // BLK) * j + off, 0, NB - 1))
    ctx_blk = lambda b, j: T_LAT // CTX + b
    k_spec = lambda n, f: pl.BlockSpec((n, KV_DIM), lambda b, j: (f(b, j), 0))
    v_spec = lambda n, f: pl.BlockSpec((KV_DIM, n), lambda b, j: (0, f(b, j)))
    return pl.pallas_call(
        _attn_kernel,
        grid=(BATCH, n_pairs + 1),
        in_specs=[
            pl.BlockSpec((ATT_DIM, QB), lambda b, j: (0, q_blk(b, j))),
            k_spec(BLK, edge(-1)), k_spec(QB, pair), k_spec(BLK, edge(QB // BLK)), k_spec(CTX, ctx_blk),
            v_spec(BLK, edge(-1)), v_spec(QB, pair), v_spec(BLK, edge(QB // BLK)), v_spec(CTX, ctx_blk),
            pl.BlockSpec((1, ATT_HEADS * BLK), lambda b, j: (0, 0)),
        ],
        out_specs=pl.BlockSpec((QB, ATT_DIM), lambda b, j: (q_blk(b, j), 0)),
        out_shape=jax.ShapeDtypeStruct((T_ALL, ATT_DIM), BF16),
        compiler_params=_params(2),
        name="window_attn",
    )(qt, k, k, k, k, vt, vt, vt, vt, sink)


CONV_S = 4
CONV_ROWS = 8 * CONV_S
N_SLAB = CONV_DIM // 128


def _even_out_kernel(x_ref, m_ref, u_ref, up_ref, un_ref, att_ref, cw_ref, cb_ref,
                     lg_ref, lb_ref, wa_ref, wb_ref, o_ref, ext_ref, a_ref, cva_ref, cvb_ref, *, tm, tiles_per_seq):
    t = pl.program_id(0)
    first = (t % tiles_per_seq) == 0
    last = (t % tiles_per_seq) == tiles_per_seq - 1
    slabs = [slice(s * 128, (s + 1) * 128) for s in range(N_SLAB)]
    for s, ls in enumerate(slabs):
        ext_ref[s, 0:HALO, :] = jnp.where(first, 0.0, up_ref[:, ls])
        ext_ref[s, HALO:HALO + tm, :] = u_ref[:, ls]
        ext_ref[s, HALO + tm:, :] = jnp.where(last, 0.0, un_ref[:, ls])

    def conv_block(blk, cv_ref):
        base = blk * CONV_ROWS
        for s, ls in enumerate(slabs):
            acc = [jnp.broadcast_to(cb_ref[:, ls], (8, 128)) for _ in range(CONV_S)]
            for o in range(CONV_W + CONV_S - 1):
                v = ext_ref[s, pl.ds(base + (HALO - CONV_PAD) + o, 8, stride=CONV_S), :]
                for j in range(CONV_S):
                    k = o - j
                    if 0 <= k < CONV_W:
                        acc[j] = acc[j] + cw_ref[k:k + 1, ls] * v
            for j in range(CONV_S):
                cv_ref[s * CONV_S + j] = acc[j]

    def norm_block(blk, cv_ref):
        base = blk * CONV_ROWS
        for j in range(CONV_S):
            row = [cv_ref[s * CONV_S + j] for s in range(N_SLAB)]
            mu = jnp.sum(sum(row), axis=-1, keepdims=True) * (1.0 / CONV_DIM)
            cen = [r - mu for r in row]
            var = jnp.sum(sum(c * c for c in cen), axis=-1, keepdims=True) * (1.0 / CONV_DIM)
            rs = lax.rsqrt(var + EPS)
            for s, ls in enumerate(slabs):
                y = cen[s] * rs * lg_ref[:, ls] + lb_ref[:, ls]
                a_ref[s, pl.ds(base + j, 8, stride=CONV_S), :] = _silu(y)

    n_blk = tm // CONV_ROWS
    conv_block(0, cva_ref)

    def body(i, carry):
        conv_block(2 * i + 1, cvb_ref)
        norm_block(2 * i, cva_ref)
        conv_block(jnp.minimum(2 * i + 2, n_blk - 1), cva_ref)
        norm_block(2 * i + 1, cvb_ref)
        return carry

    lax.fori_loop(0, n_blk // 2, body, 0)
    a = jnp.concatenate([a_ref[s] for s in range(N_SLAB)], axis=1).astype(BF16)
    y = _dot(a, wa_ref[...]) + _dot(att_ref[...], wb_ref[...])
    o_ref[...] = x_ref[...] + m_ref[:, 2 * D:3 * D] * y


def _even_out(x, mods, u, att, conv_w, conv_b, ln_g, ln_b, wa, wb, *, tm, row0, n_tiles,
              tiles_per_seq, mod_row):
    blk0 = row0 // tm
    hb = tm // HALO
    n_halo = T_ALL // HALO
    row_map = lambda i: (blk0 + i, 0)
    in_specs = [
        pl.BlockSpec((tm, D), row_map),
        pl.BlockSpec((None, 1, 3 * D), lambda i: (mod_row(i), 0, 1)),
        pl.BlockSpec((tm, CONV_DIM), row_map),
        pl.BlockSpec((HALO, CONV_DIM), lambda i: (jnp.maximum((blk0 + i) * hb - 1, 0), 0)),
        pl.BlockSpec((HALO, CONV_DIM), lambda i: (jnp.minimum((blk0 + i + 1) * hb, n_halo - 1), 0)),
        pl.BlockSpec((tm, ATT_DIM), row_map),
        _resident((32, CONV_DIM)),
        _resident((1, CONV_DIM)),
        _resident((1, CONV_DIM)),
        _resident((1, CONV_DIM)),
        _resident((CONV_DIM, D)),
        _resident((ATT_DIM, D)),
    ]
    return pl.pallas_call(
        functools.partial(_even_out_kernel, tm=tm, tiles_per_seq=tiles_per_seq),
        grid=(n_tiles,),
        in_specs=in_specs,
        out_specs=pl.BlockSpec((tm, D), row_map),
        out_shape=jax.ShapeDtypeStruct((T_ALL, D), F32),
        scratch_shapes=[pltpu.VMEM((N_SLAB, tm + 2 * HALO, 128), F32),
                        pltpu.VMEM((N_SLAB, tm, 128), F32),
                        pltpu.VMEM((N_SLAB * CONV_S, 8, 128), F32),
                        pltpu.VMEM((N_SLAB * CONV_S, 8, 128), F32)],
        input_output_aliases={0: 0},
        compiler_params=_params(1),
        name="even_out" if tiles_per_seq > 1 else "even_out_ctx",
    )(x, mods, u, u, u, att, conv_w, conv_b, ln_g, ln_b, wa, wb)


N_GATE = 4 * ML_HEADS
N_CHAIN = 2 * ML_HEADS
L = ML_CHUNK


def _log_sigmoid(x):
    return jnp.minimum(x, 0.0) - jnp.log(1.0 + jnp.exp(-jnp.abs(x)))


def _lane_scan(x, lane, fwd_rows, combine, fill):
    sh = 1
    while sh < L:
        pre = jnp.where(lane >= sh, pltpu.roll(x, sh, 1), fill)
        suf = jnp.where(lane < L - sh, pltpu.roll(x, L - sh, 1), fill)
        x = combine(x, jnp.where(fwd_rows, pre, suf))
        sh *= 2
    return x


def _odd_in_kernel(x_ref, m_ref, g_ref, wqt_ref, wkt_ref, wvt_ref, bgt_ref,
                   qt_ref, k_ref, vt_ref, a_ref, amax_ref, b_ref):
    x = x_ref[...]
    hf = _rms_mod(x, g_ref[...], m_ref[:, 0:D], m_ref[:, D:2 * D])
    h = hf.astype(BF16)
    h_lo = (hf - h.astype(F32)).astype(BF16)
    qg = _dot_nt(wqt_ref[...], h)
    qt_ref[...] = qg[0:ML_DIM].astype(BF16)
    g = (qg[ML_DIM:ML_DIM + N_GATE] + qg[ML_DIM + N_GATE:]
         + _dot_nt(wqt_ref[ML_DIM:ML_DIM + N_GATE, :], h_lo) + bgt_ref[...])
    k_ref[...] = (_dot_nt(h, wkt_ref[...]) * (ML_DH ** -0.5)).astype(BF16)
    vt_ref[...] = _dot_nt(wvt_ref[...], h).astype(BF16)
    li = g[0:N_CHAIN]
    lf = _log_sigmoid(g[N_CHAIN:N_GATE])
    fwd_rows = lax.broadcasted_iota(jnp.int32, (N_CHAIN, L), 0) < ML_HEADS
    lane = lax.broadcasted_iota(jnp.int32, (N_CHAIN, L), 1)
    for ch in range(TM // L):
        cs = slice(ch * L, (ch + 1) * L)
        b = _lane_scan(lf[:, cs], lane, fwd_rows, jnp.add, 0.0)
        a = li[:, cs] - b
        a_ref[:, cs] = a
        amax_ref[:, cs] = _lane_scan(a, lane, fwd_rows, jnp.maximum, NEG)
        b_ref[:, cs] = b


def _w_in_t_block(j):
    return pl.BlockSpec((ML_DIM, D), lambda i: (j, 0), pipeline_mode=pl.Buffered(1))


def _odd_in(x, mods, norm_g, w_qg_t, w_in_t, b_gate_t):
    row = lambda i: (i, 0)
    col = lambda i: (0, i)
    return pl.pallas_call(
        _odd_in_kernel,
        grid=(N_ALL_TILES,),
        in_specs=[
            pl.BlockSpec((TM, D), row),
            _mod_spec(1),
            _resident((1, D)),
            _resident((ML_DIM + 2 * N_GATE, D)),
            _w_in_t_block(1),
            _w_in_t_block(2),
            _resident((N_GATE, 1)),
        ],
        out_specs=[
            pl.BlockSpec((ML_DIM, TM), col),
            pl.BlockSpec((TM, ML_DIM), row),
            pl.BlockSpec((ML_DIM, TM), col),
            pl.BlockSpec((N_CHAIN, TM), col),
            pl.BlockSpec((N_CHAIN, TM), col),
            pl.BlockSpec((N_CHAIN, TM), col),
        ],
        out_shape=[
            jax.ShapeDtypeStruct((ML_DIM, T_ALL), BF16),
            jax.ShapeDtypeStruct((T_ALL, ML_DIM), BF16),
            jax.ShapeDtypeStruct((ML_DIM, T_ALL), BF16),
            jax.ShapeDtypeStruct((N_CHAIN, T_ALL), F32),
            jax.ShapeDtypeStruct((N_CHAIN, T_ALL), F32),
            jax.ShapeDtypeStruct((N_CHAIN, T_ALL), F32),
        ],
        compiler_params=_params(1),
        name="odd_in",
    )(x, mods, norm_g.reshape(1, D), w_qg_t, w_in_t, w_in_t, b_gate_t)


SUBS = 2
ML_BLK = SUBS * ML_CHUNK
N_CTX_CHUNKS = CTX // ML_BLK
N_LAT_CHUNKS = SEQ // ML_BLK
N_STEPS = N_CTX_CHUNKS + N_LAT_CHUNKS
N_AUG = 16


def _mlstm_kernel(qtf_ref, kf_ref, vtf_ref, qtb_ref, kb_ref, vtb_ref,
                  af_ref, amaxf_ref, bf_ref, ab_ref, amaxb_ref, bb_ref, hf_ref, hb_ref, c_ref, m_ref):
    t = pl.program_id(1)

    @pl.when(t == 0)
    def _():
        c_ref[...] = jnp.zeros_like(c_ref)
        m_ref[...] = jnp.zeros_like(m_ref)

    for sub in range(SUBS):
        _mlstm_chunk(slice(sub * L, (sub + 1) * L), slice((SUBS - 1 - sub) * L, (SUBS - sub) * L),
                     qtf_ref, kf_ref, vtf_ref, qtb_ref, kb_ref, vtb_ref,
                     af_ref, amaxf_ref, bf_ref, ab_ref, amaxb_ref, bb_ref, hf_ref, hb_ref, c_ref, m_ref)


def _mlstm_chunk(fs, bs, qtf_ref, kf_ref, vtf_ref, qtb_ref, kb_ref, vtb_ref,
                 af_ref, amaxf_ref, bf_ref, ab_ref, amaxb_ref, bb_ref, hf_ref, hb_ref, c_ref, m_ref):
    fwd_rows = lax.broadcasted_iota(jnp.int32, (N_CHAIN, L), 0) < ML_HEADS
    a = jnp.where(fwd_rows, af_ref[:, fs], ab_ref[:, bs])
    amax = jnp.where(fwd_rows, amaxf_ref[:, fs], amaxb_ref[:, bs])
    b = jnp.where(fwd_rows, bf_ref[:, fs], bb_ref[:, bs])
    m_old = m_ref[:, 0:1]
    big = jnp.maximum(m_old, jnp.max(amax, axis=1, keepdims=True))
    decay = jnp.exp(m_old - big)
    e = jnp.exp(a - big)
    mm = jnp.maximum(amax, m_old)
    w_inter = jnp.exp(m_old - mm)
    floor = jnp.exp(-(b + mm))
    m_ref[...] = jnp.broadcast_to(jnp.min(b, axis=1, keepdims=True) + big, (N_CHAIN, 128))
    kq = lax.broadcasted_iota(jnp.int32, (L, L), 0)
    qq = lax.broadcasted_iota(jnp.int32, (L, L), 1)
    a_col = jnp.concatenate([a, jnp.zeros((L - N_CHAIN, L), F32)], axis=0).T

    for d, (qt_ref, k_ref, vt_ref, h_ref, ts) in enumerate(
            ((qtf_ref, kf_ref, vtf_ref, hf_ref, fs), (qtb_ref, kb_ref, vtb_ref, hb_ref, bs))):
        visible = (kq >= qq) if d == 1 else (kq <= qq)
        for hd in range(ML_HEADS):
            c = d * ML_HEADS + hd
            cs = slice(hd * ML_DH, (hd + 1) * ML_DH)
            qt = qt_ref[cs, ts]
            k = k_ref[ts, cs]
            vt = vt_ref[cs, ts]
            p = jnp.where(visible, jnp.exp(a_col[:, c:c + 1] - mm[c:c + 1, :]), 0.0)
            c_old = c_ref[c]
            kcq = _dot(jnp.concatenate([k, c_old.astype(BF16)], axis=0), qt)
            st = kcq[0:L] * p
            cq = kcq[L:]
            wi = w_inter[c:c + 1, :]
            num = _dot(vt, st.astype(BF16)) + wi * cq[0:ML_DH]
            den = jnp.sum(st, axis=0, keepdims=True) + wi * cq[ML_DH:ML_DH + 1]
            ht = num * (1.0 / jnp.maximum(jnp.abs(den), floor[c:c + 1, :]))
            h_ref[ts, cs] = ht.T.astype(BF16)
            er = e[c:c + 1, :]
            vte = jnp.concatenate([vt.astype(F32) * er, jnp.broadcast_to(er, (N_AUG, L))], axis=0)
            c_ref[c] = decay[c:c + 1, :] * c_old + _dot(vte.astype(BF16), k)


def _mlstm(qt, k, vt, a, amax, b):
    lat_chunks = T_LAT // ML_BLK

    def fwd_in(b, t):
        return jnp.where(t < N_CTX_CHUNKS, lat_chunks + b * N_CTX_CHUNKS + t,
                         b * N_LAT_CHUNKS + (t - N_CTX_CHUNKS))

    def bwd_in(b, t):
        return jnp.where(t < N_CTX_CHUNKS, lat_chunks + b * N_CTX_CHUNKS + (N_CTX_CHUNKS - 1 - t),
                         b * N_LAT_CHUNKS + (N_STEPS - 1 - t))

    def fwd_out(b, t):
        return b * N_LAT_CHUNKS + jnp.maximum(t - N_CTX_CHUNKS, 0)

    def bwd_out(b, t):
        return b * N_LAT_CHUNKS + (N_STEPS - 1 - jnp.maximum(t, N_CTX_CHUNKS))

    rows = lambda f: pl.BlockSpec((ML_BLK, ML_DIM), lambda b, t: (f(b, t), 0))
    cols = lambda f: pl.BlockSpec((ML_DIM, ML_BLK), lambda b, t: (0, f(b, t)))
    gate = lambda f: pl.BlockSpec((N_CHAIN, ML_BLK), lambda b, t: (0, f(b, t)))
    return pl.pallas_call(
        _mlstm_kernel,
        grid=(BATCH, N_STEPS),
        in_specs=[cols(fwd_in), rows(fwd_in), cols(fwd_in),
                  cols(bwd_in), rows(bwd_in), cols(bwd_in),
                  gate(fwd_in), gate(fwd_in), gate(fwd_in), gate(bwd_in), gate(bwd_in), gate(bwd_in)],
        out_specs=[rows(fwd_out), rows(bwd_out)],
        out_shape=[jax.ShapeDtypeStruct((T_LAT, ML_DIM), BF16),
                   jax.ShapeDtypeStruct((T_LAT, ML_DIM), BF16)],
        scratch_shapes=[pltpu.VMEM((N_CHAIN, ML_DH + N_AUG, ML_DH), F32),
                        pltpu.VMEM((N_CHAIN, 128), F32)],
        compiler_params=_params(2),
        name="mlstm",
    )(qt, k, vt, qt, k, vt, a, amax, b, a, amax, b)


def _odd_out_kernel(x_ref, m_ref, g_ref, hf_ref, hb_ref, wot_ref, ng_ref, wout_ref, o_ref):
    x = x_ref[...]
    h = _rms_mod(x, g_ref[...], m_ref[:, 0:D], m_ref[:, D:2 * D]).astype(BF16)
    o = _sigmoid(_dot_nt(h, wot_ref[...]))
    hs = hf_ref[...].astype(F32) + hb_ref[...].astype(F32)
    parts = []
    for hd in range(ML_HEADS):
        p = hs[:, hd * ML_DH:(hd + 1) * ML_DH]
        parts.append(p * lax.rsqrt(jnp.mean(p * p, axis=-1, keepdims=True) + EPS))
    hn = jnp.concatenate(parts, axis=-1) * ng_ref[...]
    y = _dot((o * hn).astype(BF16), wout_ref[...])
    o_ref[...] = x + m_ref[:, 2 * D:3 * D] * y


def _odd_out(x, mods, norm_g, hf, hb, w_in_t, head_g, w_out):
    row = lambda i: (i, 0)
    return pl.pallas_call(
        _odd_out_kernel,
        grid=(N_LAT_TILES,),
        in_specs=[
            pl.BlockSpec((TM, D), row),
            _mod_spec(1),
            _resident((1, D)),
            pl.BlockSpec((TM, ML_DIM), row),
            pl.BlockSpec((TM, ML_DIM), row),
            _w_in_t_block(3),
            _resident((1, ML_DIM)),
            _resident((ML_DIM, D)),
        ],
        out_specs=pl.BlockSpec((TM, D), row),
        out_shape=jax.ShapeDtypeStruct((T_LAT, D), F32),
        compiler_params=_params(1),
        name="odd_out",
    )(x, mods, norm_g.reshape(1, D), hf, hb, w_in_t, head_g.reshape(1, ML_DIM), w_out)


def kernel(x, c, ctx, c_ctx, mod_w, mod_b, ffn1_norm, ffn1_w_gu, ffn1_w_d, mix_norm, ffn2_norm,
           ffn2_w_gu, ffn2_w_d, ev_w_in, ev_conv_w, ev_conv_b, ev_conv_ln_g, ev_conv_ln_b, ev_sink,
           ev_w_out, od_w_in, od_b_gate, od_norm_g, od_w_out, final_norm):
    assert DEPTH == 2 and x.shape == (BATCH, SEQ, D) and ctx.shape == (BATCH, CTX, D)
    cond = jnp.zeros((MOD_ROWS, D), F32).at[:BATCH].set(c).at[BATCH].set(c_ctx)
    mods = _ada_mods(cond, mod_w, mod_b).reshape(DEPTH, MOD_ROWS, 1, N_MOD * D)
    bf = lambda w: w.astype(BF16)

    m0 = mods[0]
    xs = _ffn(x.reshape(T_LAT, D), m0, 0, ffn1_norm[0], ffn1_w_gu, ffn1_w_d, 0, N_ALL_TILES,
              x_ctx=ctx.reshape(T_CTX, D))
    w_in = ev_w_in[0]
    rope, rope_t = _rope_tables()
    u, qt, kk, vt = _even_in(xs, m0, mix_norm[0], bf(w_in).T, rope, rope_t)
    sink = jnp.repeat(ev_sink[0].astype(F32), BLK).reshape(1, ATT_HEADS * BLK)
    att = _attention(qt, kk, vt, sink)
    conv_w = jnp.concatenate([ev_conv_w[0], jnp.zeros((1, CONV_DIM), F32)], axis=0)
    ev_args = (conv_w, ev_conv_b[0].reshape(1, -1), ev_conv_ln_g[0].reshape(1, -1),
               ev_conv_ln_b[0].reshape(1, -1), bf(ev_w_out[0][:CONV_DIM]), bf(ev_w_out[0][CONV_DIM:]))
    xs = _even_out(xs, m0, u, att, *ev_args, tm=TM, row0=0, n_tiles=N_LAT_TILES,
                   tiles_per_seq=SEQ // TM, mod_row=lambda i: i // (SEQ // TM))
    xs = _even_out(xs, m0, u, att, *ev_args, tm=CTX, row0=T_LAT, n_tiles=BATCH,
                   tiles_per_seq=1, mod_row=lambda i: BATCH)
    xs = _ffn(xs, m0, 2, ffn2_norm[0], ffn2_w_gu, ffn2_w_d, 0, N_ALL_TILES)

    m1 = mods[1]
    xs = _ffn(xs, m1, 0, ffn1_norm[1], ffn1_w_gu, ffn1_w_d, 1, N_ALL_TILES)
    w_in = od_w_in[0]
    perm = np.concatenate([np.arange(0, 4), np.arange(8, 12), np.arange(4, 8), np.arange(12, 16)])
    w_gate_t = w_in[:, 4 * ML_DIM:].T[perm]
    b_gate_t = od_b_gate[0][perm].reshape(N_GATE, 1)
    w_gate_hi = bf(w_gate_t)
    w_gate_lo = bf(w_gate_t - w_gate_hi.astype(F32))
    w_in_t = bf(w_in[:, :4 * ML_DIM]).T
    w_qg_t = jnp.concatenate([w_in_t[:ML_DIM], w_gate_hi, w_gate_lo], axis=0)
    qt, km, vt, ga, gamax, gb = _odd_in(xs, m1, mix_norm[1], w_qg_t, w_in_t, b_gate_t)
    hf, hb = _mlstm(qt, km, vt, ga, gamax, gb)
    xl = _odd_out(xs, m1, mix_norm[1], hf, hb, w_in_t, od_norm_g[0], bf(od_w_out[0]))
    out = _ffn(xl, m1, 2, ffn2_norm[1], ffn2_w_gu, ffn2_w_d, 1, N_LAT_TILES,
               final_g=final_norm)
    return out.reshape(BATCH, SEQ, D)
```

```python
import functools

import jax
import jax.numpy as jnp
import numpy as np
from jax import lax
from jax.experimental import pallas as pl
from jax.experimental.pallas import tpu as pltpu

D = 1024
BATCH = 4
SEQ = 4096
DEPTH = 2
GRID_W = 64
CTX = 256
N_MOD = 9
D_FF = 2816
EPS = 1e-6
CONV_DIM = 512
CONV_W = 31
CONV_PAD = 15
HEAD_DIM = 64
ATT_HEADS = 8
KV_HEADS = 2
ATT_DIM = 512
KV_DIM = 128
WINDOW = 128
BLK = 128
ROPE_BASE = 10000.0
ROPE_FREQS = 16
Q_OFF = 2 * CONV_DIM
K_OFF = Q_OFF + ATT_DIM
V_OFF = K_OFF + KV_DIM
EVEN_IN = V_OFF + KV_DIM
ML_HEADS = 4
ML_DH = 256
ML_DIM = 1024
ML_CHUNK = 128

T_LAT = BATCH * SEQ
T_CTX = BATCH * CTX
T_ALL = T_LAT + T_CTX
TM = 1024
N_LAT_TILES = T_LAT // TM
N_ALL_TILES = T_ALL // TM
MOD_ROWS = 8
HALO = 16
NEG = -1e30
VMEM_LIMIT = 56 * 1024 * 1024
VMEM_LIMIT_FFN = 60 * 1024 * 1024

F32 = jnp.float32
BF16 = jnp.bfloat16


def _sigmoid(x):
    return 1.0 / (1.0 + jnp.exp(-x))


def _silu(x):
    return x * _sigmoid(x)


def _dot(a, b, precision=None):
    return jnp.dot(a, b, preferred_element_type=F32, precision=precision)


def _dot_nt(a, b, precision=None):
    return lax.dot_general(a, b, (((1,), (1,)), ((), ())),
                           preferred_element_type=F32, precision=precision)


def _dot_tn(a, b):
    return lax.dot_general(a, b, (((0,), (0,)), ((), ())), preferred_element_type=F32)


def _rms_mod(x, g, shift, scale):
    y = x * lax.rsqrt(jnp.mean(x * x, axis=-1, keepdims=True) + EPS)
    return (y * g) * (1.0 + scale) + shift


def _resident(shape):
    nd = len(shape)
    return pl.BlockSpec(shape, lambda *_: (0,) * nd, pipeline_mode=pl.Buffered(1))


def _params(n_axes=1, vmem_limit=VMEM_LIMIT):
    return pltpu.CompilerParams(dimension_semantics=("arbitrary",) * n_axes,
                                vmem_limit_bytes=vmem_limit)


def _mod_kernel(c_ref, w_ref, b_ref, o_ref):
    s = _silu(c_ref[...]).astype(BF16)
    o_ref[...] = _dot(s, w_ref[...].astype(BF16)) + b_ref[...]


def _ada_mods(cond, mod_w, mod_b):
    tn = 1024
    n = N_MOD * D
    return pl.pallas_call(
        _mod_kernel,
        grid=(DEPTH, n // tn),
        in_specs=[
            pl.BlockSpec((MOD_ROWS, D), lambda l, j: (0, 0)),
            pl.BlockSpec((None, D, tn), lambda l, j: (l, 0, j)),
            pl.BlockSpec((None, 1, tn), lambda l, j: (l, 0, j)),
        ],
        out_specs=pl.BlockSpec((None, MOD_ROWS, tn), lambda l, j: (l, 0, j)),
        out_shape=jax.ShapeDtypeStruct((DEPTH, MOD_ROWS, n), F32),
        compiler_params=_params(2),
        name="ada_mods",
    )(cond, mod_w, mod_b.reshape(DEPTH, 1, n))


def _mod_spec(k):
    return pl.BlockSpec((None, 1, 3 * D), lambda i: (i // (SEQ // TM), 0, k))


FF_CHUNKS = tuple((c, min(c + 512, D_FF)) for c in range(0, D_FF, 512))


STAGE = 256
N_STAGE = 3


def _stage_weights(src_ref, dst_ref, stage_ref, sem_ref, chunks):
    def copy(n):
        slot = n % N_STAGE
        return pltpu.make_async_copy(src_ref.at[chunks[n][0]], stage_ref.at[slot], sem_ref.at[slot])

    for n in range(min(N_STAGE, len(chunks))):
        copy(n).start()
    for n in range(len(chunks)):
        copy(n).wait()
        dst_ref[chunks[n][1]] = stage_ref[n % N_STAGE].astype(BF16)
        if n + N_STAGE < len(chunks):
            copy(n + N_STAGE).start()


N_XBUF = 3
NORM_ROWS = -(-TM // len(FF_CHUNKS) // 16) * 16


def _ffn_kernel(x_hbm, *rest, layer, n_tiles, split, final):
    if split:
        xc_hbm, *rest = rest
    m_ref, mn_ref, g_ref, wgu_hbm, wd_hbm, *rest = rest
    if final:
        fn_ref, *rest = rest
    o_ref, wgu_ref, wd_ref, sgu_ref, sd_ref, xbuf, hbuf, sem_gu, sem_d, sem_x = rest
    i = pl.program_id(0)

    def x_copy(j, go):
        slot = j % N_XBUF
        if split:
            @pl.when(j < N_LAT_TILES)
            def _():
                go(pltpu.make_async_copy(x_hbm.at[pl.ds(j * TM, TM), :], xbuf.at[slot], sem_x.at[slot]))

            @pl.when(j >= N_LAT_TILES)
            def _():
                go(pltpu.make_async_copy(xc_hbm, xbuf.at[slot], sem_x.at[slot]))
        else:
            go(pltpu.make_async_copy(x_hbm.at[pl.ds(j * TM, TM), :], xbuf.at[slot], sem_x.at[slot]))

    start = lambda cp: cp.start()
    wait = lambda cp: cp.wait()

    def normed(x, mod_ref):
        return _rms_mod(x, g_ref[...], mod_ref[:, 0:D], mod_ref[:, D:2 * D]).astype(BF16)

    @pl.when(i == 0)
    def _():
        x_copy(0, start)
        x_copy(1, start)
        _stage_weights(wgu_hbm, wgu_ref, sgu_ref, sem_gu,
                       [((layer, slice(None), pl.ds(c, STAGE)), (slice(None), pl.ds(c, STAGE)))
                        for c in range(0, 2 * D_FF, STAGE)])
        _stage_weights(wd_hbm, wd_ref, sd_ref, sem_d,
                       [((layer, pl.ds(r, STAGE), slice(None)), (pl.ds(r, STAGE), slice(None)))
                        for r in range(0, D_FF, STAGE)])
        x_copy(0, wait)
        hbuf[0] = normed(xbuf[0], m_ref)

    @pl.when(i + 2 < n_tiles)
    def _():
        x_copy(i + 2, start)

    @pl.when(i + 1 < n_tiles)
    def _():
        x_copy(i + 1, wait)

    h_ref = hbuf.at[i % 2]
    hn_ref = hbuf.at[(i + 1) % 2]
    xn_ref = xbuf.at[(i + 1) % N_XBUF]
    acc = None
    for n, (c0, c1) in enumerate(FF_CHUNKS):
        hg = _dot(h_ref[...], wgu_ref[:, c0:c1])
        hu = _dot(h_ref[...], wgu_ref[:, D_FF + c0:D_FF + c1])
        a = (_silu(hg) * hu).astype(BF16)
        p = _dot(a, wd_ref[c0:c1, :])
        acc = p if acc is None else acc + p
        rows = slice(n * NORM_ROWS, min((n + 1) * NORM_ROWS, TM))
        hn_ref[rows, :] = normed(xn_ref[rows, :], mn_ref)
    y = xbuf[i % N_XBUF] + (0.5 * m_ref[:, 2 * D:3 * D]) * acc
    if final:
        y = (y * lax.rsqrt(jnp.mean(y * y, axis=-1, keepdims=True) + EPS)) * fn_ref[...]
    o_ref[...] = y


def _ffn(x, mods, k, norm_g, w_gu, w_d, layer, n_tiles, final_g=None, x_ctx=None):
    final = final_g is not None
    split = x_ctx is not None
    hbm = pl.BlockSpec(memory_space=pl.ANY)
    args = [x, x_ctx] if split else [x]
    in_specs = [hbm] * len(args)
    in_specs += [
        _mod_spec(k),
        pl.BlockSpec((None, 1, 3 * D), lambda i: (jnp.minimum(i + 1, n_tiles - 1) // (SEQ // TM), 0, k)),
        _resident((1, D)),
        hbm,
        hbm,
    ]
    args += [mods, mods, norm_g.reshape(1, D), w_gu, w_d]
    if final:
        in_specs.append(_resident((1, D)))
        args.append(final_g.reshape(1, D))
    return pl.pallas_call(
        functools.partial(_ffn_kernel, layer=layer, n_tiles=n_tiles, split=split, final=final),
        grid=(n_tiles,),
        in_specs=in_specs,
        out_specs=pl.BlockSpec((TM, D), lambda i: (i, 0)),
        out_shape=jax.ShapeDtypeStruct((n_tiles * TM, D), F32),
        scratch_shapes=[pltpu.VMEM((D, 2 * D_FF), BF16),
                        pltpu.VMEM((D_FF, D), BF16),
                        pltpu.VMEM((N_STAGE, D, STAGE), F32),
                        pltpu.VMEM((N_STAGE, STAGE, D), F32),
                        pltpu.VMEM((N_XBUF, TM, D), F32),
                        pltpu.VMEM((2, TM, D), BF16),
                        pltpu.SemaphoreType.DMA((N_STAGE,)),
                        pltpu.SemaphoreType.DMA((N_STAGE,)),
                        pltpu.SemaphoreType.DMA((N_XBUF,))],
        compiler_params=_params(1, VMEM_LIMIT_FFN),
        name="ffn_final" if final else ("ffn_first" if split else "ffn"),
    )(*args)


def _even_in_kernel(x_ref, m_ref, g_ref, wt_ref,
                    c_ref, s1_ref, s2_ref, ct_ref, s1t_ref, s2t_ref,
                    u_ref, qt_ref, k_ref, vt_ref):
    x = x_ref[...]
    h = _rms_mod(x, g_ref[...], m_ref[:, 0:D], m_ref[:, D:2 * D]).astype(BF16)
    vg = _dot_nt(h, wt_ref[0:Q_OFF, :])
    u_ref[...] = vg[:, 0:CONV_DIM] * _sigmoid(vg[:, CONV_DIM:Q_OFF])
    qf = _dot_nt(wt_ref[Q_OFF:K_OFF, :], h)
    ct, s1t, s2t = ct_ref[...], s1t_ref[...], s2t_ref[...]
    for hd in range(ATT_HEADS):
        xh = qf[hd * HEAD_DIM:(hd + 1) * HEAD_DIM]
        up = jnp.concatenate([xh[ROPE_FREQS:], xh[:ROPE_FREQS]], axis=0)
        dn = jnp.concatenate([xh[HEAD_DIM - ROPE_FREQS:], xh[:HEAD_DIM - ROPE_FREQS]], axis=0)
        r = xh * ct + up * s1t + dn * s2t
        qt_ref[hd * HEAD_DIM:(hd + 1) * HEAD_DIM, :] = (r * (HEAD_DIM ** -0.5)).astype(BF16)
    kf = _dot_nt(h, wt_ref[K_OFF:V_OFF, :])
    kr = kf * c_ref[...] + pltpu.roll(kf, 128 - ROPE_FREQS, 1) * s1_ref[...] + pltpu.roll(kf, ROPE_FREQS, 1) * s2_ref[...]
    k_ref[...] = kr.astype(BF16)
    vt_ref[...] = _dot_nt(wt_ref[V_OFF:EVEN_IN, :], h).astype(BF16)


def _even_in(x, mods, norm_g, w_t, rope, rope_t):
    pos = lambda i: jnp.where(i < N_LAT_TILES, i % (SEQ // TM), SEQ // TM)
    rope_spec = lambda: pl.BlockSpec((TM, KV_DIM), lambda i: (pos(i), 0))
    rope_t_spec = lambda: pl.BlockSpec((HEAD_DIM, TM), lambda i: (0, pos(i)))
    row = lambda i: (i, 0)
    col = lambda i: (0, i)
    return pl.pallas_call(
        _even_in_kernel,
        grid=(N_ALL_TILES,),
        in_specs=[
            pl.BlockSpec((TM, D), row),
            _mod_spec(1),
            _resident((1, D)),
            _resident((EVEN_IN, D)),
            rope_spec(), rope_spec(), rope_spec(),
            rope_t_spec(), rope_t_spec(), rope_t_spec(),
        ],
        out_specs=[
            pl.BlockSpec((TM, CONV_DIM), row),
            pl.BlockSpec((ATT_DIM, TM), col),
            pl.BlockSpec((TM, KV_DIM), row),
            pl.BlockSpec((KV_DIM, TM), col),
        ],
        out_shape=[
            jax.ShapeDtypeStruct((T_ALL, CONV_DIM), F32),
            jax.ShapeDtypeStruct((ATT_DIM, T_ALL), BF16),
            jax.ShapeDtypeStruct((T_ALL, KV_DIM), BF16),
            jax.ShapeDtypeStruct((KV_DIM, T_ALL), BF16),
        ],
        compiler_params=_params(1),
        name="even_in",
    )(x, mods, norm_g.reshape(1, D), w_t, *rope, *rope_t)


def _rope_tables():
    rows = SEQ // GRID_W
    inv = ROPE_BASE ** (-jnp.arange(ROPE_FREQS, dtype=F32) / ROPE_FREQS)
    d = np.arange(128) % HEAD_DIM
    first = jnp.asarray(((d % 32) // ROPE_FREQS) == 0)
    by_row = jnp.asarray(d // 32 == 0)

    def expand(fn):
        t_row = fn(jnp.arange(rows, dtype=F32)[:, None] * inv)[:, d % ROPE_FREQS]
        t_col = fn(jnp.arange(GRID_W, dtype=F32)[:, None] * inv)[:, d % ROPE_FREQS]
        return jnp.where(by_row, jnp.repeat(t_row, GRID_W, axis=0), jnp.tile(t_col, (rows, 1)))

    c = expand(jnp.cos)
    s = expand(jnp.sin)
    s1 = jnp.where(first, -s, 0.0)
    s2 = jnp.where(first, 0.0, s)
    pad = lambda t, v: jnp.concatenate([t, jnp.full((TM, 128), v, F32)], axis=0)
    tok = (pad(c, 1.0), pad(s1, 0.0), pad(s2, 0.0))
    return tok, tuple(t[:, :HEAD_DIM].T for t in tok)


NB = SEQ // BLK
QB = 2 * BLK
GROUP = ATT_HEADS // KV_HEADS
assert CTX == QB


def _attn_kernel(qt_ref, kp_ref, kc_ref, kn_ref, kx_ref, vp_ref, vc_ref, vn_ref, vx_ref,
                 sink_ref, o_ref):
    for sub in range(QB // BLK):
        ks = [kp_ref[...], kc_ref[0:BLK], kc_ref[BLK:2 * BLK], kn_ref[...]][sub:sub + 3]
        vs = [vp_ref[...], vc_ref[:, 0:BLK], vc_ref[:, BLK:2 * BLK], vn_ref[...]][sub:sub + 3]
        o_ref[sub * BLK:(sub + 1) * BLK, :] = _attn_block(
            (QB // BLK) * pl.program_id(1) + sub, qt_ref[:, sub * BLK:(sub + 1) * BLK],
            ks + [kx_ref[...]], vs + [vx_ref[...]], sink_ref)


def _attn_block(i, qt, ks, vs, sink_ref):
    kk = lax.broadcasted_iota(jnp.int32, (3 * BLK, BLK), 0)
    qq = lax.broadcasted_iota(jnp.int32, (3 * BLK, BLK), 1)
    rel = kk - BLK - qq
    kpos = (i - 1) * BLK + kk
    ok = (jnp.abs(rel) <= WINDOW) & (kpos >= 0) & (kpos < SEQ) & (i < NB)
    bias1 = jnp.where(ok, 0.0, NEG).astype(F32)
    bias = jnp.concatenate([bias1] * GROUP, axis=1)
    keys = jnp.concatenate(ks, axis=0)
    vt = jnp.concatenate(vs, axis=1)
    zero = jnp.zeros((HEAD_DIM, BLK), BF16)
    outs = []
    for g in range(KV_HEADS):
        cols = []
        for hh in range(GROUP):
            hd = g * GROUP + hh
            qh = qt[hd * HEAD_DIM:(hd + 1) * HEAD_DIM]
            cols.append(jnp.concatenate([qh, zero] if g == 0 else [zero, qh], axis=0))
        s = _dot(keys, jnp.concatenate(cols, axis=1))
        s_b = s[0:3 * BLK] + bias
        s_x = s[3 * BLK:]
        sink = sink_ref[:, g * GROUP * BLK:(g + 1) * GROUP * BLK]
        m = jnp.maximum(jnp.maximum(jnp.max(s_b, axis=0, keepdims=True),
                                    jnp.max(s_x, axis=0, keepdims=True)), sink)
        p_b = jnp.exp(s_b - m)
        p_x = jnp.exp(s_x - m)
        den = (jnp.sum(p_b, axis=0, keepdims=True) + jnp.sum(p_x, axis=0, keepdims=True)
               + jnp.exp(sink - m))
        p = jnp.concatenate([p_b, p_x], axis=0).astype(BF16)
        ot = _dot(vt[g * HEAD_DIM:(g + 1) * HEAD_DIM], p) * (1.0 / den)
        outs += [ot[:, hh * BLK:(hh + 1) * BLK] for hh in range(GROUP)]
    return jnp.concatenate(outs, axis=0).T.astype(BF16)


def _attention(qt, k, vt, sink):
    n_pairs = SEQ // QB
    q_blk = lambda b, j: jnp.where(j < n_pairs, b * n_pairs + j, T_LAT // QB + b)
    pair = lambda b, j: b * n_pairs + jnp.minimum(j, n_pairs - 1)
    edge = lambda off: (lambda b, j: b * NB + jnp.clip((QB // BLK) * j + off, 0, NB - 1))
    ctx_blk = lambda b, j: T_LAT // CTX + b
    k_spec = lambda n, f: pl.BlockSpec((n, KV_DIM), lambda b, j: (f(b, j), 0))
    v_spec = lambda n, f: pl.BlockSpec((KV_DIM, n), lambda b, j: (0, f(b, j)))
    return pl.pallas_call(
        _attn_kernel,
        grid=(BATCH, n_pairs + 1),
        in_specs=[
            pl.BlockSpec((ATT_DIM, QB), lambda b, j: (0, q_blk(b, j))),
            k_spec(BLK, edge(-1)), k_spec(QB, pair), k_spec(BLK, edge(QB // BLK)), k_spec(CTX, ctx_blk),
            v_spec(BLK, edge(-1)), v_spec(QB, pair), v_spec(BLK, edge(QB // BLK)), v_spec(CTX, ctx_blk),
            pl.BlockSpec((1, ATT_HEADS * BLK), lambda b, j: (0, 0)),
        ],
        out_specs=pl.BlockSpec((QB, ATT_DIM), lambda b, j: (q_blk(b, j), 0)),
        out_shape=jax.ShapeDtypeStruct((T_ALL, ATT_DIM), BF16),
        compiler_params=_params(2),
        name="window_attn",
    )(qt, k, k, k, k, vt, vt, vt, vt, sink)


CONV_S = 4
CONV_ROWS = 8 * CONV_S
N_SLAB = CONV_DIM // 128


def _even_out_kernel(x_ref, m_ref, u_ref, up_ref, un_ref, att_ref, cw_ref, cb_ref,
                     lg_ref, lb_ref, wa_ref, wb_ref, o_ref, ext_ref, a_ref, cva_ref, cvb_ref, *, tm, tiles_per_seq):
    t = pl.program_id(0)
    first = (t % tiles_per_seq) == 0
    last = (t % tiles_per_seq) == tiles_per_seq - 1
    slabs = [slice(s * 128, (s + 1) * 128) for s in range(N_SLAB)]
    for s, ls in enumerate(slabs):
        ext_ref[s, 0:HALO, :] = jnp.where(first, 0.0, up_ref[:, ls])
        ext_ref[s, HALO:HALO + tm, :] = u_ref[:, ls]
        ext_ref[s, HALO + tm:, :] = jnp.where(last, 0.0, un_ref[:, ls])

    def conv_block(blk, cv_ref):
        base = blk * CONV_ROWS
        for s, ls in enumerate(slabs):
            acc = [jnp.broadcast_to(cb_ref[:, ls], (8, 128)) for _ in range(CONV_S)]
            for o in range(CONV_W + CONV_S - 1):
                v = ext_ref[s, pl.ds(base + (HALO - CONV_PAD) + o, 8, stride=CONV_S), :]
                for j in range(CONV_S):
                    k = o - j
                    if 0 <= k < CONV_W:
                        acc[j] = acc[j] + cw_ref[k:k + 1, ls] * v
            for j in range(CONV_S):
                cv_ref[s * CONV_S + j] = acc[j]

    def norm_block(blk, cv_ref):
        base = blk * CONV_ROWS
        for j in range(CONV_S):
            row = [cv_ref[s * CONV_S + j] for s in range(N_SLAB)]
            mu = jnp.sum(sum(row), axis=-1, keepdims=True) * (1.0 / CONV_DIM)
            cen = [r - mu for r in row]
            var = jnp.sum(sum(c * c for c in cen), axis=-1, keepdims=True) * (1.0 / CONV_DIM)
            rs = lax.rsqrt(var + EPS)
            for s, ls in enumerate(slabs):
                y = cen[s] * rs * lg_ref[:, ls] + lb_ref[:, ls]
                a_ref[s, pl.ds(base + j, 8, stride=CONV_S), :] = _silu(y)

    n_blk = tm // CONV_ROWS
    conv_block(0, cva_ref)

    def body(i, carry):
        conv_block(2 * i + 1, cvb_ref)
        norm_block(2 * i, cva_ref)
        conv_block(jnp.minimum(2 * i + 2, n_blk - 1), cva_ref)
        norm_block(2 * i + 1, cvb_ref)
        return carry

    lax.fori_loop(0, n_blk // 2, body, 0)
    a = jnp.concatenate([a_ref[s] for s in range(N_SLAB)], axis=1).astype(BF16)
    y = _dot(a, wa_ref[...]) + _dot(att_ref[...], wb_ref[...])
    o_ref[...] = x_ref[...] + m_ref[:, 2 * D:3 * D] * y


def _even_out(x, mods, u, att, conv_w, conv_b, ln_g, ln_b, wa, wb, *, tm, row0, n_tiles,
              tiles_per_seq, mod_row):
    blk0 = row0 // tm
    hb = tm // HALO
    n_halo = T_ALL // HALO
    row_map = lambda i: (blk0 + i, 0)
    in_specs = [
        pl.BlockSpec((tm, D), row_map),
        pl.BlockSpec((None, 1, 3 * D), lambda i: (mod_row(i), 0, 1)),
        pl.BlockSpec((tm, CONV_DIM), row_map),
        pl.BlockSpec((HALO, CONV_DIM), lambda i: (jnp.maximum((blk0 + i) * hb - 1, 0), 0)),
        pl.BlockSpec((HALO, CONV_DIM), lambda i: (jnp.minimum((blk0 + i + 1) * hb, n_halo - 1), 0)),
        pl.BlockSpec((tm, ATT_DIM), row_map),
        _resident((32, CONV_DIM)),
        _resident((1, CONV_DIM)),
        _resident((1, CONV_DIM)),
        _resident((1, CONV_DIM)),
        _resident((CONV_DIM, D)),
        _resident((ATT_DIM, D)),
    ]
    return pl.pallas_call(
        functools.partial(_even_out_kernel, tm=tm, tiles_per_seq=tiles_per_seq),
        grid=(n_tiles,),
        in_specs=in_specs,
        out_specs=pl.BlockSpec((tm, D), row_map),
        out_shape=jax.ShapeDtypeStruct((T_ALL, D), F32),
        scratch_shapes=[pltpu.VMEM((N_SLAB, tm + 2 * HALO, 128), F32),
                        pltpu.VMEM((N_SLAB, tm, 128), F32),
                        pltpu.VMEM((N_SLAB * CONV_S, 8, 128), F32),
                        pltpu.VMEM((N_SLAB * CONV_S, 8, 128), F32)],
        input_output_aliases={0: 0},
        compiler_params=_params(1),
        name="even_out" if tiles_per_seq > 1 else "even_out_ctx",
    )(x, mods, u, u, u, att, conv_w, conv_b, ln_g, ln_b, wa, wb)


N_GATE = 4 * ML_HEADS
N_CHAIN = 2 * ML_HEADS
L = ML_CHUNK


def _log_sigmoid(x):
    return jnp.minimum(x, 0.0) - jnp.log(1.0 + jnp.exp(-jnp.abs(x)))


def _lane_scan(x, lane, fwd_rows, combine, fill):
    sh = 1
    while sh < L:
        pre = jnp.where(lane >= sh, pltpu.roll(x, sh, 1), fill)
        suf = jnp.where(lane < L - sh, pltpu.roll(x, L - sh, 1), fill)
        x = combine(x, jnp.where(fwd_rows, pre, suf))
        sh *= 2
    return x


def _odd_in_kernel(x_ref, m_ref, g_ref, wqt_ref, wkt_ref, wvt_ref, bgt_ref,
                   qt_ref, k_ref, vt_ref, a_ref, amax_ref, b_ref):
    x = x_ref[...]
    hf = _rms_mod(x, g_ref[...], m_ref[:, 0:D], m_ref[:, D:2 * D])
    h = hf.astype(BF16)
    h_lo = (hf - h.astype(F32)).astype(BF16)
    qg = _dot_nt(wqt_ref[...], h)
    qt_ref[...] = qg[0:ML_DIM].astype(BF16)
    g = (qg[ML_DIM:ML_DIM + N_GATE] + qg[ML_DIM + N_GATE:]
         + _dot_nt(wqt_ref[ML_DIM:ML_DIM + N_GATE, :], h_lo) + bgt_ref[...])
    k_ref[...] = (_dot_nt(h, wkt_ref[...]) * (ML_DH ** -0.5)).astype(BF16)
    vt_ref[...] = _dot_nt(wvt_ref[...], h).astype(BF16)
    li = g[0:N_CHAIN]
    lf = _log_sigmoid(g[N_CHAIN:N_GATE])
    fwd_rows = lax.broadcasted_iota(jnp.int32, (N_CHAIN, L), 0) < ML_HEADS
    lane = lax.broadcasted_iota(jnp.int32, (N_CHAIN, L), 1)
    for ch in range(TM // L):
        cs = slice(ch * L, (ch + 1) * L)
        b = _lane_scan(lf[:, cs], lane, fwd_rows, jnp.add, 0.0)
        a = li[:, cs] - b
        a_ref[:, cs] = a
        amax_ref[:, cs] = _lane_scan(a, lane, fwd_rows, jnp.maximum, NEG)
        b_ref[:, cs] = b


def _w_in_t_block(j):
    return pl.BlockSpec((ML_DIM, D), lambda i: (j, 0), pipeline_mode=pl.Buffered(1))


def _odd_in(x, mods, norm_g, w_qg_t, w_in_t, b_gate_t):
    row = lambda i: (i, 0)
    col = lambda i: (0, i)
    return pl.pallas_call(
        _odd_in_kernel,
        grid=(N_ALL_TILES,),
        in_specs=[
            pl.BlockSpec((TM, D), row),
            _mod_spec(1),
            _resident((1, D)),
            _resident((ML_DIM + 2 * N_GATE, D)),
            _w_in_t_block(1),
            _w_in_t_block(2),
            _resident((N_GATE, 1)),
        ],
        out_specs=[
            pl.BlockSpec((ML_DIM, TM), col),
            pl.BlockSpec((TM, ML_DIM), row),
            pl.BlockSpec((ML_DIM, TM), col),
            pl.BlockSpec((N_CHAIN, TM), col),
            pl.BlockSpec((N_CHAIN, TM), col),
            pl.BlockSpec((N_CHAIN, TM), col),
        ],
        out_shape=[
            jax.ShapeDtypeStruct((ML_DIM, T_ALL), BF16),
            jax.ShapeDtypeStruct((T_ALL, ML_DIM), BF16),
            jax.ShapeDtypeStruct((ML_DIM, T_ALL), BF16),
            jax.ShapeDtypeStruct((N_CHAIN, T_ALL), F32),
            jax.ShapeDtypeStruct((N_CHAIN, T_ALL), F32),
            jax.ShapeDtypeStruct((N_CHAIN, T_ALL), F32),
        ],
        compiler_params=_params(1),
        name="odd_in",
    )(x, mods, norm_g.reshape(1, D), w_qg_t, w_in_t, w_in_t, b_gate_t)


SUBS = 2
ML_BLK = SUBS * ML_CHUNK
N_CTX_CHUNKS = CTX // ML_BLK
N_LAT_CHUNKS = SEQ // ML_BLK
N_STEPS = N_CTX_CHUNKS + N_LAT_CHUNKS
N_AUG = 16


def _mlstm_kernel(qtf_ref, kf_ref, vtf_ref, qtb_ref, kb_ref, vtb_ref,
                  af_ref, amaxf_ref, bf_ref, ab_ref, amaxb_ref, bb_ref, hf_ref, hb_ref, c_ref, m_ref):
    t = pl.program_id(1)

    @pl.when(t == 0)
    def _():
        c_ref[...] = jnp.zeros_like(c_ref)
        m_ref[...] = jnp.zeros_like(m_ref)

    for sub in range(SUBS):
        _mlstm_chunk(slice(sub * L, (sub + 1) * L), slice((SUBS - 1 - sub) * L, (SUBS - sub) * L),
                     qtf_ref, kf_ref, vtf_ref, qtb_ref, kb_ref, vtb_ref,
                     af_ref, amaxf_ref, bf_ref, ab_ref, amaxb_ref, bb_ref, hf_ref, hb_ref, c_ref, m_ref)


def _mlstm_chunk(fs, bs, qtf_ref, kf_ref, vtf_ref, qtb_ref, kb_ref, vtb_ref,
                 af_ref, amaxf_ref, bf_ref, ab_ref, amaxb_ref, bb_ref, hf_ref, hb_ref, c_ref, m_ref):
    fwd_rows = lax.broadcasted_iota(jnp.int32, (N_CHAIN, L), 0) < ML_HEADS
    a = jnp.where(fwd_rows, af_ref[:, fs], ab_ref[:, bs])
    amax = jnp.where(fwd_rows, amaxf_ref[:, fs], amaxb_ref[:, bs])
    b = jnp.where(fwd_rows, bf_ref[:, fs], bb_ref[:, bs])
    m_old = m_ref[:, 0:1]
    big = jnp.maximum(m_old, jnp.max(amax, axis=1, keepdims=True))
    decay = jnp.exp(m_old - big)
    e = jnp.exp(a - big)
    mm = jnp.maximum(amax, m_old)
    w_inter = jnp.exp(m_old - mm)
    floor = jnp.exp(-(b + mm))
    m_ref[...] = jnp.broadcast_to(jnp.min(b, axis=1, keepdims=True) + big, (N_CHAIN, 128))
    kq = lax.broadcasted_iota(jnp.int32, (L, L), 0)
    qq = lax.broadcasted_iota(jnp.int32, (L, L), 1)
    a_col = jnp.concatenate([a, jnp.zeros((L - N_CHAIN, L), F32)], axis=0).T

    for d, (qt_ref, k_ref, vt_ref, h_ref, ts) in enumerate(
            ((qtf_ref, kf_ref, vtf_ref, hf_ref, fs), (qtb_ref, kb_ref, vtb_ref, hb_ref, bs))):
        visible = (kq >= qq) if d == 1 else (kq <= qq)
        for hd in range(ML_HEADS):
            c = d * ML_HEADS + hd
            cs = slice(hd * ML_DH, (hd + 1) * ML_DH)
            qt = qt_ref[cs, ts]
            k = k_ref[ts, cs]
            vt = vt_ref[cs, ts]
            p = jnp.where(visible, jnp.exp(a_col[:, c:c + 1] - mm[c:c + 1, :]), 0.0)
            c_old = c_ref[c]
            kcq = _dot(jnp.concatenate([k, c_old.astype(BF16)], axis=0), qt)
            st = kcq[0:L] * p
            cq = kcq[L:]
            wi = w_inter[c:c + 1, :]
            num = _dot(vt, st.astype(BF16)) + wi * cq[0:ML_DH]
            den = jnp.sum(st, axis=0, keepdims=True) + wi * cq[ML_DH:ML_DH + 1]
            ht = num * (1.0 / jnp.maximum(jnp.abs(den), floor[c:c + 1, :]))
            h_ref[ts, cs] = ht.T.astype(BF16)
            er = e[c:c + 1, :]
            vte = jnp.concatenate([vt.astype(F32) * er, jnp.broadcast_to(er, (N_AUG, L))], axis=0)
            c_ref[c] = decay[c:c + 1, :] * c_old + _dot(vte.astype(BF16), k)


def _mlstm(qt, k, vt, a, amax, b):
    lat_chunks = T_LAT // ML_BLK

    def fwd_in(b, t):
        return jnp.where(t < N_CTX_CHUNKS, lat_chunks + b * N_CTX_CHUNKS + t,
                         b * N_LAT_CHUNKS + (t - N_CTX_CHUNKS))

    def bwd_in(b, t):
        return jnp.where(t < N_CTX_CHUNKS, lat_chunks + b * N_CTX_CHUNKS + (N_CTX_CHUNKS - 1 - t),
                         b * N_LAT_CHUNKS + (N_STEPS - 1 - t))

    def fwd_out(b, t):
        return b * N_LAT_CHUNKS + jnp.maximum(t - N_CTX_CHUNKS, 0)

    def bwd_out(b, t):
        return b * N_LAT_CHUNKS + (N_STEPS - 1 - jnp.maximum(t, N_CTX_CHUNKS))

    rows = lambda f: pl.BlockSpec((ML_BLK, ML_DIM), lambda b, t: (f(b, t), 0))
    cols = lambda f: pl.BlockSpec((ML_DIM, ML_BLK), lambda b, t: (0, f(b, t)))
    gate = lambda f: pl.BlockSpec((N_CHAIN, ML_BLK), lambda b, t: (0, f(b, t)))
    return pl.pallas_call(
        _mlstm_kernel,
        grid=(BATCH, N_STEPS),
        in_specs=[cols(fwd_in), rows(fwd_in), cols(fwd_in),
                  cols(bwd_in), rows(bwd_in), cols(bwd_in),
                  gate(fwd_in), gate(fwd_in), gate(fwd_in), gate(bwd_in), gate(bwd_in), gate(bwd_in)],
        out_specs=[rows(fwd_out), rows(bwd_out)],
        out_shape=[jax.ShapeDtypeStruct((T_LAT, ML_DIM), BF16),
                   jax.ShapeDtypeStruct((T_LAT, ML_DIM), BF16)],
        scratch_shapes=[pltpu.VMEM((N_CHAIN, ML_DH + N_AUG, ML_DH), F32),
                        pltpu.VMEM((N_CHAIN, 128), F32)],
        compiler_params=_params(2),
        name="mlstm",
    )(qt, k, vt, qt, k, vt, a, amax, b, a, amax, b)


def _odd_out_kernel(x_ref, m_ref, g_ref, hf_ref, hb_ref, wot_ref, ng_ref, wout_ref, o_ref):
    x = x_ref[...]
    h = _rms_mod(x, g_ref[...], m_ref[:, 0:D], m_ref[:, D:2 * D]).astype(BF16)
    o = _sigmoid(_dot_nt(h, wot_ref[...]))
    hs = hf_ref[...].astype(F32) + hb_ref[...].astype(F32)
    parts = []
    for hd in range(ML_HEADS):
        p = hs[:, hd * ML_DH:(hd + 1) * ML_DH]
        parts.append(p * lax.rsqrt(jnp.mean(p * p, axis=-1, keepdims=True) + EPS))
    hn = jnp.concatenate(parts, axis=-1) * ng_ref[...]
    y = _dot((o * hn).astype(BF16), wout_ref[...])
    o_ref[...] = x + m_ref[:, 2 * D:3 * D] * y


def _odd_out(x, mods, norm_g, hf, hb, w_in_t, head_g, w_out):
    row = lambda i: (i, 0)
    return pl.pallas_call(
        _odd_out_kernel,
        grid=(N_LAT_TILES,),
        in_specs=[
            pl.BlockSpec((TM, D), row),
            _mod_spec(1),
            _resident((1, D)),
            pl.BlockSpec((TM, ML_DIM), row),
            pl.BlockSpec((TM, ML_DIM), row),
            _w_in_t_block(3),
            _resident((1, ML_DIM)),
            _resident((ML_DIM, D)),
        ],
        out_specs=pl.BlockSpec((TM, D), row),
        out_shape=jax.ShapeDtypeStruct((T_LAT, D), F32),
        compiler_params=_params(1),
        name="odd_out",
    )(x, mods, norm_g.reshape(1, D), hf, hb, w_in_t, head_g.reshape(1, ML_DIM), w_out)


def kernel(x, c, ctx, c_ctx, mod_w, mod_b, ffn1_norm, ffn1_w_gu, ffn1_w_d, mix_norm, ffn2_norm,
           ffn2_w_gu, ffn2_w_d, ev_w_in, ev_conv_w, ev_conv_b, ev_conv_ln_g, ev_conv_ln_b, ev_sink,
           ev_w_out, od_w_in, od_b_gate, od_norm_g, od_w_out, final_norm):
    assert DEPTH == 2 and x.shape == (BATCH, SEQ, D) and ctx.shape == (BATCH, CTX, D)
    cond = jnp.zeros((MOD_ROWS, D), F32).at[:BATCH].set(c).at[BATCH].set(c_ctx)
    mods = _ada_mods(cond, mod_w, mod_b).reshape(DEPTH, MOD_ROWS, 1, N_MOD * D)
    bf = lambda w: w.astype(BF16)

    m0 = mods[0]
    xs = _ffn(x.reshape(T_LAT, D), m0, 0, ffn1_norm[0], ffn1_w_gu, ffn1_w_d, 0, N_ALL_TILES,
              x_ctx=ctx.reshape(T_CTX, D))
    w_in = ev_w_in[0]
    rope, rope_t = _rope_tables()
    u, qt, kk, vt = _even_in(xs, m0, mix_norm[0], bf(w_in).T, rope, rope_t)
    sink = jnp.repeat(ev_sink[0].astype(F32), BLK).reshape(1, ATT_HEADS * BLK)
    att = _attention(qt, kk, vt, sink)
    conv_w = jnp.concatenate([ev_conv_w[0], jnp.zeros((1, CONV_DIM), F32)], axis=0)
    ev_args = (conv_w, ev_conv_b[0].reshape(1, -1), ev_conv_ln_g[0].reshape(1, -1),
               ev_conv_ln_b[0].reshape(1, -1), bf(ev_w_out[0][:CONV_DIM]), bf(ev_w_out[0][CONV_DIM:]))
    xs = _even_out(xs, m0, u, att, *ev_args, tm=TM, row0=0, n_tiles=N_LAT_TILES,
                   tiles_per_seq=SEQ // TM, mod_row=lambda i: i // (SEQ // TM))
    xs = _even_out(xs, m0, u, att, *ev_args, tm=CTX, row0=T_LAT, n_tiles=BATCH,
                   tiles_per_seq=1, mod_row=lambda i: BATCH)
    xs = _ffn(xs, m0, 2, ffn2_norm[0], ffn2_w_gu, ffn2_w_d, 0, N_ALL_TILES)

    m1 = mods[1]
    xs = _ffn(xs, m1, 0, ffn1_norm[1], ffn1_w_gu, ffn1_w_d, 1, N_ALL_TILES)
    w_in = od_w_in[0]
    perm = np.concatenate([np.arange(0, 4), np.arange(8, 12), np.arange(4, 8), np.arange(12, 16)])
    w_gate_t = w_in[:, 4 * ML_DIM:].T[perm]
    b_gate_t = od_b_gate[0][perm].reshape(N_GATE, 1)
    w_gate_hi = bf(w_gate_t)
    w_gate_lo = bf(w_gate_t - w_gate_hi.astype(F32))
    w_in_t = bf(w_in[:, :4 * ML_DIM]).T
    w_qg_t = jnp.concatenate([w_in_t[:ML_DIM], w_gate_hi, w_gate_lo], axis=0)
    qt, km, vt, ga, gamax, gb = _odd_in(xs, m1, mix_norm[1], w_qg_t, w_in_t, b_gate_t)
    hf, hb = _mlstm(qt, km, vt, ga, gamax, gb)
    xl = _odd_out(xs, m1, mix_norm[1], hf, hb, w_in_t, od_norm_g[0], bf(od_w_out[0]))
    out = _ffn(xl, m1, 2, ffn2_norm[1], ffn2_w_gu, ffn2_w_d, 1, N_LAT_TILES,
               final_g=final_norm)
    return out.reshape(BATCH, SEQ, D)
```

```python
import functools

import jax
import jax.numpy as jnp
import numpy as np
from jax import lax
from jax.experimental import pallas as pl
from jax.experimental.pallas import tpu as pltpu

D = 1024
BATCH = 4
SEQ = 4096
DEPTH = 2
GRID_W = 64
CTX = 256
N_MOD = 9
D_FF = 2816
EPS = 1e-6
CONV_DIM = 512
CONV_W = 31
CONV_PAD = 15
HEAD_DIM = 64
ATT_HEADS = 8
KV_HEADS = 2
ATT_DIM = 512
KV_DIM = 128
WINDOW = 128
BLK = 128
ROPE_BASE = 10000.0
ROPE_FREQS = 16
Q_OFF = 2 * CONV_DIM
K_OFF = Q_OFF + ATT_DIM
V_OFF = K_OFF + KV_DIM
EVEN_IN = V_OFF + KV_DIM
ML_HEADS = 4
ML_DH = 256
ML_DIM = 1024
ML_CHUNK = 128

T_LAT = BATCH * SEQ
T_CTX = BATCH * CTX
T_ALL = T_LAT + T_CTX
TM = 1024
N_LAT_TILES = T_LAT // TM
N_ALL_TILES = T_ALL // TM
MOD_ROWS = 8
HALO = 16
NEG = -1e30
VMEM_LIMIT = 56 * 1024 * 1024
VMEM_LIMIT_FFN = 60 * 1024 * 1024

F32 = jnp.float32
BF16 = jnp.bfloat16


def _sigmoid(x):
    return 1.0 / (1.0 + jnp.exp(-x))


def _silu(x):
    return x * _sigmoid(x)


def _dot(a, b, precision=None):
    return jnp.dot(a, b, preferred_element_type=F32, precision=precision)


def _dot_nt(a, b, precision=None):
    return lax.dot_general(a, b, (((1,), (1,)), ((), ())),
                           preferred_element_type=F32, precision=precision)


def _rms_mod(x, g, shift, scale):
    y = x * lax.rsqrt(jnp.mean(x * x, axis=-1, keepdims=True) + EPS)
    return (y * g) * (1.0 + scale) + shift


def _resident(shape):
    nd = len(shape)
    return pl.BlockSpec(shape, lambda *_: (0,) * nd, pipeline_mode=pl.Buffered(1))


def _params(n_axes=1, vmem_limit=VMEM_LIMIT):
    return pltpu.CompilerParams(dimension_semantics=("arbitrary",) * n_axes,
                                vmem_limit_bytes=vmem_limit)


def _mod_kernel(c_ref, w_ref, b_ref, o_ref):
    s = _silu(c_ref[...]).astype(BF16)
    o_ref[...] = _dot(s, w_ref[...].astype(BF16)) + b_ref[...]


def _ada_mods(cond, mod_w, mod_b):
    tn = 1024
    n = N_MOD * D
    return pl.pallas_call(
        _mod_kernel,
        grid=(DEPTH, n // tn),
        in_specs=[
            pl.BlockSpec((MOD_ROWS, D), lambda l, j: (0, 0)),
            pl.BlockSpec((None, D, tn), lambda l, j: (l, 0, j)),
            pl.BlockSpec((None, 1, tn), lambda l, j: (l, 0, j)),
        ],
        out_specs=pl.BlockSpec((None, MOD_ROWS, tn), lambda l, j: (l, 0, j)),
        out_shape=jax.ShapeDtypeStruct((DEPTH, MOD_ROWS, n), F32),
        compiler_params=_params(2),
        name="ada_mods",
    )(cond, mod_w, mod_b.reshape(DEPTH, 1, n))


def _mod_spec(k):
    return pl.BlockSpec((None, 1, 3 * D), lambda i: (i // (SEQ // TM), 0, k))


FF_CHUNKS = tuple((c, c + 256) for c in range(0, D_FF, 256))


STAGE = 256
N_STAGE = 3


def _stage_weights(src_ref, dst_ref, stage_ref, sem_ref, chunks):
    def copy(n):
        slot = n % N_STAGE
        return pltpu.make_async_copy(src_ref.at[chunks[n][0]], stage_ref.at[slot], sem_ref.at[slot])

    for n in range(min(N_STAGE, len(chunks))):
        copy(n).start()
    for n in range(len(chunks)):
        copy(n).wait()
        dst_ref[chunks[n][1]] = stage_ref[n % N_STAGE].astype(BF16)
        if n + N_STAGE < len(chunks):
            copy(n + N_STAGE).start()


def _ffn_kernel(x_ref, *rest, layer, split, final):
    if split:
        xc_ref, *rest = rest
    m_ref, g_ref, wgu_hbm, wd_hbm, *rest = rest
    if final:
        fn_ref, *rest = rest
    o_ref, wgu_ref, wd_ref, sgu_ref, sd_ref, sem_gu, sem_d = rest

    @pl.when(pl.program_id(0) == 0)
    def _():
        _stage_weights(wgu_hbm, wgu_ref, sgu_ref, sem_gu,
                       [((layer, slice(None), pl.ds(c, STAGE)), (slice(None), pl.ds(c, STAGE)))
                        for c in range(0, 2 * D_FF, STAGE)])
        _stage_weights(wd_hbm, wd_ref, sd_ref, sem_d,
                       [((layer, pl.ds(r, STAGE), slice(None)), (pl.ds(r, STAGE), slice(None)))
                        for r in range(0, D_FF, STAGE)])

    x = x_ref[...]
    if split:
        x = jnp.where(pl.program_id(0) < N_LAT_TILES, x, xc_ref[...])
    shift = m_ref[:, 0:D]
    scale = m_ref[:, D:2 * D]
    gate = m_ref[:, 2 * D:3 * D]
    h = _rms_mod(x, g_ref[...], shift, scale).astype(BF16)
    acc = None
    for c0, c1 in FF_CHUNKS:
        hg = _dot(h, wgu_ref[:, c0:c1])
        hu = _dot(h, wgu_ref[:, D_FF + c0:D_FF + c1])
        a = (_silu(hg) * hu).astype(BF16)
        p = _dot(a, wd_ref[c0:c1, :])
        acc = p if acc is None else acc + p
    y = x + (0.5 * gate) * acc
    if final:
        y = (y * lax.rsqrt(jnp.mean(y * y, axis=-1, keepdims=True) + EPS)) * fn_ref[...]
    o_ref[...] = y


def _ffn(x, mods, k, norm_g, w_gu, w_d, layer, n_tiles, final_g=None, x_ctx=None):
    final = final_g is not None
    split = x_ctx is not None
    if split:
        in_specs = [pl.BlockSpec((TM, D), lambda i: (jnp.minimum(i, N_LAT_TILES - 1), 0)),
                    _resident((TM, D))]
        args = [x, x_ctx]
    else:
        in_specs = [pl.BlockSpec((TM, D), lambda i: (i, 0))]
        args = [x]
    in_specs += [
        _mod_spec(k),
        _resident((1, D)),
        pl.BlockSpec(memory_space=pl.ANY),
        pl.BlockSpec(memory_space=pl.ANY),
    ]
    args += [mods, norm_g.reshape(1, D), w_gu, w_d]
    if final:
        in_specs.append(_resident((1, D)))
        args.append(final_g.reshape(1, D))
    return pl.pallas_call(
        functools.partial(_ffn_kernel, layer=layer, split=split, final=final),
        grid=(n_tiles,),
        in_specs=in_specs,
        out_specs=pl.BlockSpec((TM, D), lambda i: (i, 0)),
        out_shape=jax.ShapeDtypeStruct((n_tiles * TM, D), F32),
        scratch_shapes=[pltpu.VMEM((D, 2 * D_FF), BF16),
                        pltpu.VMEM((D_FF, D), BF16),
                        pltpu.VMEM((N_STAGE, D, STAGE), F32),
                        pltpu.VMEM((N_STAGE, STAGE, D), F32),
                        pltpu.SemaphoreType.DMA((N_STAGE,)),
                        pltpu.SemaphoreType.DMA((N_STAGE,))],
        compiler_params=_params(1, VMEM_LIMIT_FFN),
        name="ffn_final" if final else ("ffn_first" if split else "ffn"),
    )(*args)


def _even_in_kernel(x_ref, m_ref, g_ref, wt_ref,
                    c_ref, s1_ref, s2_ref, ct_ref, s1t_ref, s2t_ref,
                    u_ref, qt_ref, k_ref, vt_ref):
    x = x_ref[...]
    h = _rms_mod(x, g_ref[...], m_ref[:, 0:D], m_ref[:, D:2 * D]).astype(BF16)
    vg = _dot_nt(h, wt_ref[0:Q_OFF, :])
    u_ref[...] = vg[:, 0:CONV_DIM] * _sigmoid(vg[:, CONV_DIM:Q_OFF])
    qf = _dot_nt(wt_ref[Q_OFF:K_OFF, :], h)
    ct, s1t, s2t = ct_ref[...], s1t_ref[...], s2t_ref[...]
    for hd in range(ATT_HEADS):
        xh = qf[hd * HEAD_DIM:(hd + 1) * HEAD_DIM]
        up = jnp.concatenate([xh[ROPE_FREQS:], xh[:ROPE_FREQS]], axis=0)
        dn = jnp.concatenate([xh[HEAD_DIM - ROPE_FREQS:], xh[:HEAD_DIM - ROPE_FREQS]], axis=0)
        r = xh * ct + up * s1t + dn * s2t
        qt_ref[hd * HEAD_DIM:(hd + 1) * HEAD_DIM, :] = (r * (HEAD_DIM ** -0.5)).astype(BF16)
    kf = _dot_nt(h, wt_ref[K_OFF:V_OFF, :])
    kr = kf * c_ref[...] + pltpu.roll(kf, 128 - ROPE_FREQS, 1) * s1_ref[...] + pltpu.roll(kf, ROPE_FREQS, 1) * s2_ref[...]
    k_ref[...] = kr.astype(BF16)
    vt_ref[...] = _dot_nt(wt_ref[V_OFF:EVEN_IN, :], h).astype(BF16)


def _even_in(x, mods, norm_g, w_t, rope, rope_t):
    pos = lambda i: jnp.where(i < N_LAT_TILES, i % (SEQ // TM), SEQ // TM)
    rope_spec = lambda: pl.BlockSpec((TM, KV_DIM), lambda i: (pos(i), 0))
    rope_t_spec = lambda: pl.BlockSpec((HEAD_DIM, TM), lambda i: (0, pos(i)))
    row = lambda i: (i, 0)
    col = lambda i: (0, i)
    return pl.pallas_call(
        _even_in_kernel,
        grid=(N_ALL_TILES,),
        in_specs=[
            pl.BlockSpec((TM, D), row),
            _mod_spec(1),
            _resident((1, D)),
            _resident((EVEN_IN, D)),
            rope_spec(), rope_spec(), rope_spec(),
            rope_t_spec(), rope_t_spec(), rope_t_spec(),
        ],
        out_specs=[
            pl.BlockSpec((TM, CONV_DIM), row),
            pl.BlockSpec((ATT_DIM, TM), col),
            pl.BlockSpec((TM, KV_DIM), row),
            pl.BlockSpec((KV_DIM, TM), col),
        ],
        out_shape=[
            jax.ShapeDtypeStruct((T_ALL, CONV_DIM), F32),
            jax.ShapeDtypeStruct((ATT_DIM, T_ALL), BF16),
            jax.ShapeDtypeStruct((T_ALL, KV_DIM), BF16),
            jax.ShapeDtypeStruct((KV_DIM, T_ALL), BF16),
        ],
        compiler_params=_params(1),
        name="even_in",
    )(x, mods, norm_g.reshape(1, D), w_t, *rope, *rope_t)


def _rope_tables():
    rows = SEQ // GRID_W
    inv = ROPE_BASE ** (-jnp.arange(ROPE_FREQS, dtype=F32) / ROPE_FREQS)
    d = np.arange(128) % HEAD_DIM
    first = jnp.asarray(((d % 32) // ROPE_FREQS) == 0)
    by_row = jnp.asarray(d // 32 == 0)

    def expand(fn):
        t_row = fn(jnp.arange(rows, dtype=F32)[:, None] * inv)[:, d % ROPE_FREQS]
        t_col = fn(jnp.arange(GRID_W, dtype=F32)[:, None] * inv)[:, d % ROPE_FREQS]
        return jnp.where(by_row, jnp.repeat(t_row, GRID_W, axis=0), jnp.tile(t_col, (rows, 1)))

    c = expand(jnp.cos)
    s = expand(jnp.sin)
    s1 = jnp.where(first, -s, 0.0)
    s2 = jnp.where(first, 0.0, s)
    pad = lambda t, v: jnp.concatenate([t, jnp.full((TM, 128), v, F32)], axis=0)
    tok = (pad(c, 1.0), pad(s1, 0.0), pad(s2, 0.0))
    return tok, tuple(t[:, :HEAD_DIM].T for t in tok)


NB = SEQ // BLK
QB = 2 * BLK
GROUP = ATT_HEADS // KV_HEADS
assert CTX == QB


def _attn_kernel(qt_ref, kp_ref, kc_ref, kn_ref, kx_ref, vp_ref, vc_ref, vn_ref, vx_ref,
                 sink_ref, o_ref):
    for sub in range(QB // BLK):
        ks = [kp_ref[...], kc_ref[0:BLK], kc_ref[BLK:2 * BLK], kn_ref[...]][sub:sub + 3]
        vs = [vp_ref[...], vc_ref[:, 0:BLK], vc_ref[:, BLK:2 * BLK], vn_ref[...]][sub:sub + 3]
        o_ref[sub * BLK:(sub + 1) * BLK, :] = _attn_block(
            (QB // BLK) * pl.program_id(1) + sub, qt_ref[:, sub * BLK:(sub + 1) * BLK],
            ks + [kx_ref[...]], vs + [vx_ref[...]], sink_ref)


def _attn_block(i, qt, ks, vs, sink_ref):
    kk = lax.broadcasted_iota(jnp.int32, (3 * BLK, BLK), 0)
    qq = lax.broadcasted_iota(jnp.int32, (3 * BLK, BLK), 1)
    rel = kk - BLK - qq
    kpos = (i - 1) * BLK + kk
    ok = (jnp.abs(rel) <= WINDOW) & (kpos >= 0) & (kpos < SEQ) & (i < NB)
    bias1 = jnp.where(ok, 0.0, NEG).astype(F32)
    bias = jnp.concatenate([bias1] * GROUP, axis=1)
    keys = jnp.concatenate(ks, axis=0)
    vt = jnp.concatenate(vs, axis=1)
    zero = jnp.zeros((HEAD_DIM, BLK), BF16)
    outs = []
    for g in range(KV_HEADS):
        cols = []
        for hh in range(GROUP):
            hd = g * GROUP + hh
            qh = qt[hd * HEAD_DIM:(hd + 1) * HEAD_DIM]
            cols.append(jnp.concatenate([qh, zero] if g == 0 else [zero, qh], axis=0))
        s = _dot(keys, jnp.concatenate(cols, axis=1))
        s_b = s[0:3 * BLK] + bias
        s_x = s[3 * BLK:]
        sink = sink_ref[:, g * GROUP * BLK:(g + 1) * GROUP * BLK]
        m = jnp.maximum(jnp.maximum(jnp.max(s_b, axis=0, keepdims=True),
                                    jnp.max(s_x, axis=0, keepdims=True)), sink)
        p_b = jnp.exp(s_b - m)
        p_x = jnp.exp(s_x - m)
        den = (jnp.sum(p_b, axis=0, keepdims=True) + jnp.sum(p_x, axis=0, keepdims=True)
               + jnp.exp(sink - m))
        p = jnp.concatenate([p_b, p_x], axis=0).astype(BF16)
        ot = _dot(vt[g * HEAD_DIM:(g + 1) * HEAD_DIM], p) * (1.0 / den)
        outs += [ot[:, hh * BLK:(hh + 1) * BLK] for hh in range(GROUP)]
    return jnp.concatenate(outs, axis=0).T.astype(BF16)


def _attention(qt, k, vt, sink):
    n_pairs = SEQ // QB
    q_blk = lambda b, j: jnp.where(j < n_pairs, b * n_pairs + j, T_LAT // QB + b)
    pair = lambda b, j: b * n_pairs + jnp.minimum(j, n_pairs - 1)
    edge = lambda off: (lambda b, j: b * NB + jnp.clip((QB // BLK) * j + off, 0, NB - 1))
    ctx_blk = lambda b, j: T_LAT // CTX + b
    k_spec = lambda n, f: pl.BlockSpec((n, KV_DIM), lambda b, j: (f(b, j), 0))
    v_spec = lambda n, f: pl.BlockSpec((KV_DIM, n), lambda b, j: (0, f(b, j)))
    return pl.pallas_call(
        _attn_kernel,
        grid=(BATCH, n_pairs + 1),
        in_specs=[
            pl.BlockSpec((ATT_DIM, QB), lambda b, j: (0, q_blk(b, j))),
            k_spec(BLK, edge(-1)), k_spec(QB, pair), k_spec(BLK, edge(QB // BLK)), k_spec(CTX, ctx_blk),
            v_spec(BLK, edge(-1)), v_spec(QB, pair), v_spec(BLK, edge(QB // BLK)), v_spec(CTX, ctx_blk),
            pl.BlockSpec((1, ATT_HEADS * BLK), lambda b, j: (0, 0)),
        ],
        out_specs=pl.BlockSpec((QB, ATT_DIM), lambda b, j: (q_blk(b, j), 0)),
        out_shape=jax.ShapeDtypeStruct((T_ALL, ATT_DIM), BF16),
        compiler_params=_params(2),
        name="window_attn",
    )(qt, k, k, k, k, vt, vt, vt, vt, sink)


CONV_S = 4
CONV_ROWS = 8 * CONV_S
N_SLAB = CONV_DIM // 128


def _even_out_kernel(x_ref, m_ref, u_ref, up_ref, un_ref, att_ref, cw_ref, cb_ref,
                     lg_ref, lb_ref, wa_ref, wb_ref, o_ref, ext_ref, a_ref, cva_ref, cvb_ref, *, tm, tiles_per_seq):
    t = pl.program_id(0)
    first = (t % tiles_per_seq) == 0
    last = (t % tiles_per_seq) == tiles_per_seq - 1
    slabs = [slice(s * 128, (s + 1) * 128) for s in range(N_SLAB)]
    for s, ls in enumerate(slabs):
        ext_ref[s, 0:HALO, :] = jnp.where(first, 0.0, up_ref[:, ls])
        ext_ref[s, HALO:HALO + tm, :] = u_ref[:, ls]
        ext_ref[s, HALO + tm:, :] = jnp.where(last, 0.0, un_ref[:, ls])

    def conv_block(blk, cv_ref):
        base = blk * CONV_ROWS
        for s, ls in enumerate(slabs):
            acc = [jnp.broadcast_to(cb_ref[:, ls], (8, 128)) for _ in range(CONV_S)]
            for o in range(CONV_W + CONV_S - 1):
                v = ext_ref[s, pl.ds(base + (HALO - CONV_PAD) + o, 8, stride=CONV_S), :]
                for j in range(CONV_S):
                    k = o - j
                    if 0 <= k < CONV_W:
                        acc[j] = acc[j] + cw_ref[k:k + 1, ls] * v
            for j in range(CONV_S):
                cv_ref[s * CONV_S + j] = acc[j]

    def norm_block(blk, cv_ref):
        base = blk * CONV_ROWS
        for j in range(CONV_S):
            row = [cv_ref[s * CONV_S + j] for s in range(N_SLAB)]
            mu = jnp.sum(sum(row), axis=-1, keepdims=True) * (1.0 / CONV_DIM)
            cen = [r - mu for r in row]
            var = jnp.sum(sum(c * c for c in cen), axis=-1, keepdims=True) * (1.0 / CONV_DIM)
            rs = lax.rsqrt(var + EPS)
            for s, ls in enumerate(slabs):
                y = cen[s] * rs * lg_ref[:, ls] + lb_ref[:, ls]
                a_ref[s, pl.ds(base + j, 8, stride=CONV_S), :] = _silu(y)

    n_blk = tm // CONV_ROWS
    conv_block(0, cva_ref)

    def body(i, carry):
        conv_block(2 * i + 1, cvb_ref)
        norm_block(2 * i, cva_ref)
        conv_block(jnp.minimum(2 * i + 2, n_blk - 1), cva_ref)
        norm_block(2 * i + 1, cvb_ref)
        return carry

    lax.fori_loop(0, n_blk // 2, body, 0)
    a = jnp.concatenate([a_ref[s] for s in range(N_SLAB)], axis=1).astype(BF16)
    y = _dot(a, wa_ref[...]) + _dot(att_ref[...], wb_ref[...])
    o_ref[...] = x_ref[...] + m_ref[:, 2 * D:3 * D] * y


def _even_out(x, mods, u, att, conv_w, conv_b, ln_g, ln_b, wa, wb, *, tm, row0, n_tiles,
              tiles_per_seq, mod_row):
    blk0 = row0 // tm
    hb = tm // HALO
    n_halo = T_ALL // HALO
    row_map = lambda i: (blk0 + i, 0)
    in_specs = [
        pl.BlockSpec((tm, D), row_map),
        pl.BlockSpec((None, 1, 3 * D), lambda i: (mod_row(i), 0, 1)),
        pl.BlockSpec((tm, CONV_DIM), row_map),
        pl.BlockSpec((HALO, CONV_DIM), lambda i: (jnp.maximum((blk0 + i) * hb - 1, 0), 0)),
        pl.BlockSpec((HALO, CONV_DIM), lambda i: (jnp.minimum((blk0 + i + 1) * hb, n_halo - 1), 0)),
        pl.BlockSpec((tm, ATT_DIM), row_map),
        _resident((32, CONV_DIM)),
        _resident((1, CONV_DIM)),
        _resident((1, CONV_DIM)),
        _resident((1, CONV_DIM)),
        _resident((CONV_DIM, D)),
        _resident((ATT_DIM, D)),
    ]
    return pl.pallas_call(
        functools.partial(_even_out_kernel, tm=tm, tiles_per_seq=tiles_per_seq),
        grid=(n_tiles,),
        in_specs=in_specs,
        out_specs=pl.BlockSpec((tm, D), row_map),
        out_shape=jax.ShapeDtypeStruct((T_ALL, D), F32),
        scratch_shapes=[pltpu.VMEM((N_SLAB, tm + 2 * HALO, 128), F32),
                        pltpu.VMEM((N_SLAB, tm, 128), F32),
                        pltpu.VMEM((N_SLAB * CONV_S, 8, 128), F32),
                        pltpu.VMEM((N_SLAB * CONV_S, 8, 128), F32)],
        input_output_aliases={0: 0},
        compiler_params=_params(1),
        name="even_out" if tiles_per_seq > 1 else "even_out_ctx",
    )(x, mods, u, u, u, att, conv_w, conv_b, ln_g, ln_b, wa, wb)


N_GATE = 4 * ML_HEADS
N_CHAIN = 2 * ML_HEADS
L = ML_CHUNK


def _log_sigmoid(x):
    return jnp.minimum(x, 0.0) - jnp.log(1.0 + jnp.exp(-jnp.abs(x)))


def _lane_scan(x, lane, fwd_rows, combine, fill):
    sh = 1
    while sh < L:
        pre = jnp.where(lane >= sh, pltpu.roll(x, sh, 1), fill)
        suf = jnp.where(lane < L - sh, pltpu.roll(x, L - sh, 1), fill)
        x = combine(x, jnp.where(fwd_rows, pre, suf))
        sh *= 2
    return x


def _odd_in_kernel(x_ref, m_ref, g_ref, wqt_ref, wkt_ref, wvt_ref, bgt_ref,
                   qt_ref, k_ref, vt_ref, a_ref, amax_ref, b_ref):
    x = x_ref[...]
    hf = _rms_mod(x, g_ref[...], m_ref[:, 0:D], m_ref[:, D:2 * D])
    h = hf.astype(BF16)
    h_lo = (hf - h.astype(F32)).astype(BF16)
    qg = _dot_nt(wqt_ref[...], h)
    qt_ref[...] = qg[0:ML_DIM].astype(BF16)
    g = (qg[ML_DIM:ML_DIM + N_GATE] + qg[ML_DIM + N_GATE:]
         + _dot_nt(wqt_ref[ML_DIM:ML_DIM + N_GATE, :], h_lo) + bgt_ref[...])
    k_ref[...] = (_dot_nt(h, wkt_ref[...]) * (ML_DH ** -0.5)).astype(BF16)
    vt_ref[...] = _dot_nt(wvt_ref[...], h).astype(BF16)
    li = g[0:N_CHAIN]
    lf = _log_sigmoid(g[N_CHAIN:N_GATE])
    fwd_rows = lax.broadcasted_iota(jnp.int32, (N_CHAIN, L), 0) < ML_HEADS
    lane = lax.broadcasted_iota(jnp.int32, (N_CHAIN, L), 1)
    for ch in range(TM // L):
        cs = slice(ch * L, (ch + 1) * L)
        b = _lane_scan(lf[:, cs], lane, fwd_rows, jnp.add, 0.0)
        a = li[:, cs] - b
        a_ref[:, cs] = a
        amax_ref[:, cs] = _lane_scan(a, lane, fwd_rows, jnp.maximum, NEG)
        b_ref[:, cs] = b


def _w_in_t_block(j):
    return pl.BlockSpec((ML_DIM, D), lambda i: (j, 0), pipeline_mode=pl.Buffered(1))


def _odd_in(x, mods, norm_g, w_qg_t, w_in_t, b_gate_t):
    row = lambda i: (i, 0)
    col = lambda i: (0, i)
    return pl.pallas_call(
        _odd_in_kernel,
        grid=(N_ALL_TILES,),
        in_specs=[
            pl.BlockSpec((TM, D), row),
            _mod_spec(1),
            _resident((1, D)),
            _resident((ML_DIM + 2 * N_GATE, D)),
            _w_in_t_block(1),
            _w_in_t_block(2),
            _resident((N_GATE, 1)),
        ],
        out_specs=[
            pl.BlockSpec((ML_DIM, TM), col),
            pl.BlockSpec((TM, ML_DIM), row),
            pl.BlockSpec((ML_DIM, TM), col),
            pl.BlockSpec((N_CHAIN, TM), col),
            pl.BlockSpec((N_CHAIN, TM), col),
            pl.BlockSpec((N_CHAIN, TM), col),
        ],
        out_shape=[
            jax.ShapeDtypeStruct((ML_DIM, T_ALL), BF16),
            jax.ShapeDtypeStruct((T_ALL, ML_DIM), BF16),
            jax.ShapeDtypeStruct((ML_DIM, T_ALL), BF16),
            jax.ShapeDtypeStruct((N_CHAIN, T_ALL), F32),
            jax.ShapeDtypeStruct((N_CHAIN, T_ALL), F32),
            jax.ShapeDtypeStruct((N_CHAIN, T_ALL), F32),
        ],
        compiler_params=_params(1),
        name="odd_in",
    )(x, mods, norm_g.reshape(1, D), w_qg_t, w_in_t, w_in_t, b_gate_t)


SUBS = 2
ML_BLK = SUBS * ML_CHUNK
N_CTX_CHUNKS = CTX // ML_BLK
N_LAT_CHUNKS = SEQ // ML_BLK
N_STEPS = N_CTX_CHUNKS + N_LAT_CHUNKS
N_AUG = 16


def _mlstm_kernel(qtf_ref, kf_ref, vtf_ref, qtb_ref, kb_ref, vtb_ref,
                  af_ref, amaxf_ref, bf_ref, ab_ref, amaxb_ref, bb_ref, hf_ref, hb_ref, c_ref, m_ref):
    t = pl.program_id(1)

    @pl.when(t == 0)
    def _():
        c_ref[...] = jnp.zeros_like(c_ref)
        m_ref[...] = jnp.zeros_like(m_ref)

    for sub in range(SUBS):
        _mlstm_chunk(slice(sub * L, (sub + 1) * L), slice((SUBS - 1 - sub) * L, (SUBS - sub) * L),
                     qtf_ref, kf_ref, vtf_ref, qtb_ref, kb_ref, vtb_ref,
                     af_ref, amaxf_ref, bf_ref, ab_ref, amaxb_ref, bb_ref, hf_ref, hb_ref, c_ref, m_ref)


def _mlstm_chunk(fs, bs, qtf_ref, kf_ref, vtf_ref, qtb_ref, kb_ref, vtb_ref,
                 af_ref, amaxf_ref, bf_ref, ab_ref, amaxb_ref, bb_ref, hf_ref, hb_ref, c_ref, m_ref):
    fwd_rows = lax.broadcasted_iota(jnp.int32, (N_CHAIN, L), 0) < ML_HEADS
    a = jnp.where(fwd_rows, af_ref[:, fs], ab_ref[:, bs])
    amax = jnp.where(fwd_rows, amaxf_ref[:, fs], amaxb_ref[:, bs])
    b = jnp.where(fwd_rows, bf_ref[:, fs], bb_ref[:, bs])
    m_old = m_ref[:, 0:1]
    big = jnp.maximum(m_old, jnp.max(amax, axis=1, keepdims=True))
    decay = jnp.exp(m_old - big)
    e = jnp.exp(a - big)
    mm = jnp.maximum(amax, m_old)
    w_inter = jnp.exp(m_old - mm)
    floor = jnp.exp(-(b + mm))
    m_ref[...] = jnp.broadcast_to(jnp.min(b, axis=1, keepdims=True) + big, (N_CHAIN, 128))
    kq = lax.broadcasted_iota(jnp.int32, (L, L), 0)
    qq = lax.broadcasted_iota(jnp.int32, (L, L), 1)
    a_col = jnp.concatenate([a, jnp.zeros((L - N_CHAIN, L), F32)], axis=0).T

    for d, (qt_ref, k_ref, vt_ref, h_ref, ts) in enumerate(
            ((qtf_ref, kf_ref, vtf_ref, hf_ref, fs), (qtb_ref, kb_ref, vtb_ref, hb_ref, bs))):
        visible = (kq >= qq) if d == 1 else (kq <= qq)
        for hd in range(ML_HEADS):
            c = d * ML_HEADS + hd
            cs = slice(hd * ML_DH, (hd + 1) * ML_DH)
            qt = qt_ref[cs, ts]
            k = k_ref[ts, cs]
            vt = vt_ref[cs, ts]
            p = jnp.where(visible, jnp.exp(a_col[:, c:c + 1] - mm[c:c + 1, :]), 0.0)
            c_old = c_ref[c]
            kcq = _dot(jnp.concatenate([k, c_old.astype(BF16)], axis=0), qt)
            st = kcq[0:L] * p
            cq = kcq[L:]
            wi = w_inter[c:c + 1, :]
            num = _dot(vt, st.astype(BF16)) + wi * cq[0:ML_DH]
            den = jnp.sum(st, axis=0, keepdims=True) + wi * cq[ML_DH:ML_DH + 1]
            ht = num * (1.0 / jnp.maximum(jnp.abs(den), floor[c:c + 1, :]))
            h_ref[ts, cs] = ht.T.astype(BF16)
            er = e[c:c + 1, :]
            vte = jnp.concatenate([vt.astype(F32) * er, jnp.broadcast_to(er, (N_AUG, L))], axis=0)
            c_ref[c] = decay[c:c + 1, :] * c_old + _dot(vte.astype(BF16), k)


def _mlstm(qt, k, vt, a, amax, b):
    lat_chunks = T_LAT // ML_BLK

    def fwd_in(b, t):
        return jnp.where(t < N_CTX_CHUNKS, lat_chunks + b * N_CTX_CHUNKS + t,
                         b * N_LAT_CHUNKS + (t - N_CTX_CHUNKS))

    def bwd_in(b, t):
        return jnp.where(t < N_CTX_CHUNKS, lat_chunks + b * N_CTX_CHUNKS + (N_CTX_CHUNKS - 1 - t),
                         b * N_LAT_CHUNKS + (N_STEPS - 1 - t))

    def fwd_out(b, t):
        return b * N_LAT_CHUNKS + jnp.maximum(t - N_CTX_CHUNKS, 0)

    def bwd_out(b, t):
        return b * N_LAT_CHUNKS + (N_STEPS - 1 - jnp.maximum(t, N_CTX_CHUNKS))

    rows = lambda f: pl.BlockSpec((ML_BLK, ML_DIM), lambda b, t: (f(b, t), 0))
    cols = lambda f: pl.BlockSpec((ML_DIM, ML_BLK), lambda b, t: (0, f(b, t)))
    gate = lambda f: pl.BlockSpec((N_CHAIN, ML_BLK), lambda b, t: (0, f(b, t)))
    return pl.pallas_call(
        _mlstm_kernel,
        grid=(BATCH, N_STEPS),
        in_specs=[cols(fwd_in), rows(fwd_in), cols(fwd_in),
                  cols(bwd_in), rows(bwd_in), cols(bwd_in),
                  gate(fwd_in), gate(fwd_in), gate(fwd_in), gate(bwd_in), gate(bwd_in), gate(bwd_in)],
        out_specs=[rows(fwd_out), rows(bwd_out)],
        out_shape=[jax.ShapeDtypeStruct((T_LAT, ML_DIM), BF16),
                   jax.ShapeDtypeStruct((T_LAT, ML_DIM), BF16)],
        scratch_shapes=[pltpu.VMEM((N_CHAIN, ML_DH + N_AUG, ML_DH), F32),
                        pltpu.VMEM((N_CHAIN, 128), F32)],
        compiler_params=_params(2),
        name="mlstm",
    )(qt, k, vt, qt, k, vt, a, amax, b, a, amax, b)


def _odd_out_kernel(x_ref, m_ref, g_ref, hf_ref, hb_ref, wot_ref, ng_ref, wout_ref, o_ref):
    x = x_ref[...]
    h = _rms_mod(x, g_ref[...], m_ref[:, 0:D], m_ref[:, D:2 * D]).astype(BF16)
    o = _sigmoid(_dot_nt(h, wot_ref[...]))
    hs = hf_ref[...].astype(F32) + hb_ref[...].astype(F32)
    parts = []
    for hd in range(ML_HEADS):
        p = hs[:, hd * ML_DH:(hd + 1) * ML_DH]
        parts.append(p * lax.rsqrt(jnp.mean(p * p, axis=-1, keepdims=True) + EPS))
    hn = jnp.concatenate(parts, axis=-1) * ng_ref[...]
    y = _dot((o * hn).astype(BF16), wout_ref[...])
    o_ref[...] = x + m_ref[:, 2 * D:3 * D] * y


def _odd_out(x, mods, norm_g, hf, hb, w_in_t, head_g, w_out):
    row = lambda i: (i, 0)
    return pl.pallas_call(
        _odd_out_kernel,
        grid=(N_LAT_TILES,),
        in_specs=[
            pl.BlockSpec((TM, D), row),
            _mod_spec(1),
            _resident((1, D)),
            pl.BlockSpec((TM, ML_DIM), row),
            pl.BlockSpec((TM, ML_DIM), row),
            _w_in_t_block(3),
            _resident((1, ML_DIM)),
            _resident((ML_DIM, D)),
        ],
        out_specs=pl.BlockSpec((TM, D), row),
        out_shape=jax.ShapeDtypeStruct((T_LAT, D), F32),
        compiler_params=_params(1),
        name="odd_out",
    )(x, mods, norm_g.reshape(1, D), hf, hb, w_in_t, head_g.reshape(1, ML_DIM), w_out)


def kernel(x, c, ctx, c_ctx, mod_w, mod_b, ffn1_norm, ffn1_w_gu, ffn1_w_d, mix_norm, ffn2_norm,
           ffn2_w_gu, ffn2_w_d, ev_w_in, ev_conv_w, ev_conv_b, ev_conv_ln_g, ev_conv_ln_b, ev_sink,
           ev_w_out, od_w_in, od_b_gate, od_norm_g, od_w_out, final_norm):
    assert DEPTH == 2 and x.shape == (BATCH, SEQ, D) and ctx.shape == (BATCH, CTX, D)
    cond = jnp.zeros((MOD_ROWS, D), F32).at[:BATCH].set(c).at[BATCH].set(c_ctx)
    mods = _ada_mods(cond, mod_w, mod_b).reshape(DEPTH, MOD_ROWS, 1, N_MOD * D)
    bf = lambda w: w.astype(BF16)

    m0 = mods[0]
    xs = _ffn(x.reshape(T_LAT, D), m0, 0, ffn1_norm[0], ffn1_w_gu, ffn1_w_d, 0, N_ALL_TILES,
              x_ctx=ctx.reshape(T_CTX, D))
    w_in = ev_w_in[0]
    rope, rope_t = _rope_tables()
    u, qt, kk, vt = _even_in(xs, m0, mix_norm[0], bf(w_in).T, rope, rope_t)
    sink = jnp.repeat(ev_sink[0].astype(F32), BLK).reshape(1, ATT_HEADS * BLK)
    att = _attention(qt, kk, vt, sink)
    conv_w = jnp.concatenate([ev_conv_w[0], jnp.zeros((1, CONV_DIM), F32)], axis=0)
    ev_args = (conv_w, ev_conv_b[0].reshape(1, -1), ev_conv_ln_g[0].reshape(1, -1),
               ev_conv_ln_b[0].reshape(1, -1), bf(ev_w_out[0][:CONV_DIM]), bf(ev_w_out[0][CONV_DIM:]))
    xs = _even_out(xs, m0, u, att, *ev_args, tm=TM, row0=0, n_tiles=N_LAT_TILES,
                   tiles_per_seq=SEQ // TM, mod_row=lambda i: i // (SEQ // TM))
    xs = _even_out(xs, m0, u, att, *ev_args, tm=CTX, row0=T_LAT, n_tiles=BATCH,
                   tiles_per_seq=1, mod_row=lambda i: BATCH)
    xs = _ffn(xs, m0, 2, ffn2_norm[0], ffn2_w_gu, ffn2_w_d, 0, N_ALL_TILES)

    m1 = mods[1]
    xs = _ffn(xs, m1, 0, ffn1_norm[1], ffn1_w_gu, ffn1_w_d, 1, N_ALL_TILES)
    w_in = od_w_in[0]
    perm = np.concatenate([np.arange(0, 4), np.arange(8, 12), np.arange(4, 8), np.arange(12, 16)])
    w_gate_t = w_in[:, 4 * ML_DIM:].T[perm]
    b_gate_t = od_b_gate[0][perm].reshape(N_GATE, 1)
    w_gate_hi = bf(w_gate_t)
    w_gate_lo = bf(w_gate_t - w_gate_hi.astype(F32))
    w_in_t = bf(w_in[:, :4 * ML_DIM]).T
    w_qg_t = jnp.concatenate([w_in_t[:ML_DIM], w_gate_hi, w_gate_lo], axis=0)
    qt, km, vt, ga, gamax, gb = _odd_in(xs, m1, mix_norm[1], w_qg_t, w_in_t, b_gate_t)
    hf, hb = _mlstm(qt, km, vt, ga, gamax, gb)
    xl = _odd_out(xs, m1, mix_norm[1], hf, hb, w_in_t, od_norm_g[0], bf(od_w_out[0]))
    out = _ffn(xl, m1, 2, ffn2_norm[1], ffn2_w_gu, ffn2_w_d, 1, N_LAT_TILES,
               final_g=final_norm)
    return out.reshape(BATCH, SEQ, D)
```

```python
import functools

import jax
import jax.numpy as jnp
import numpy as np
from jax import lax
from jax.experimental import pallas as pl
from jax.experimental.pallas import tpu as pltpu

D = 1024
BATCH = 4
SEQ = 4096
DEPTH = 2
GRID_W = 64
CTX = 256
N_MOD = 9
D_FF = 2816
EPS = 1e-6
CONV_DIM = 512
CONV_W = 31
CONV_PAD = 15
HEAD_DIM = 64
ATT_HEADS = 8
KV_HEADS = 2
ATT_DIM = 512
KV_DIM = 128
WINDOW = 128
BLK = 128
ROPE_BASE = 10000.0
ROPE_FREQS = 16
Q_OFF = 2 * CONV_DIM
K_OFF = Q_OFF + ATT_DIM
V_OFF = K_OFF + KV_DIM
EVEN_IN = V_OFF + KV_DIM
ML_HEADS = 4
ML_DH = 256
ML_DIM = 1024
ML_CHUNK = 128

T_LAT = BATCH * SEQ
T_CTX = BATCH * CTX
T_ALL = T_LAT + T_CTX
TM = 1024
N_LAT_TILES = T_LAT // TM
N_ALL_TILES = T_ALL // TM
MOD_ROWS = 8
HALO = 16
NEG = -1e30
VMEM_LIMIT = 56 * 1024 * 1024
VMEM_LIMIT_FFN = 60 * 1024 * 1024

F32 = jnp.float32
BF16 = jnp.bfloat16


def _sigmoid(x):
    return 1.0 / (1.0 + jnp.exp(-x))


def _silu(x):
    return x * _sigmoid(x)


def _dot(a, b, precision=None):
    return jnp.dot(a, b, preferred_element_type=F32, precision=precision)


def _dot_nt(a, b, precision=None):
    return lax.dot_general(a, b, (((1,), (1,)), ((), ())),
                           preferred_element_type=F32, precision=precision)


def _rms_mod(x, g, shift, scale):
    y = x * lax.rsqrt(jnp.mean(x * x, axis=-1, keepdims=True) + EPS)
    return (y * g) * (1.0 + scale) + shift


def _resident(shape):
    nd = len(shape)
    return pl.BlockSpec(shape, lambda *_: (0,) * nd, pipeline_mode=pl.Buffered(1))


def _params(n_axes=1, vmem_limit=VMEM_LIMIT):
    return pltpu.CompilerParams(dimension_semantics=("arbitrary",) * n_axes,
                                vmem_limit_bytes=vmem_limit)


def _mod_kernel(c_ref, w_ref, b_ref, o_ref):
    s = _silu(c_ref[...]).astype(BF16)
    o_ref[...] = _dot(s, w_ref[...].astype(BF16)) + b_ref[...]


def _ada_mods(cond, mod_w, mod_b):
    tn = 1024
    n = N_MOD * D
    return pl.pallas_call(
        _mod_kernel,
        grid=(DEPTH, n // tn),
        in_specs=[
            pl.BlockSpec((MOD_ROWS, D), lambda l, j: (0, 0)),
            pl.BlockSpec((None, D, tn), lambda l, j: (l, 0, j)),
            pl.BlockSpec((None, 1, tn), lambda l, j: (l, 0, j)),
        ],
        out_specs=pl.BlockSpec((None, MOD_ROWS, tn), lambda l, j: (l, 0, j)),
        out_shape=jax.ShapeDtypeStruct((DEPTH, MOD_ROWS, n), F32),
        compiler_params=_params(2),
        name="ada_mods",
    )(cond, mod_w, mod_b.reshape(DEPTH, 1, n))


def _mod_spec(k):
    return pl.BlockSpec((None, 1, 3 * D), lambda i: (i // (SEQ // TM), 0, k))


FF_CHUNKS = tuple((c, c + 256) for c in range(0, D_FF, 256))


STAGE = 256
N_STAGE = 3


def _stage_weights(src_ref, dst_ref, stage_ref, sem_ref, chunks):
    def copy(n):
        slot = n % N_STAGE
        return pltpu.make_async_copy(src_ref.at[chunks[n][0]], stage_ref.at[slot], sem_ref.at[slot])

    for n in range(min(N_STAGE, len(chunks))):
        copy(n).start()
    for n in range(len(chunks)):
        copy(n).wait()
        dst_ref[chunks[n][1]] = stage_ref[n % N_STAGE].astype(BF16)
        if n + N_STAGE < len(chunks):
            copy(n + N_STAGE).start()


def _ffn_kernel(x_ref, *rest, layer, split, final):
    if split:
        xc_ref, *rest = rest
    m_ref, g_ref, wgu_hbm, wd_hbm, *rest = rest
    if final:
        fn_ref, *rest = rest
    o_ref, wgu_ref, wd_ref, sgu_ref, sd_ref, act_ref, sem_gu, sem_d = rest

    @pl.when(pl.program_id(0) == 0)
    def _():
        _stage_weights(wgu_hbm, wgu_ref, sgu_ref, sem_gu,
                       [((layer, slice(None), pl.ds(c, STAGE)), (slice(None), pl.ds(c, STAGE)))
                        for c in range(0, 2 * D_FF, STAGE)])
        _stage_weights(wd_hbm, wd_ref, sd_ref, sem_d,
                       [((layer, pl.ds(r, STAGE), slice(None)), (pl.ds(r, STAGE), slice(None)))
                        for r in range(0, D_FF, STAGE)])

    x = x_ref[...]
    if split:
        x = jnp.where(pl.program_id(0) < N_LAT_TILES, x, xc_ref[...])
    shift = m_ref[:, 0:D]
    scale = m_ref[:, D:2 * D]
    gate = m_ref[:, 2 * D:3 * D]
    h = _rms_mod(x, g_ref[...], shift, scale).astype(BF16)
    for c0, c1 in FF_CHUNKS:
        hg = _dot(h, wgu_ref[:, c0:c1])
        hu = _dot(h, wgu_ref[:, D_FF + c0:D_FF + c1])
        act_ref[:, c0:c1] = (_silu(hg) * hu).astype(BF16)
    y = x + (0.5 * gate) * _dot(act_ref[...], wd_ref[...])
    if final:
        y = (y * lax.rsqrt(jnp.mean(y * y, axis=-1, keepdims=True) + EPS)) * fn_ref[...]
    o_ref[...] = y


def _ffn(x, mods, k, norm_g, w_gu, w_d, layer, n_tiles, final_g=None, x_ctx=None):
    final = final_g is not None
    split = x_ctx is not None
    if split:
        in_specs = [pl.BlockSpec((TM, D), lambda i: (jnp.minimum(i, N_LAT_TILES - 1), 0)),
                    _resident((TM, D))]
        args = [x, x_ctx]
    else:
        in_specs = [pl.BlockSpec((TM, D), lambda i: (i, 0))]
        args = [x]
    in_specs += [
        _mod_spec(k),
        _resident((1, D)),
        pl.BlockSpec(memory_space=pl.ANY),
        pl.BlockSpec(memory_space=pl.ANY),
    ]
    args += [mods, norm_g.reshape(1, D), w_gu, w_d]
    if final:
        in_specs.append(_resident((1, D)))
        args.append(final_g.reshape(1, D))
    return pl.pallas_call(
        functools.partial(_ffn_kernel, layer=layer, split=split, final=final),
        grid=(n_tiles,),
        in_specs=in_specs,
        out_specs=pl.BlockSpec((TM, D), lambda i: (i, 0)),
        out_shape=jax.ShapeDtypeStruct((n_tiles * TM, D), F32),
        scratch_shapes=[pltpu.VMEM((D, 2 * D_FF), BF16),
                        pltpu.VMEM((D_FF, D), BF16),
                        pltpu.VMEM((N_STAGE, D, STAGE), F32),
                        pltpu.VMEM((N_STAGE, STAGE, D), F32),
                        pltpu.VMEM((TM, D_FF), BF16),
                        pltpu.SemaphoreType.DMA((N_STAGE,)),
                        pltpu.SemaphoreType.DMA((N_STAGE,))],
        compiler_params=_params(1, VMEM_LIMIT_FFN),
        name="ffn_final" if final else ("ffn_first" if split else "ffn"),
    )(*args)


def _even_in_kernel(x_ref, m_ref, g_ref, wt_ref,
                    c_ref, s1_ref, s2_ref, ct_ref, s1t_ref, s2t_ref,
                    u_ref, qt_ref, k_ref, vt_ref):
    x = x_ref[...]
    h = _rms_mod(x, g_ref[...], m_ref[:, 0:D], m_ref[:, D:2 * D]).astype(BF16)
    vg = _dot_nt(h, wt_ref[0:Q_OFF, :])
    u_ref[...] = vg[:, 0:CONV_DIM] * _sigmoid(vg[:, CONV_DIM:Q_OFF])
    qf = _dot_nt(wt_ref[Q_OFF:K_OFF, :], h)
    ct, s1t, s2t = ct_ref[...], s1t_ref[...], s2t_ref[...]
    for hd in range(ATT_HEADS):
        xh = qf[hd * HEAD_DIM:(hd + 1) * HEAD_DIM]
        up = jnp.concatenate([xh[ROPE_FREQS:], xh[:ROPE_FREQS]], axis=0)
        dn = jnp.concatenate([xh[HEAD_DIM - ROPE_FREQS:], xh[:HEAD_DIM - ROPE_FREQS]], axis=0)
        r = xh * ct + up * s1t + dn * s2t
        qt_ref[hd * HEAD_DIM:(hd + 1) * HEAD_DIM, :] = (r * (HEAD_DIM ** -0.5)).astype(BF16)
    kf = _dot_nt(h, wt_ref[K_OFF:V_OFF, :])
    kr = kf * c_ref[...] + pltpu.roll(kf, 128 - ROPE_FREQS, 1) * s1_ref[...] + pltpu.roll(kf, ROPE_FREQS, 1) * s2_ref[...]
    k_ref[...] = kr.astype(BF16)
    vt_ref[...] = _dot_nt(wt_ref[V_OFF:EVEN_IN, :], h).astype(BF16)


def _even_in(x, mods, norm_g, w_t, rope, rope_t):
    pos = lambda i: jnp.where(i < N_LAT_TILES, i % (SEQ // TM), SEQ // TM)
    rope_spec = lambda: pl.BlockSpec((TM, KV_DIM), lambda i: (pos(i), 0))
    rope_t_spec = lambda: pl.BlockSpec((HEAD_DIM, TM), lambda i: (0, pos(i)))
    row = lambda i: (i, 0)
    col = lambda i: (0, i)
    return pl.pallas_call(
        _even_in_kernel,
        grid=(N_ALL_TILES,),
        in_specs=[
            pl.BlockSpec((TM, D), row),
            _mod_spec(1),
            _resident((1, D)),
            _resident((EVEN_IN, D)),
            rope_spec(), rope_spec(), rope_spec(),
            rope_t_spec(), rope_t_spec(), rope_t_spec(),
        ],
        out_specs=[
            pl.BlockSpec((TM, CONV_DIM), row),
            pl.BlockSpec((ATT_DIM, TM), col),
            pl.BlockSpec((TM, KV_DIM), row),
            pl.BlockSpec((KV_DIM, TM), col),
        ],
        out_shape=[
            jax.ShapeDtypeStruct((T_ALL, CONV_DIM), F32),
            jax.ShapeDtypeStruct((ATT_DIM, T_ALL), BF16),
            jax.ShapeDtypeStruct((T_ALL, KV_DIM), BF16),
            jax.ShapeDtypeStruct((KV_DIM, T_ALL), BF16),
        ],
        compiler_params=_params(1),
        name="even_in",
    )(x, mods, norm_g.reshape(1, D), w_t, *rope, *rope_t)


def _rope_tables():
    rows = SEQ // GRID_W
    inv = ROPE_BASE ** (-jnp.arange(ROPE_FREQS, dtype=F32) / ROPE_FREQS)
    d = np.arange(128) % HEAD_DIM
    first = jnp.asarray(((d % 32) // ROPE_FREQS) == 0)
    by_row = jnp.asarray(d // 32 == 0)

    def expand(fn):
        t_row = fn(jnp.arange(rows, dtype=F32)[:, None] * inv)[:, d % ROPE_FREQS]
        t_col = fn(jnp.arange(GRID_W, dtype=F32)[:, None] * inv)[:, d % ROPE_FREQS]
        return jnp.where(by_row, jnp.repeat(t_row, GRID_W, axis=0), jnp.tile(t_col, (rows, 1)))

    c = expand(jnp.cos)
    s = expand(jnp.sin)
    s1 = jnp.where(first, -s, 0.0)
    s2 = jnp.where(first, 0.0, s)
    pad = lambda t, v: jnp.concatenate([t, jnp.full((TM, 128), v, F32)], axis=0)
    tok = (pad(c, 1.0), pad(s1, 0.0), pad(s2, 0.0))
    return tok, tuple(t[:, :HEAD_DIM].T for t in tok)


NB = SEQ // BLK
QB = 2 * BLK
GROUP = ATT_HEADS // KV_HEADS
assert CTX == QB


def _attn_kernel(qt_ref, kp_ref, kc_ref, kn_ref, kx_ref, vp_ref, vc_ref, vn_ref, vx_ref,
                 sink_ref, o_ref):
    for sub in range(QB // BLK):
        ks = [kp_ref[...], kc_ref[0:BLK], kc_ref[BLK:2 * BLK], kn_ref[...]][sub:sub + 3]
        vs = [vp_ref[...], vc_ref[:, 0:BLK], vc_ref[:, BLK:2 * BLK], vn_ref[...]][sub:sub + 3]
        o_ref[sub * BLK:(sub + 1) * BLK, :] = _attn_block(
            (QB // BLK) * pl.program_id(1) + sub, qt_ref[:, sub * BLK:(sub + 1) * BLK],
            ks + [kx_ref[...]], vs + [vx_ref[...]], sink_ref)


def _attn_block(i, qt, ks, vs, sink_ref):
    kk = lax.broadcasted_iota(jnp.int32, (3 * BLK, BLK), 0)
    qq = lax.broadcasted_iota(jnp.int32, (3 * BLK, BLK), 1)
    rel = kk - BLK - qq
    kpos = (i - 1) * BLK + kk
    ok = (jnp.abs(rel) <= WINDOW) & (kpos >= 0) & (kpos < SEQ) & (i < NB)
    bias1 = jnp.where(ok, 0.0, NEG).astype(F32)
    bias = jnp.concatenate([bias1] * GROUP, axis=1)
    keys = jnp.concatenate(ks, axis=0)
    vt = jnp.concatenate(vs, axis=1)
    zero = jnp.zeros((HEAD_DIM, BLK), BF16)
    outs = []
    for g in range(KV_HEADS):
        cols = []
        for hh in range(GROUP):
            hd = g * GROUP + hh
            qh = qt[hd * HEAD_DIM:(hd + 1) * HEAD_DIM]
            cols.append(jnp.concatenate([qh, zero] if g == 0 else [zero, qh], axis=0))
        s = _dot(keys, jnp.concatenate(cols, axis=1))
        s_b = s[0:3 * BLK] + bias
        s_x = s[3 * BLK:]
        sink = sink_ref[:, g * GROUP * BLK:(g + 1) * GROUP * BLK]
        m = jnp.maximum(jnp.maximum(jnp.max(s_b, axis=0, keepdims=True),
                                    jnp.max(s_x, axis=0, keepdims=True)), sink)
        p_b = jnp.exp(s_b - m)
        p_x = jnp.exp(s_x - m)
        den = (jnp.sum(p_b, axis=0, keepdims=True) + jnp.sum(p_x, axis=0, keepdims=True)
               + jnp.exp(sink - m))
        p = jnp.concatenate([p_b, p_x], axis=0).astype(BF16)
        ot = _dot(vt[g * HEAD_DIM:(g + 1) * HEAD_DIM], p) * (1.0 / den)
        outs += [ot[:, hh * BLK:(hh + 1) * BLK] for hh in range(GROUP)]
    return jnp.concatenate(outs, axis=0).T.astype(BF16)


def _attention(qt, k, vt, sink):
    n_pairs = SEQ // QB
    q_blk = lambda b, j: jnp.where(j < n_pairs, b * n_pairs + j, T_LAT // QB + b)
    pair = lambda b, j: b * n_pairs + jnp.minimum(j, n_pairs - 1)
    edge = lambda off: (lambda b, j: b * NB + jnp.clip((QB // BLK) * j + off, 0, NB - 1))
    ctx_blk = lambda b, j: T_LAT // CTX + b
    k_spec = lambda n, f: pl.BlockSpec((n, KV_DIM), lambda b, j: (f(b, j), 0))
    v_spec = lambda n, f: pl.BlockSpec((KV_DIM, n), lambda b, j: (0, f(b, j)))
    return pl.pallas_call(
        _attn_kernel,
        grid=(BATCH, n_pairs + 1),
        in_specs=[
            pl.BlockSpec((ATT_DIM, QB), lambda b, j: (0, q_blk(b, j))),
            k_spec(BLK, edge(-1)), k_spec(QB, pair), k_spec(BLK, edge(QB // BLK)), k_spec(CTX, ctx_blk),
            v_spec(BLK, edge(-1)), v_spec(QB, pair), v_spec(BLK, edge(QB // BLK)), v_spec(CTX, ctx_blk),
            pl.BlockSpec((1, ATT_HEADS * BLK), lambda b, j: (0, 0)),
        ],
        out_specs=pl.BlockSpec((QB, ATT_DIM), lambda b, j: (q_blk(b, j), 0)),
        out_shape=jax.ShapeDtypeStruct((T_ALL, ATT_DIM), BF16),
        compiler_params=_params(2),
        name="window_attn",
    )(qt, k, k, k, k, vt, vt, vt, vt, sink)


CONV_S = 4
CONV_ROWS = 8 * CONV_S
N_SLAB = CONV_DIM // 128


def _even_out_kernel(x_ref, m_ref, u_ref, up_ref, un_ref, att_ref, cw_ref, cb_ref,
                     lg_ref, lb_ref, wa_ref, wb_ref, o_ref, ext_ref, a_ref, cva_ref, cvb_ref, *, tm, tiles_per_seq):
    t = pl.program_id(0)
    first = (t % tiles_per_seq) == 0
    last = (t % tiles_per_seq) == tiles_per_seq - 1
    slabs = [slice(s * 128, (s + 1) * 128) for s in range(N_SLAB)]
    for s, ls in enumerate(slabs):
        ext_ref[s, 0:HALO, :] = jnp.where(first, 0.0, up_ref[:, ls])
        ext_ref[s, HALO:HALO + tm, :] = u_ref[:, ls]
        ext_ref[s, HALO + tm:, :] = jnp.where(last, 0.0, un_ref[:, ls])

    def conv_block(blk, cv_ref):
        base = blk * CONV_ROWS
        for s, ls in enumerate(slabs):
            acc = [jnp.broadcast_to(cb_ref[:, ls], (8, 128)) for _ in range(CONV_S)]
            for o in range(CONV_W + CONV_S - 1):
                v = ext_ref[s, pl.ds(base + (HALO - CONV_PAD) + o, 8, stride=CONV_S), :]
                for j in range(CONV_S):
                    k = o - j
                    if 0 <= k < CONV_W:
                        acc[j] = acc[j] + cw_ref[k:k + 1, ls] * v
            for j in range(CONV_S):
                cv_ref[s * CONV_S + j] = acc[j]

    def norm_block(blk, cv_ref):
        base = blk * CONV_ROWS
        for j in range(CONV_S):
            row = [cv_ref[s * CONV_S + j] for s in range(N_SLAB)]
            mu = jnp.sum(sum(row), axis=-1, keepdims=True) * (1.0 / CONV_DIM)
            cen = [r - mu for r in row]
            var = jnp.sum(sum(c * c for c in cen), axis=-1, keepdims=True) * (1.0 / CONV_DIM)
            rs = lax.rsqrt(var + EPS)
            for s, ls in enumerate(slabs):
                y = cen[s] * rs * lg_ref[:, ls] + lb_ref[:, ls]
                a_ref[s, pl.ds(base + j, 8, stride=CONV_S), :] = _silu(y)

    n_blk = tm // CONV_ROWS
    conv_block(0, cva_ref)

    def body(i, carry):
        conv_block(2 * i + 1, cvb_ref)
        norm_block(2 * i, cva_ref)
        conv_block(jnp.minimum(2 * i + 2, n_blk - 1), cva_ref)
        norm_block(2 * i + 1, cvb_ref)
        return carry

    lax.fori_loop(0, n_blk // 2, body, 0)
    a = jnp.concatenate([a_ref[s] for s in range(N_SLAB)], axis=1).astype(BF16)
    y = _dot(a, wa_ref[...]) + _dot(att_ref[...], wb_ref[...])
    o_ref[...] = x_ref[...] + m_ref[:, 2 * D:3 * D] * y


def _even_out(x, mods, u, att, conv_w, conv_b, ln_g, ln_b, wa, wb, *, tm, row0, n_tiles,
              tiles_per_seq, mod_row):
    blk0 = row0 // tm
    hb = tm // HALO
    n_halo = T_ALL // HALO
    row_map = lambda i: (blk0 + i, 0)
    in_specs = [
        pl.BlockSpec((tm, D), row_map),
        pl.BlockSpec((None, 1, 3 * D), lambda i: (mod_row(i), 0, 1)),
        pl.BlockSpec((tm, CONV_DIM), row_map),
        pl.BlockSpec((HALO, CONV_DIM), lambda i: (jnp.maximum((blk0 + i) * hb - 1, 0), 0)),
        pl.BlockSpec((HALO, CONV_DIM), lambda i: (jnp.minimum((blk0 + i + 1) * hb, n_halo - 1), 0)),
        pl.BlockSpec((tm, ATT_DIM), row_map),
        _resident((32, CONV_DIM)),
        _resident((1, CONV_DIM)),
        _resident((1, CONV_DIM)),
        _resident((1, CONV_DIM)),
        _resident((CONV_DIM, D)),
        _resident((ATT_DIM, D)),
    ]
    return pl.pallas_call(
        functools.partial(_even_out_kernel, tm=tm, tiles_per_seq=tiles_per_seq),
        grid=(n_tiles,),
        in_specs=in_specs,
        out_specs=pl.BlockSpec((tm, D), row_map),
        out_shape=jax.ShapeDtypeStruct((T_ALL, D), F32),
        scratch_shapes=[pltpu.VMEM((N_SLAB, tm + 2 * HALO, 128), F32),
                        pltpu.VMEM((N_SLAB, tm, 128), F32),
                        pltpu.VMEM((N_SLAB * CONV_S, 8, 128), F32),
                        pltpu.VMEM((N_SLAB * CONV_S, 8, 128), F32)],
        input_output_aliases={0: 0},
        compiler_params=_params(1),
        name="even_out" if tiles_per_seq > 1 else "even_out_ctx",
    )(x, mods, u, u, u, att, conv_w, conv_b, ln_g, ln_b, wa, wb)


N_GATE = 4 * ML_HEADS
N_CHAIN = 2 * ML_HEADS
L = ML_CHUNK


def _log_sigmoid(x):
    return jnp.minimum(x, 0.0) - jnp.log(1.0 + jnp.exp(-jnp.abs(x)))


def _lane_scan(x, lane, fwd_rows, combine, fill):
    sh = 1
    while sh < L:
        pre = jnp.where(lane >= sh, pltpu.roll(x, sh, 1), fill)
        suf = jnp.where(lane < L - sh, pltpu.roll(x, L - sh, 1), fill)
        x = combine(x, jnp.where(fwd_rows, pre, suf))
        sh *= 2
    return x


def _odd_in_kernel(x_ref, m_ref, g_ref, wqt_ref, wkt_ref, wvt_ref, bgt_ref,
                   qt_ref, k_ref, vt_ref, a_ref, amax_ref, b_ref):
    x = x_ref[...]
    hf = _rms_mod(x, g_ref[...], m_ref[:, 0:D], m_ref[:, D:2 * D])
    h = hf.astype(BF16)
    h_lo = (hf - h.astype(F32)).astype(BF16)
    qg = _dot_nt(wqt_ref[...], h)
    qt_ref[...] = qg[0:ML_DIM].astype(BF16)
    g = (qg[ML_DIM:ML_DIM + N_GATE] + qg[ML_DIM + N_GATE:]
         + _dot_nt(wqt_ref[ML_DIM:ML_DIM + N_GATE, :], h_lo) + bgt_ref[...])
    k_ref[...] = (_dot_nt(h, wkt_ref[...]) * (ML_DH ** -0.5)).astype(BF16)
    vt_ref[...] = _dot_nt(wvt_ref[...], h).astype(BF16)
    li = g[0:N_CHAIN]
    lf = _log_sigmoid(g[N_CHAIN:N_GATE])
    fwd_rows = lax.broadcasted_iota(jnp.int32, (N_CHAIN, L), 0) < ML_HEADS
    lane = lax.broadcasted_iota(jnp.int32, (N_CHAIN, L), 1)
    for ch in range(TM // L):
        cs = slice(ch * L, (ch + 1) * L)
        b = _lane_scan(lf[:, cs], lane, fwd_rows, jnp.add, 0.0)
        a = li[:, cs] - b
        a_ref[:, cs] = a
        amax_ref[:, cs] = _lane_scan(a, lane, fwd_rows, jnp.maximum, NEG)
        b_ref[:, cs] = b


def _w_in_t_block(j):
    return pl.BlockSpec((ML_DIM, D), lambda i: (j, 0), pipeline_mode=pl.Buffered(1))


def _odd_in(x, mods, norm_g, w_qg_t, w_in_t, b_gate_t):
    row = lambda i: (i, 0)
    col = lambda i: (0, i)
    return pl.pallas_call(
        _odd_in_kernel,
        grid=(N_ALL_TILES,),
        in_specs=[
            pl.BlockSpec((TM, D), row),
            _mod_spec(1),
            _resident((1, D)),
            _resident((ML_DIM + 2 * N_GATE, D)),
            _w_in_t_block(1),
            _w_in_t_block(2),
            _resident((N_GATE, 1)),
        ],
        out_specs=[
            pl.BlockSpec((ML_DIM, TM), col),
            pl.BlockSpec((TM, ML_DIM), row),
            pl.BlockSpec((ML_DIM, TM), col),
            pl.BlockSpec((N_CHAIN, TM), col),
            pl.BlockSpec((N_CHAIN, TM), col),
            pl.BlockSpec((N_CHAIN, TM), col),
        ],
        out_shape=[
            jax.ShapeDtypeStruct((ML_DIM, T_ALL), BF16),
            jax.ShapeDtypeStruct((T_ALL, ML_DIM), BF16),
            jax.ShapeDtypeStruct((ML_DIM, T_ALL), BF16),
            jax.ShapeDtypeStruct((N_CHAIN, T_ALL), F32),
            jax.ShapeDtypeStruct((N_CHAIN, T_ALL), F32),
            jax.ShapeDtypeStruct((N_CHAIN, T_ALL), F32),
        ],
        compiler_params=_params(1),
        name="odd_in",
    )(x, mods, norm_g.reshape(1, D), w_qg_t, w_in_t, w_in_t, b_gate_t)


SUBS = 2
ML_BLK = SUBS * ML_CHUNK
N_CTX_CHUNKS = CTX // ML_BLK
N_LAT_CHUNKS = SEQ // ML_BLK
N_STEPS = N_CTX_CHUNKS + N_LAT_CHUNKS
N_AUG = 16


def _mlstm_kernel(qtf_ref, kf_ref, vtf_ref, qtb_ref, kb_ref, vtb_ref,
                  af_ref, amaxf_ref, bf_ref, ab_ref, amaxb_ref, bb_ref, hf_ref, hb_ref, c_ref, m_ref):
    t = pl.program_id(1)

    @pl.when(t == 0)
    def _():
        c_ref[...] = jnp.zeros_like(c_ref)
        m_ref[...] = jnp.zeros_like(m_ref)

    for sub in range(SUBS):
        _mlstm_chunk(slice(sub * L, (sub + 1) * L), slice((SUBS - 1 - sub) * L, (SUBS - sub) * L),
                     qtf_ref, kf_ref, vtf_ref, qtb_ref, kb_ref, vtb_ref,
                     af_ref, amaxf_ref, bf_ref, ab_ref, amaxb_ref, bb_ref, hf_ref, hb_ref, c_ref, m_ref)


def _mlstm_chunk(fs, bs, qtf_ref, kf_ref, vtf_ref, qtb_ref, kb_ref, vtb_ref,
                 af_ref, amaxf_ref, bf_ref, ab_ref, amaxb_ref, bb_ref, hf_ref, hb_ref, c_ref, m_ref):
    fwd_rows = lax.broadcasted_iota(jnp.int32, (N_CHAIN, L), 0) < ML_HEADS
    a = jnp.where(fwd_rows, af_ref[:, fs], ab_ref[:, bs])
    amax = jnp.where(fwd_rows, amaxf_ref[:, fs], amaxb_ref[:, bs])
    b = jnp.where(fwd_rows, bf_ref[:, fs], bb_ref[:, bs])
    m_old = m_ref[:, 0:1]
    big = jnp.maximum(m_old, jnp.max(amax, axis=1, keepdims=True))
    decay = jnp.exp(m_old - big)
    e = jnp.exp(a - big)
    mm = jnp.maximum(amax, m_old)
    w_inter = jnp.exp(m_old - mm)
    floor = jnp.exp(-(b + mm))
    m_ref[...] = jnp.broadcast_to(jnp.min(b, axis=1, keepdims=True) + big, (N_CHAIN, 128))
    kq = lax.broadcasted_iota(jnp.int32, (L, L), 0)
    qq = lax.broadcasted_iota(jnp.int32, (L, L), 1)
    a_col = jnp.concatenate([a, jnp.zeros((L - N_CHAIN, L), F32)], axis=0).T

    for d, (qt_ref, k_ref, vt_ref, h_ref, ts) in enumerate(
            ((qtf_ref, kf_ref, vtf_ref, hf_ref, fs), (qtb_ref, kb_ref, vtb_ref, hb_ref, bs))):
        visible = (kq >= qq) if d == 1 else (kq <= qq)
        for hd in range(ML_HEADS):
            c = d * ML_HEADS + hd
            cs = slice(hd * ML_DH, (hd + 1) * ML_DH)
            qt = qt_ref[cs, ts]
            k = k_ref[ts, cs]
            vt = vt_ref[cs, ts]
            p = jnp.where(visible, jnp.exp(a_col[:, c:c + 1] - mm[c:c + 1, :]), 0.0)
            c_old = c_ref[c]
            kcq = _dot(jnp.concatenate([k, c_old.astype(BF16)], axis=0), qt)
            st = kcq[0:L] * p
            cq = kcq[L:]
            wi = w_inter[c:c + 1, :]
            num = _dot(vt, st.astype(BF16)) + wi * cq[0:ML_DH]
            den = jnp.sum(st, axis=0, keepdims=True) + wi * cq[ML_DH:ML_DH + 1]
            ht = num * (1.0 / jnp.maximum(jnp.abs(den), floor[c:c + 1, :]))
            h_ref[ts, cs] = ht.T.astype(BF16)
            er = e[c:c + 1, :]
            vte = jnp.concatenate([vt.astype(F32) * er, jnp.broadcast_to(er, (N_AUG, L))], axis=0)
            c_ref[c] = decay[c:c + 1, :] * c_old + _dot(vte.astype(BF16), k)


def _mlstm(qt, k, vt, a, amax, b):
    lat_chunks = T_LAT // ML_BLK

    def fwd_in(b, t):
        return jnp.where(t < N_CTX_CHUNKS, lat_chunks + b * N_CTX_CHUNKS + t,
                         b * N_LAT_CHUNKS + (t - N_CTX_CHUNKS))

    def bwd_in(b, t):
        return jnp.where(t < N_CTX_CHUNKS, lat_chunks + b * N_CTX_CHUNKS + (N_CTX_CHUNKS - 1 - t),
                         b * N_LAT_CHUNKS + (N_STEPS - 1 - t))

    def fwd_out(b, t):
        return b * N_LAT_CHUNKS + jnp.maximum(t - N_CTX_CHUNKS, 0)

    def bwd_out(b, t):
        return b * N_LAT_CHUNKS + (N_STEPS - 1 - jnp.maximum(t, N_CTX_CHUNKS))

    rows = lambda f: pl.BlockSpec((ML_BLK, ML_DIM), lambda b, t: (f(b, t), 0))
    cols = lambda f: pl.BlockSpec((ML_DIM, ML_BLK), lambda b, t: (0, f(b, t)))
    gate = lambda f: pl.BlockSpec((N_CHAIN, ML_BLK), lambda b, t: (0, f(b, t)))
    return pl.pallas_call(
        _mlstm_kernel,
        grid=(BATCH, N_STEPS),
        in_specs=[cols(fwd_in), rows(fwd_in), cols(fwd_in),
                  cols(bwd_in), rows(bwd_in), cols(bwd_in),
                  gate(fwd_in), gate(fwd_in), gate(fwd_in), gate(bwd_in), gate(bwd_in), gate(bwd_in)],
        out_specs=[rows(fwd_out), rows(bwd_out)],
        out_shape=[jax.ShapeDtypeStruct((T_LAT, ML_DIM), BF16),
                   jax.ShapeDtypeStruct((T_LAT, ML_DIM), BF16)],
        scratch_shapes=[pltpu.VMEM((N_CHAIN, ML_DH + N_AUG, ML_DH), F32),
                        pltpu.VMEM((N_CHAIN, 128), F32)],
        compiler_params=_params(2),
        name="mlstm",
    )(qt, k, vt, qt, k, vt, a, amax, b, a, amax, b)


def _odd_out_kernel(x_ref, m_ref, g_ref, hf_ref, hb_ref, wot_ref, ng_ref, wout_ref, o_ref):
    x = x_ref[...]
    h = _rms_mod(x, g_ref[...], m_ref[:, 0:D], m_ref[:, D:2 * D]).astype(BF16)
    o = _sigmoid(_dot_nt(h, wot_ref[...]))
    hs = hf_ref[...].astype(F32) + hb_ref[...].astype(F32)
    parts = []
    for hd in range(ML_HEADS):
        p = hs[:, hd * ML_DH:(hd + 1) * ML_DH]
        parts.append(p * lax.rsqrt(jnp.mean(p * p, axis=-1, keepdims=True) + EPS))
    hn = jnp.concatenate(parts, axis=-1) * ng_ref[...]
    y = _dot((o * hn).astype(BF16), wout_ref[...])
    o_ref[...] = x + m_ref[:, 2 * D:3 * D] * y


def _odd_out(x, mods, norm_g, hf, hb, w_in_t, head_g, w_out):
    row = lambda i: (i, 0)
    return pl.pallas_call(
        _odd_out_kernel,
        grid=(N_LAT_TILES,),
        in_specs=[
            pl.BlockSpec((TM, D), row),
            _mod_spec(1),
            _resident((1, D)),
            pl.BlockSpec((TM, ML_DIM), row),
            pl.BlockSpec((TM, ML_DIM), row),
            _w_in_t_block(3),
            _resident((1, ML_DIM)),
            _resident((ML_DIM, D)),
        ],
        out_specs=pl.BlockSpec((TM, D), row),
        out_shape=jax.ShapeDtypeStruct((T_LAT, D), F32),
        compiler_params=_params(1),
        name="odd_out",
    )(x, mods, norm_g.reshape(1, D), hf, hb, w_in_t, head_g.reshape(1, ML_DIM), w_out)


def kernel(x, c, ctx, c_ctx, mod_w, mod_b, ffn1_norm, ffn1_w_gu, ffn1_w_d, mix_norm, ffn2_norm,
           ffn2_w_gu, ffn2_w_d, ev_w_in, ev_conv_w, ev_conv_b, ev_conv_ln_g, ev_conv_ln_b, ev_sink,
           ev_w_out, od_w_in, od_b_gate, od_norm_g, od_w_out, final_norm):
    assert DEPTH == 2 and x.shape == (BATCH, SEQ, D) and ctx.shape == (BATCH, CTX, D)
    cond = jnp.zeros((MOD_ROWS, D), F32).at[:BATCH].set(c).at[BATCH].set(c_ctx)
    mods = _ada_mods(cond, mod_w, mod_b).reshape(DEPTH, MOD_ROWS, 1, N_MOD * D)
    bf = lambda w: w.astype(BF16)

    m0 = mods[0]
    xs = _ffn(x.reshape(T_LAT, D), m0, 0, ffn1_norm[0], ffn1_w_gu, ffn1_w_d, 0, N_ALL_TILES,
              x_ctx=ctx.reshape(T_CTX, D))
    w_in = ev_w_in[0]
    rope, rope_t = _rope_tables()
    u, qt, kk, vt = _even_in(xs, m0, mix_norm[0], bf(w_in).T, rope, rope_t)
    sink = jnp.repeat(ev_sink[0].astype(F32), BLK).reshape(1, ATT_HEADS * BLK)
    att = _attention(qt, kk, vt, sink)
    conv_w = jnp.concatenate([ev_conv_w[0], jnp.zeros((1, CONV_DIM), F32)], axis=0)
    ev_args = (conv_w, ev_conv_b[0].reshape(1, -1), ev_conv_ln_g[0].reshape(1, -1),
               ev_conv_ln_b[0].reshape(1, -1), bf(ev_w_out[0][:CONV_DIM]), bf(ev_w_out[0][CONV_DIM:]))
    xs = _even_out(xs, m0, u, att, *ev_args, tm=TM, row0=0, n_tiles=N_LAT_TILES,
                   tiles_per_seq=SEQ // TM, mod_row=lambda i: i // (SEQ // TM))
    xs = _even_out(xs, m0, u, att, *ev_args, tm=CTX, row0=T_LAT, n_tiles=BATCH,
                   tiles_per_seq=1, mod_row=lambda i: BATCH)
    xs = _ffn(xs, m0, 2, ffn2_norm[0], ffn2_w_gu, ffn2_w_d, 0, N_ALL_TILES)

    m1 = mods[1]
    xs = _ffn(xs, m1, 0, ffn1_norm[1], ffn1_w_gu, ffn1_w_d, 1, N_ALL_TILES)
    w_in = od_w_in[0]
    perm = np.concatenate([np.arange(0, 4), np.arange(8, 12), np.arange(4, 8), np.arange(12, 16)])
    w_gate_t = w_in[:, 4 * ML_DIM:].T[perm]
    b_gate_t = od_b_gate[0][perm].reshape(N_GATE, 1)
    w_gate_hi = bf(w_gate_t)
    w_gate_lo = bf(w_gate_t - w_gate_hi.astype(F32))
    w_in_t = bf(w_in[:, :4 * ML_DIM]).T
    w_qg_t = jnp.concatenate([w_in_t[:ML_DIM], w_gate_hi, w_gate_lo], axis=0)
    qt, km, vt, ga, gamax, gb = _odd_in(xs, m1, mix_norm[1], w_qg_t, w_in_t, b_gate_t)
    hf, hb = _mlstm(qt, km, vt, ga, gamax, gb)
    xl = _odd_out(xs, m1, mix_norm[1], hf, hb, w_in_t, od_norm_g[0], bf(od_w_out[0]))
    out = _ffn(xl, m1, 2, ffn2_norm[1], ffn2_w_gu, ffn2_w_d, 1, N_LAT_TILES,
               final_g=final_norm)
    return out.reshape(BATCH, SEQ, D)
```
